```python
import jax, jax.numpy as jnp
from jax import lax
import numpy as np

D_MODEL = 1024
BATCH = 8
SEQ = 2048
DEPTH = 2
DEC_BATCH = 128
DEC_SEQ = 8
PAST_LEN = 8192
PAGE_SIZE = 128

BRANCH_W = 512
N_BRANCH = 4
A_HEADS = 4
A_DK = 128
A_DV = 128
B_HEADS = 8
B_Q_RANK = 384
B_KV_RANK = 256
B_NOPE = 64
B_ROPE = 32
B_VDIM = 64
C_CH = 512
C_WIDTH = 31
D_HEADS = 4
D_DK = 128
D_DV = 128
FFN_HIDDEN = -(-8 * D_MODEL // (3 * 256)) * 256
CHUNK = 128
Q_BLOCK = 128
ROPE_BASE = 10000.0
LN_EPS = 1e-5
RMS_EPS = 1e-6
DEEPNORM_ALPHA = (2 * DEPTH) ** 0.25
DEEPNORM_BETA = (8 * DEPTH) ** -0.25
CACHE_W = B_KV_RANK + B_ROPE
IN_SIZES = (A_HEADS * A_DK, A_HEADS * A_DK, A_HEADS * A_DV, A_HEADS * A_DV, A_HEADS, A_HEADS,
            B_Q_RANK, B_KV_RANK, B_ROPE,
            2 * C_CH,
            D_HEADS * D_DK, D_HEADS * D_DK, D_HEADS * D_DV, D_HEADS * D_DV,
            N_BRANCH * D_MODEL)
IN_OFFSETS = tuple(int(o) for o in np.cumsum(IN_SIZES)[:-1])
N_IN = int(sum(IN_SIZES))
A_F_OFFSET = IN_OFFSETS[4]

kernel_name = 'hybrid_mlstm_mla_conv_retention_decode_step'


def layer_norm(x, g=None, b=None):
    xf = x.astype(jnp.float32)
    mu = jnp.mean(xf, axis=-1, keepdims=True)
    var = jnp.mean(jnp.square(xf - mu), axis=-1, keepdims=True)
    y = (xf - mu) * lax.rsqrt(var + LN_EPS)
    if g is not None:
        y = y * g + b
    return y


def rms_norm(x, g):
    xf = x.astype(jnp.float32)
    return xf * lax.rsqrt(jnp.mean(xf * xf, axis=-1, keepdims=True) + RMS_EPS) * g


def rotary(x, pos):
    d = x.shape[-1]
    inv = ROPE_BASE ** (-jnp.arange(0, d, 2, dtype=jnp.float32) / d)
    ang = pos.astype(jnp.float32)[:, None] * inv[None, :]
    cos = jnp.cos(ang)[None, :, None, :]
    sin = jnp.sin(ang)[None, :, None, :]
    x1, x2 = jnp.split(x.astype(jnp.float32), 2, axis=-1)
    return jnp.concatenate([x1 * cos - x2 * sin, x1 * sin + x2 * cos], axis=-1)


def to_chunks(t, L):
    B, T = t.shape[:2]
    return jnp.moveaxis(t.reshape((B, T // L, L) + t.shape[2:]), 1, 0)


def from_chunks(t):
    nc, B, L = t.shape[:3]
    return jnp.moveaxis(t, 0, 1).reshape((B, nc * L) + t.shape[3:])


def mlstm_chunk(carry, inp):
    C, n, m = carry
    q, k, v, ig, lf = inp
    L = q.shape[1]
    b = jnp.cumsum(lf, axis=1)
    inter = b + m[:, None, :]
    intra = b[:, :, None, :] - b[:, None, :, :] + ig[:, None, :, :]
    causal = jnp.tril(jnp.ones((L, L), dtype=bool))[None, :, :, None]
    intra = jnp.where(causal, intra, -jnp.inf)
    m_t = jnp.maximum(inter, jnp.max(intra, axis=2))
    w_inter = jnp.exp(inter - m_t)
    s = jnp.einsum('bqhd,bshd->bqsh', q, k) * jnp.exp(intra - m_t[:, :, None, :])
    num = jnp.einsum('bqsh,bshv->bqhv', s, v) + w_inter[..., None] * jnp.einsum('bqhd,bhdv->bqhv', q, C)
    den = jnp.sum(s, axis=2) + w_inter * jnp.einsum('bqhd,bhd->bqh', q, n)
    h = num / jnp.maximum(jnp.abs(den), jnp.exp(-m_t))[..., None]
    bL = b[:, -1]
    m_new = m_t[:, -1]
    w_c = jnp.exp(bL + m - m_new)
    w_s = jnp.exp(bL[:, None, :] - b + ig - m_new[:, None, :])
    C_new = w_c[..., None, None] * C + jnp.einsum('bsh,bshd,bshv->bhdv', w_s, k, v)
    n_new = w_c[..., None] * n + jnp.einsum('bsh,bshd->bhd', w_s, k)
    return (C_new, n_new, m_new), h


def mlstm(q, k, v, ig, lf, C0, n0, m0):
    L = min(CHUNK, q.shape[1])
    xs = (to_chunks(q, L), to_chunks(k, L), to_chunks(v, L), to_chunks(ig, L), to_chunks(lf, L))
    (C, n, m), h = lax.scan(mlstm_chunk, (C0, n0, m0), xs)
    return from_chunks(h), C, n, m


def retention(q, k, v, S0):
    log_gamma = jnp.log(1.0 - 2.0 ** (-5.0 - jnp.arange(D_HEADS, dtype=jnp.float32)))
    L = min(CHUNK, q.shape[1])
    j = jnp.arange(L, dtype=jnp.float32)
    diff = j[:, None] - j[None, :]
    decay = jnp.where((diff >= 0)[..., None], jnp.exp(jnp.maximum(diff, 0.0)[..., None] * log_gamma), 0.0)
    w_in = jnp.exp((j + 1.0)[:, None] * log_gamma)
    w_st = jnp.exp((L - 1.0 - j)[:, None] * log_gamma)
    w_S = jnp.exp(L * log_gamma)

    def chunk(S, inp):
        qc, kc, vc = inp
        s = jnp.einsum('bqhd,bshd->bqsh', qc, kc) * decay[None]
        o = jnp.einsum('bqsh,bshv->bqhv', s, vc) + w_in[None, :, :, None] * jnp.einsum('bqhd,bhdv->bqhv', qc, S)
        S_new = w_S[None, :, None, None] * S + jnp.einsum('sh,bshd,bshv->bhdv', w_st, kc, vc)
        return S_new, o

    S, o = lax.scan(chunk, S0, (to_chunks(q, L), to_chunks(k, L), to_chunks(v, L)))
    return from_chunks(o), S


def conv_module(glu_in, buf, w, b, g, beta):
    glu_in = glu_in.astype(jnp.float32)
    u = glu_in[..., :C_CH] * jax.nn.sigmoid(glu_in[..., C_CH:])
    full = jnp.concatenate([buf.astype(jnp.float32), u], axis=1)
    y = lax.conv_general_dilated(full, w.astype(jnp.float32)[:, None, :], window_strides=(1,), padding='VALID',
                                 dimension_numbers=('NWC', 'WIO', 'NWC'), feature_group_count=C_CH)
    y = jax.nn.silu(layer_norm(y + b, g, beta))
    return y, full[:, -(C_WIDTH - 1):]


def mla_prompt_attention(q_abs, q_pe, ckv, kpe):
    T = q_abs.shape[1]
    L = min(Q_BLOCK, T)
    scale = (B_NOPE + B_ROPE) ** -0.5
    kpos = jnp.arange(T)

    def block(args):
        ql, qp, i = args
        qpos = i * L + jnp.arange(L)
        s = (jnp.einsum('bqhr,bkr->bhqk', ql, ckv) + jnp.einsum('bqhp,bkp->bhqk', qp, kpe)) * scale
        s = jnp.where((kpos[None, :] <= qpos[:, None])[None, None], s, -jnp.inf)
        p = jax.nn.softmax(s.astype(jnp.float32), axis=-1)
        return jnp.einsum('bhqk,bkr->bqhr', p, ckv)

    o = lax.map(block, (to_chunks(q_abs, L), to_chunks(q_pe, L), jnp.arange(T // L)))
    return from_chunks(o)


def mla_sample_attention(q_abs, q_pe, ckv, kpe, past):
    T = q_abs.shape[1]
    P = past.shape[1]
    scale = (B_NOPE + B_ROPE) ** -0.5
    pc = past[..., :B_KV_RANK].astype(jnp.float32)
    pp = past[..., B_KV_RANK:].astype(jnp.float32)
    s_past = (jnp.einsum('bqhr,bkr->bhqk', q_abs, pc) + jnp.einsum('bqhp,bkp->bhqk', q_pe, pp)) * scale
    s_new = (jnp.einsum('bqhr,bkr->bhqk', q_abs, ckv) + jnp.einsum('bqhp,bkp->bhqk', q_pe, kpe)) * scale
    s_new = jnp.where(jnp.tril(jnp.ones((T, T), dtype=bool))[None, None], s_new, -jnp.inf)
    p = jax.nn.softmax(jnp.concatenate([s_past, s_new], axis=-1).astype(jnp.float32), axis=-1)
    return jnp.einsum('bhqk,bkr->bqhr', p[..., :P], pc) + jnp.einsum('bhqk,bkr->bqhr', p[..., P:], ckv)


def mixer_branches(u, pos, lw, st, mla_past):
    B, T, _ = u.shape
    f32 = jnp.float32
    C0, n0, m0, buf0, S0 = st
    z = u @ lw['w_in'] + lw['b_in']
    (a_q, a_k, a_v, a_o, a_i, a_f, b_q, b_kv, b_kr, c_glu,
     d_q, d_k, d_v, d_g, gates) = jnp.split(z, IN_OFFSETS, axis=-1)

    def heads(t, h):
        return t.reshape(B, T, h, -1).astype(f32)

    qa = heads(a_q, A_HEADS) * A_DK ** -0.5
    h_a, C1, n1, m1 = mlstm(qa, heads(a_k, A_HEADS), heads(a_v, A_HEADS), a_i.astype(f32),
                            jax.nn.log_sigmoid(a_f.astype(f32)), C0, n0, m0)
    y_a = jax.nn.sigmoid(a_o) * layer_norm(h_a).reshape(B, T, -1)

    qb = (rms_norm(b_q, lw['q_norm']) @ lw['w_uq']).reshape(B, T, B_HEADS, B_NOPE + B_ROPE)
    q_pe = rotary(qb[..., B_NOPE:], pos)
    q_abs = jnp.einsum('bthn,rhn->bthr', qb[..., :B_NOPE], lw['w_uk'])
    ckv = rms_norm(b_kv, lw['kv_norm'])
    kpe = rotary(b_kr[:, :, None, :], pos)[:, :, 0, :]
    new_rows = jnp.concatenate([ckv, kpe], axis=-1)
    if mla_past is None:
        o_lat = mla_prompt_attention(q_abs, q_pe, ckv, kpe)
    else:
        o_lat = mla_sample_attention(q_abs, q_pe, ckv, kpe, mla_past)
    y_b = jnp.einsum('bthr,rhv->bthv', o_lat, lw['w_uv']).reshape(B, T, -1)

    y_c, buf1 = conv_module(c_glu, buf0, lw['conv_w'], lw['conv_b'], lw['conv_ln_g'], lw['conv_ln_b'])

    qd = rotary(heads(d_q, D_HEADS), pos)
    kd = rotary(heads(d_k, D_HEADS), pos) * D_DK ** -0.5
    o_d, S1 = retention(qd, kd, heads(d_v, D_HEADS), S0)
    y_d = jax.nn.silu(d_g) * layer_norm(o_d).reshape(B, T, -1)

    ys = jnp.stack([y_a, y_b, y_c, y_d], axis=2)
    proj = jnp.einsum('btnw,nwd->btnd', ys, lw['w_branch'])
    g = jax.nn.sigmoid(gates.reshape(B, T, N_BRANCH, D_MODEL))
    out = jnp.sum(g * proj, axis=2) @ lw['w_out']
    return out, (new_rows, C1, n1, m1, buf1, S1)


def trunk_layer(x, c, pos, lw, st, mla_past):
    mod = jax.nn.silu(c) @ lw['w_ada'] + lw['b_ada']
    sh1, sc1, g1, sh2, sc2, g2 = jnp.split(mod[:, None, :], 6, axis=-1)
    u = x * (1.0 + sc1) + sh1
    mix, new_st = mixer_branches(u, pos, lw, st, mla_past)
    x = layer_norm(DEEPNORM_ALPHA * x + g1 * mix, lw['ln1_g'], lw['ln1_b'])
    u2 = x * (1.0 + sc2) + sh2
    f = (jax.nn.silu(u2 @ lw['w1']) * (u2 @ lw['w3'])) @ lw['w2']
    x = layer_norm(DEEPNORM_ALPHA * x + g2 * f, lw['ln2_g'], lw['ln2_b'])
    return x, new_st


def setup_inputs(seed: int = 0) -> dict:
    key = jax.random.key(seed)
    counter = [0]

    def nk():
        counter[0] += 1
        return jax.random.fold_in(key, counter[0])

    def nrm(shape, scale=1.0):
        return jax.random.normal(nk(), shape, jnp.float32) * scale

    def gain(shape):
        return 1.0 + nrm(shape, 0.02)

    n_pages = PAST_LEN // PAGE_SIZE
    n_pool = (DEC_BATCH * n_pages * 5) // 4
    page_table = jax.random.permutation(nk(), n_pool)[: DEC_BATCH * n_pages].reshape(DEC_BATCH, n_pages).astype(jnp.int32)
    b_in = nrm((DEPTH, N_IN), 0.02).at[:, A_F_OFFSET:A_F_OFFSET + A_HEADS].add(jnp.linspace(3.0, 6.0, A_HEADS))
    return {
        'x_prompt': nrm((BATCH, SEQ, D_MODEL)),
        'x_sample': nrm((DEC_BATCH, DEC_SEQ, D_MODEL)),
        'cache_mla': nrm((DEPTH, n_pool, PAGE_SIZE, CACHE_W)),
        'state_mlstm_C': nrm((DEPTH, DEC_BATCH, A_HEADS, A_DK, A_DV), 0.5),
        'state_mlstm_n': nrm((DEPTH, DEC_BATCH, A_HEADS, A_DK), 0.5),
        'state_mlstm_m': nrm((DEPTH, DEC_BATCH, A_HEADS), 0.5),
        'state_conv': nrm((DEPTH, DEC_BATCH, C_WIDTH - 1, C_CH), 0.5),
        'state_ret': nrm((DEPTH, DEC_BATCH, D_HEADS, D_DK, D_DV)),
        'page_table': page_table,
        'c_prompt': nrm((BATCH, D_MODEL)),
        'c_sample': nrm((DEC_BATCH, D_MODEL)),
        'w_ada': nrm((DEPTH, D_MODEL, 6 * D_MODEL), 0.5 * D_MODEL ** -0.5),
        'b_ada': nrm((DEPTH, 6 * D_MODEL), 0.02),
        'w_in': nrm((DEPTH, D_MODEL, N_IN), D_MODEL ** -0.5),
        'b_in': b_in,
        'mla_q_norm': gain((DEPTH, B_Q_RANK)),
        'mla_w_uq': nrm((DEPTH, B_Q_RANK, B_HEADS * (B_NOPE + B_ROPE)), B_Q_RANK ** -0.5),
        'mla_kv_norm': gain((DEPTH, B_KV_RANK)),
        'mla_w_uk': nrm((DEPTH, B_KV_RANK, B_HEADS, B_NOPE), B_KV_RANK ** -0.5),
        'mla_w_uv': nrm((DEPTH, B_KV_RANK, B_HEADS, B_VDIM), B_KV_RANK ** -0.5),
        'conv_w': nrm((DEPTH, C_WIDTH, C_CH), C_WIDTH ** -0.5),
        'conv_b': nrm((DEPTH, C_CH), 0.02),
        'conv_ln_g': gain((DEPTH, C_CH)),
        'conv_ln_b': nrm((DEPTH, C_CH), 0.02),
        'w_branch': nrm((DEPTH, N_BRANCH, BRANCH_W, D_MODEL), BRANCH_W ** -0.5),
        'w_out': nrm((DEPTH, D_MODEL, D_MODEL), DEEPNORM_BETA * D_MODEL ** -0.5),
        'ln1_g': gain((DEPTH, D_MODEL)),
        'ln1_b': nrm((DEPTH, D_MODEL), 0.02),
        'w_ffn1': nrm((DEPTH, D_MODEL, FFN_HIDDEN), D_MODEL ** -0.5),
        'w_ffn3': nrm((DEPTH, D_MODEL, FFN_HIDDEN), D_MODEL ** -0.5),
        'w_ffn2': nrm((DEPTH, FFN_HIDDEN, D_MODEL), DEEPNORM_BETA * FFN_HIDDEN ** -0.5),
        'ln2_g': gain((DEPTH, D_MODEL)),
        'ln2_b': nrm((DEPTH, D_MODEL), 0.02),
    }


def reference(x_prompt, x_sample, cache_mla, state_mlstm_C, state_mlstm_n, state_mlstm_m, state_conv, state_ret,
              page_table, c_prompt, c_sample, w_ada, b_ada, w_in, b_in, mla_q_norm, mla_w_uq, mla_kv_norm,
              mla_w_uk, mla_w_uv, conv_w, conv_b, conv_ln_g, conv_ln_b, w_branch, w_out, ln1_g, ln1_b,
              w_ffn1, w_ffn3, w_ffn2, ln2_g, ln2_b):
    f32 = jnp.float32
    dt = x_prompt.dtype
    Bp, Tp = x_prompt.shape[:2]
    Bd, Td = x_sample.shape[:2]
    past_len = page_table.shape[1] * PAGE_SIZE
    pos_p = jnp.arange(Tp)
    pos_d = past_len + jnp.arange(Td)
    xp, xd = x_prompt, x_sample
    p_states = [[] for _ in range(6)]
    d_states = [[] for _ in range(6)]
    for l in range(DEPTH):
        lw = {'w_ada': w_ada[l], 'b_ada': b_ada[l], 'w_in': w_in[l], 'b_in': b_in[l],
              'q_norm': mla_q_norm[l], 'w_uq': mla_w_uq[l], 'kv_norm': mla_kv_norm[l],
              'w_uk': mla_w_uk[l], 'w_uv': mla_w_uv[l], 'conv_w': conv_w[l], 'conv_b': conv_b[l],
              'conv_ln_g': conv_ln_g[l], 'conv_ln_b': conv_ln_b[l], 'w_branch': w_branch[l],
              'w_out': w_out[l], 'ln1_g': ln1_g[l], 'ln1_b': ln1_b[l], 'w1': w_ffn1[l], 'w3': w_ffn3[l],
              'w2': w_ffn2[l], 'ln2_g': ln2_g[l], 'ln2_b': ln2_b[l]}
        st_p = (jnp.zeros((Bp, A_HEADS, A_DK, A_DV), f32), jnp.zeros((Bp, A_HEADS, A_DK), f32),
                jnp.zeros((Bp, A_HEADS), f32), jnp.zeros((Bp, C_WIDTH - 1, C_CH), f32),
                jnp.zeros((Bp, D_HEADS, D_DK, D_DV), f32))
        xp, new_p = trunk_layer(xp, c_prompt, pos_p, lw, st_p, None)
        past = cache_mla[l][page_table].reshape(Bd, past_len, cache_mla.shape[-1])
        st_d = (state_mlstm_C[l].astype(f32), state_mlstm_n[l].astype(f32), state_mlstm_m[l].astype(f32),
                state_conv[l].astype(f32), state_ret[l].astype(f32))
        xd, new_d = trunk_layer(xd, c_sample, pos_d, lw, st_d, past)
        for i in range(6):
            p_states[i].append(new_p[i])
            d_states[i].append(new_d[i])
    ps = [jnp.stack(s, axis=0).astype(dt) for s in p_states]
    ds = [jnp.stack(s, axis=0).astype(dt) for s in d_states]
    y_prompt = xp.astype(dt)
    y_sample = xd.astype(dt)
    return (y_prompt, y_sample, ps[0], ds[0], ps[1], ds[1], ps[2], ds[2], ps[3], ds[3], ps[4], ds[4], ps[5], ds[5])
```

```python
import functools

import numpy as np
import jax
import jax.numpy as jnp
from jax import lax
from jax.experimental import pallas as pl
from jax.experimental.pallas import tpu as pltpu

F32 = jnp.float32
BF16 = jnp.bfloat16

D_MODEL = 1024
DEPTH = 2
PAGE_SIZE = 128
N_BRANCH = 4
A_HEADS, A_DK, A_DV = 4, 128, 128
B_HEADS, B_Q_RANK, B_KV_RANK, B_NOPE, B_ROPE, B_VDIM = 8, 384, 256, 64, 32, 64
C_CH, C_WIDTH = 512, 31
D_HEADS, D_DK, D_DV = 4, 128, 128
FFN_HIDDEN = -(-8 * D_MODEL // (3 * 256)) * 256
CHUNK = 128
ROPE_BASE = 10000.0
LN_EPS = 1e-5
RMS_EPS = 1e-6
DEEPNORM_ALPHA = (2 * DEPTH) ** 0.25
CACHE_W = B_KV_RANK + B_ROPE
MLA_SCALE = (B_NOPE + B_ROPE) ** -0.5

O_AQ, O_AI, O_AF = 0, 2048, 2052
O_BQ, O_BKV, O_BKR = 2056, 2440, 2696
O_C, O_D, O_G = 2728, 3752, 5800
N_IN = 9896

Z_G, Z_A, Z_D, Z_C, Z_B = 0, 4096, 6144, 8192, 9216
Z_W = 9984
Z_BW = 768
MISC_KR, MISC_AI, MISC_AF, MISC_KRSW = 0, 32, 36, 64
LANE = 128
HP = 128

VMEM_LIMIT = 56 * 1024 * 1024


def _cp(sem, vmem=VMEM_LIMIT):
    return pltpu.CompilerParams(dimension_semantics=sem, vmem_limit_bytes=vmem)


def _sigmoid(x):
    return 1.0 / (1.0 + jnp.exp(-x))


def _silu(x):
    return x * _sigmoid(x)


def _log_sigmoid(x):
    return jnp.minimum(x, 0.0) - jnp.log(1.0 + jnp.exp(-jnp.abs(x)))


def _ln_rows(x):
    mu = jnp.mean(x, axis=-1, keepdims=True)
    xc = x - mu
    var = jnp.mean(xc * xc, axis=-1, keepdims=True)
    return xc * lax.rsqrt(var + LN_EPS)


def _rms_rows(x):
    return x * lax.rsqrt(jnp.mean(x * x, axis=-1, keepdims=True) + RMS_EPS)


def _dot(a, b):
    return jnp.dot(a, b, preferred_element_type=F32)


def _dot_nt(a, b):
    return lax.dot_general(a, b, (((1,), (1,)), ((), ())), preferred_element_type=F32)


def _dot_tn(a, b):
    return lax.dot_general(a, b, (((0,), (0,)), ((), ())), preferred_element_type=F32)


def _tile(n, pref, align=8):
    if n <= pref:
        return n
    for t in range(pref, 0, -1):
        if n % t == 0 and t % align == 0:
            return t
    return n


def _mod_kernel(c_ref, w_ref, b_ref, o_ref):
    s = _silu(c_ref[...])
    o_ref[...] = _dot(s.astype(BF16), w_ref[...].astype(BF16)) + b_ref[...]


def _ada_mod(c, w, b):
    m, d = c.shape
    n = w.shape[1]
    tn = _tile(n, 768, LANE)
    return pl.pallas_call(
        _mod_kernel,
        out_shape=jax.ShapeDtypeStruct((m, n), F32),
        grid=(n // tn,),
        in_specs=[pl.BlockSpec((m, d), lambda j: (0, 0)),
                  pl.BlockSpec((d, tn), lambda j: (0, j)),
                  pl.BlockSpec((1, tn), lambda j: (0, j))],
        out_specs=pl.BlockSpec((m, tn), lambda j: (0, j)),
        compiler_params=_cp(("arbitrary",)),
        name="ada_mod",
    )(c, w, b.reshape(1, n))


def _inproj_kernel(x_ref, sc_ref, sh_ref, w_ref, b_ref, o_ref, u_ref):
    @pl.when(pl.program_id(1) == 0)
    def _():
        u = x_ref[...] * (1.0 + sc_ref[...]) + sh_ref[...]
        u_ref[...] = u.reshape(u_ref.shape).astype(BF16)

    o_ref[...] = _dot(u_ref[...], w_ref[...]) + b_ref[...]


def _token_blocks(B, T, rows):
    if T >= rows:
        return 1, _tile(T, rows)
    return _tile(B, max(1, rows // T), 1), T


def _in_proj(x, sc, sh, w, b):
    B, T, D = x.shape
    n = w.shape[1]
    bb, tt = _token_blocks(B, T, 1024)
    tm = bb * tt
    nt = T // tt
    tn = Z_BW
    grid = ((B // bb) * nt, n // tn)
    return pl.pallas_call(
        _inproj_kernel,
        out_shape=jax.ShapeDtypeStruct((B * T, n), F32),
        grid=grid,
        in_specs=[pl.BlockSpec((bb, tt, D), lambda i, j: (i // nt, i % nt, 0)),
                  pl.BlockSpec((bb, 1, D), lambda i, j: (i // nt, 0, 0)),
                  pl.BlockSpec((bb, 1, D), lambda i, j: (i // nt, 0, 0)),
                  pl.BlockSpec((D, tn), lambda i, j: (0, j)),
                  pl.BlockSpec((1, tn), lambda i, j: (0, j))],
        out_specs=pl.BlockSpec((tm, tn), lambda i, j: (i, j)),
        scratch_shapes=[pltpu.VMEM((tm, D), BF16)],
        compiler_params=_cp(("parallel", "arbitrary")),
        name="in_proj",
    )(x, sc, sh, w, b)


def _mlstm_kernel(q_ref, k_ref, v_ref, o_ref, g_ref, c0_ref, n0_ref, m0_ref,
                  y_ref, c_out, n_out, m_out, c_s, n_s, m_s, *, L, mxu):
    ci = pl.program_id(1)

    @pl.when(ci == 0)
    def _():
        c_s[...] = c0_ref[0]
        n_s[...] = n0_ref[0]
        m_s[...] = m0_ref[0]

    g = g_ref[0]
    gt = g.T
    row = lax.broadcasted_iota(jnp.int32, (L, L), 0)
    col = lax.broadcasted_iota(jnp.int32, (L, L), 1)
    tril = col <= row
    triu = row <= col
    for h in range(A_HEADS):
        hs = slice(h * A_DK, (h + 1) * A_DK)
        ig_row = gt[MISC_AI + h:MISC_AI + h + 1, :]
        lf_row = _log_sigmoid(gt[MISC_AF + h:MISC_AF + h + 1, :])
        ig_col = g[:, MISC_AI + h:MISC_AI + h + 1]
        lf_col = _log_sigmoid(g[:, MISC_AF + h:MISC_AF + h + 1])
        b_col = jnp.sum(jnp.where(tril, lf_row, 0.0), axis=1, keepdims=True)
        b_row = jnp.sum(jnp.where(triu, lf_col, 0.0), axis=0, keepdims=True)
        m_prev = m_s[h:h + 1, 0:1]
        inter = b_col + m_prev
        intra = jnp.where(tril, b_col - b_row + ig_row, -jnp.inf)
        m_t = jnp.maximum(inter, jnp.max(intra, axis=1, keepdims=True))
        w_inter = jnp.exp(inter - m_t)
        dmat = jnp.exp(intra - m_t)
        qh = q_ref[0, :, hs] * (A_DK ** -0.5)
        kh = k_ref[0, :, hs]
        vh = v_ref[0, :, hs]
        qb, kb, vb = qh.astype(mxu), kh.astype(mxu), vh.astype(mxu)
        s = _dot_nt(qb, kb) * dmat
        c_prev = c_s[h]
        n_prev = n_s[h:h + 1, :]
        num = _dot(s.astype(mxu), vb) + w_inter * _dot(qb, c_prev.astype(mxu))
        den = jnp.sum(s, axis=1, keepdims=True) + w_inter * jnp.sum(qh * n_prev, axis=1, keepdims=True)
        hv = num / jnp.maximum(jnp.abs(den), jnp.exp(-m_t))
        y_ref[0, :, hs] = (_sigmoid(o_ref[0, :, hs]) * _ln_rows(hv)).astype(y_ref.dtype)
        b_last = b_col[L - 1:L, :]
        m_new = m_t[L - 1:L, :]
        w_c = jnp.exp(b_last + m_prev - m_new)
        w_s = jnp.exp(b_last - b_col + ig_col - m_new)
        kw = kh * w_s
        c_s[h] = w_c * c_prev + _dot_tn(kw.astype(mxu), vb)
        n_s[h:h + 1, :] = w_c * n_prev + jnp.sum(kw, axis=0, keepdims=True)
        m_s[h:h + 1, :] = jnp.broadcast_to(m_new, (1, LANE))

    @pl.when(ci == pl.num_programs(1) - 1)
    def _():
        c_out[0] = c_s[...]
        n_out[0] = n_s[...]
        m_out[0] = m_s[...]


def _mlstm(z3, c0, n0, m0, ydt):
    B, T, _ = z3.shape
    L = min(CHUNK, T)
    nc = T // L
    H = A_HEADS
    W = H * A_DK
    ab = Z_A // W
    m0b = jnp.broadcast_to(m0[..., None], (B, H, LANE))
    mxu = BF16 if L >= 16 else F32

    def zspec(k):
        return pl.BlockSpec((1, L, W), lambda b, c: (b, c, ab + k))

    y, c1, n1, m1 = pl.pallas_call(
        functools.partial(_mlstm_kernel, L=L, mxu=mxu),
        out_shape=(jax.ShapeDtypeStruct((B, T, W), ydt),
                   jax.ShapeDtypeStruct((B, H, A_DK, A_DV), F32),
                   jax.ShapeDtypeStruct((B, H, A_DK), F32),
                   jax.ShapeDtypeStruct((B, H, LANE), F32)),
        grid=(B, nc),
        in_specs=[zspec(0), zspec(1), zspec(2), zspec(3),
                  pl.BlockSpec((1, L, LANE), lambda b, c: (b, c, (Z_B + Z_BW - LANE) // LANE)),
                  pl.BlockSpec((1, H, A_DK, A_DV), lambda b, c: (b, 0, 0, 0)),
                  pl.BlockSpec((1, H, A_DK), lambda b, c: (b, 0, 0)),
                  pl.BlockSpec((1, H, LANE), lambda b, c: (b, 0, 0))],
        out_specs=(pl.BlockSpec((1, L, W), lambda b, c: (b, c, 0)),
                   pl.BlockSpec((1, H, A_DK, A_DV), lambda b, c: (b, 0, 0, 0)),
                   pl.BlockSpec((1, H, A_DK), lambda b, c: (b, 0, 0)),
                   pl.BlockSpec((1, H, LANE), lambda b, c: (b, 0, 0))),
        scratch_shapes=[pltpu.VMEM((H, A_DK, A_DV), F32), pltpu.VMEM((H, A_DK), F32),
                        pltpu.VMEM((H, LANE), F32)],
        compiler_params=_cp(("parallel", "arbitrary")),
        name="mlstm",
    )(z3, z3, z3, z3, z3, c0, n0, m0b)
    return y, c1, n1, m1[..., 0]


_LOG_GAMMA = tuple(float(np.log(1.0 - 2.0 ** (-5.0 - h))) for h in range(D_HEADS))


def _ret_kernel(q_ref, k_ref, v_ref, g_ref, tab_ref, s0_ref, y_ref, s_out, s_s, *, L, mxu):
    ci = pl.program_id(1)

    @pl.when(ci == 0)
    def _():
        s_s[...] = s0_ref[0]

    cos = tab_ref[:, 0:D_DK]
    sin = tab_ref[:, D_DK:2 * D_DK]
    row = lax.broadcasted_iota(jnp.int32, (L, L), 0)
    col = lax.broadcasted_iota(jnp.int32, (L, L), 1)
    diff = (row - col).astype(F32)
    jcol = lax.broadcasted_iota(jnp.int32, (L, 1), 0).astype(F32)
    for h in range(D_HEADS):
        hs = slice(h * D_DK, (h + 1) * D_DK)
        lg = _LOG_GAMMA[h]
        decay = jnp.where(diff >= 0.0, jnp.exp(jnp.maximum(diff, 0.0) * lg), 0.0)
        w_in = jnp.exp((jcol + 1.0) * lg)
        w_st = jnp.exp((L - 1.0 - jcol) * lg)
        w_S = float(np.exp(L * lg))
        q = q_ref[0, :, hs]
        k = k_ref[0, :, hs]
        qh = q * cos + pltpu.roll(q, D_DK // 2, 1) * sin
        kh = (k * cos + pltpu.roll(k, D_DK // 2, 1) * sin) * (D_DK ** -0.5)
        vb = v_ref[0, :, hs].astype(mxu)
        qb = qh.astype(mxu)
        s = _dot_nt(qb, kh.astype(mxu)) * decay
        s_prev = s_s[h]
        o = _dot(s.astype(mxu), vb) + w_in * _dot(qb, s_prev.astype(mxu))
        y_ref[0, :, hs] = (_silu(g_ref[0, :, hs]) * _ln_rows(o)).astype(y_ref.dtype)
        s_s[h] = w_S * s_prev + _dot_tn((kh * w_st).astype(mxu), vb)

    @pl.when(ci == pl.num_programs(1) - 1)
    def _():
        s_out[0] = s_s[...]


def _retention(z3, tab, s0, ydt):
    B, T, _ = z3.shape
    L = min(CHUNK, T)
    nc = T // L
    H = D_HEADS
    W = H * D_DK
    db = Z_D // W
    mxu = BF16 if L >= 16 else F32

    def zspec(k):
        return pl.BlockSpec((1, L, W), lambda b, c: (b, c, db + k))

    return pl.pallas_call(
        functools.partial(_ret_kernel, L=L, mxu=mxu),
        out_shape=(jax.ShapeDtypeStruct((B, T, W), ydt),
                   jax.ShapeDtypeStruct((B, H, D_DK, D_DV), F32)),
        grid=(B, nc),
        in_specs=[zspec(0), zspec(1), zspec(2), zspec(3),
                  pl.BlockSpec((L, 2 * D_DK), lambda b, c: (c, 0)),
                  pl.BlockSpec((1, H, D_DK, D_DV), lambda b, c: (b, 0, 0, 0))],
        out_specs=(pl.BlockSpec((1, L, W), lambda b, c: (b, c, 0)),
                   pl.BlockSpec((1, H, D_DK, D_DV), lambda b, c: (b, 0, 0, 0))),
        scratch_shapes=[pltpu.VMEM((H, D_DK, D_DV), F32)],
        compiler_params=_cp(("parallel", "arbitrary")),
        name="retention",
    )(z3, z3, z3, z3, tab, s0)


_CPAD = 32


def _conv_kernel(glu_ref, buf_ref, w_ref, b_ref, g_ref, be_ref, y_ref, buf_out, ext, *, tt):
    ti = pl.program_id(1)
    lead = _CPAD - (C_WIDTH - 1)

    @pl.when(ti == 0)
    def _():
        ext[0:lead, :] = jnp.zeros((lead, C_CH), F32)
        ext[lead:_CPAD, :] = buf_ref[0]

    glu = glu_ref[0]
    ext[_CPAD:_CPAD + tt, :] = glu[:, :C_CH] * _sigmoid(glu[:, C_CH:])
    acc = jnp.zeros((tt, C_CH), F32)
    for j in range(C_WIDTH):
        acc = acc + ext[lead + j:lead + j + tt, :] * w_ref[j:j + 1, :]
    y = _ln_rows(acc + b_ref[...]) * g_ref[...] + be_ref[...]
    y_ref[0] = _silu(y).astype(y_ref.dtype)

    @pl.when(ti == pl.num_programs(1) - 1)
    def _():
        buf_out[0] = ext[tt + lead:tt + _CPAD, :]

    ext[0:_CPAD, :] = ext[tt:tt + _CPAD, :]


def _conv(z3, buf0, w, b, g, be, ydt):
    B, T, _ = z3.shape
    tt = _tile(T, 256)
    nt = T // tt
    vec = lambda a: a.reshape(1, C_CH)
    return pl.pallas_call(
        functools.partial(_conv_kernel, tt=tt),
        out_shape=(jax.ShapeDtypeStruct((B, T, C_CH), ydt),
                   jax.ShapeDtypeStruct((B, C_WIDTH - 1, C_CH), F32)),
        grid=(B, nt),
        in_specs=[pl.BlockSpec((1, tt, 2 * C_CH), lambda b, t: (b, t, Z_C // (2 * C_CH))),
                  pl.BlockSpec((1, C_WIDTH - 1, C_CH), lambda b, t: (b, 0, 0)),
                  pl.BlockSpec((C_WIDTH, C_CH), lambda b, t: (0, 0)),
                  pl.BlockSpec((1, C_CH), lambda b, t: (0, 0)),
                  pl.BlockSpec((1, C_CH), lambda b, t: (0, 0)),
                  pl.BlockSpec((1, C_CH), lambda b, t: (0, 0))],
        out_specs=(pl.BlockSpec((1, tt, C_CH), lambda b, t: (b, t, 0)),
                   pl.BlockSpec((1, C_WIDTH - 1, C_CH), lambda b, t: (b, 0, 0))),
        scratch_shapes=[pltpu.VMEM((tt + _CPAD, C_CH), F32)],
        compiler_params=_cp(("parallel", "arbitrary")),
        name="conv_module",
    )(z3, buf0, w, vec(b), vec(g), vec(be))


def _mla_common(zb_ref, tab, qn_ref, kvn_ref, wq1_ref, wq2_ref, rows_ref):
    zb = zb_ref[...].reshape(-1, Z_BW)
    bq = zb[:, 0:B_Q_RANK]
    bkv = zb[:, B_Q_RANK:B_Q_RANK + B_KV_RANK]
    misc = zb[:, B_Q_RANK + B_KV_RANK:]
    qn = (_rms_rows(bq) * qn_ref[...]).astype(BF16)
    ckv = _rms_rows(bkv) * kvn_ref[...]
    cos_q = jnp.concatenate([tab[:, 0:LANE]] * B_HEADS, axis=1)
    sin_q = jnp.concatenate([tab[:, LANE:2 * LANE]] * B_HEADS, axis=1)
    q = (_dot(qn, wq1_ref[...]) * cos_q + _dot(qn, wq2_ref[...]) * sin_q) * MLA_SCALE
    kpe = misc * tab[:, 2 * LANE:3 * LANE] + pltpu.roll(misc, LANE // 2, 1) * tab[:, 3 * LANE:]
    lead = rows_ref.shape[:-1]
    rows_ref[:, :, 0:B_KV_RANK] = ckv.reshape(lead + (B_KV_RANK,))
    rows_ref[:, :, B_KV_RANK:CACHE_W] = kpe[:, 0:B_ROPE].reshape(lead + (B_ROPE,))
    return q, ckv, kpe


def _mla_prep_prompt_kernel(zb_ref, tab_ref, qn_ref, kvn_ref, wq1_ref, wq2_ref, wk_ref, wv_ref,
                            q_ref, k_ref, v_ref, rows_ref):
    q, ckv, kpe = _mla_common(zb_ref, tab_ref[...], qn_ref, kvn_ref, wq1_ref, wq2_ref, rows_ref)
    q_ref[0] = q.astype(BF16)
    cb = ckv.astype(BF16)
    kpe_hi = pltpu.roll(kpe, LANE // 2, 1)
    k = _dot(cb, wk_ref[...]) + jnp.concatenate([kpe_hi] * B_HEADS, axis=1)
    k_ref[0] = k.astype(BF16)
    v_ref[0] = _dot(cb, wv_ref[...]).astype(BF16)


def _mla_prep_prompt(z3, tab, qn, kvn, wq1, wq2, wk, wv):
    B, T, _ = z3.shape
    tt = _tile(T, 512)
    nt = T // tt
    HW = B_HEADS * HP
    full = lambda a: pl.BlockSpec(a.shape, lambda b, t: (0,) * a.ndim)
    big = pl.BlockSpec((1, tt, HW), lambda b, t: (b, t, 0))
    return pl.pallas_call(
        _mla_prep_prompt_kernel,
        out_shape=(jax.ShapeDtypeStruct((B, T, HW), BF16),) * 3
        + (jax.ShapeDtypeStruct((B, T, CACHE_W), F32),),
        grid=(B, nt),
        in_specs=[pl.BlockSpec((1, tt, Z_BW), lambda b, t: (b, t, Z_B // Z_BW)),
                  pl.BlockSpec((tt, 4 * LANE), lambda b, t: (t, 0)),
                  full(qn), full(kvn), full(wq1), full(wq2), full(wk), full(wv)],
        out_specs=(big, big, big, pl.BlockSpec((1, tt, CACHE_W), lambda b, t: (b, t, 0))),
        compiler_params=_cp(("parallel", "parallel")),
        name="mla_prep_prompt",
    )(z3, tab, qn, kvn, wq1, wq2, wk, wv)


def _flash_kernel(q_ref, k_ref, v_ref, o_ref, *, tq, tk):
    qi = pl.program_id(2)
    q = q_ref[0]
    row = lax.broadcasted_iota(jnp.int32, (tq, tk), 0) + qi * tq
    col = lax.broadcasted_iota(jnp.int32, (tq, tk), 1)

    def body(kj, carry):
        m, l, acc = carry
        start = pl.multiple_of(kj * tk, tk)
        k = k_ref[0, pl.ds(start, tk), :]
        v = v_ref[0, pl.ds(start, tk), :]
        s = _dot_nt(q, k)
        s = jnp.where(col + kj * tk <= row, s, -jnp.inf)
        m_new = jnp.maximum(m, jnp.max(s, axis=1, keepdims=True))
        alpha = jnp.exp(m - m_new)
        p = jnp.exp(s - m_new)
        l = alpha * l + jnp.sum(p, axis=1, keepdims=True)
        acc = alpha * acc + _dot(p.astype(BF16), v)
        return m_new, l, acc

    n_k = (qi * tq + tq + tk - 1) // tk
    init = (jnp.full((tq, 1), -jnp.inf, F32), jnp.zeros((tq, 1), F32), jnp.zeros((tq, HP), F32))
    m, l, acc = lax.fori_loop(0, n_k, body, init)
    o_ref[0] = (acc / l).astype(o_ref.dtype)


def _flash(q, k, v):
    B, T, HW = q.shape
    tq = _tile(T, 256)
    tk = tq
    return pl.pallas_call(
        functools.partial(_flash_kernel, tq=tq, tk=tk),
        out_shape=jax.ShapeDtypeStruct((B, T, HW), BF16),
        grid=(B, B_HEADS, T // tq),
        in_specs=[pl.BlockSpec((1, tq, HP), lambda b, h, i: (b, i, h)),
                  pl.BlockSpec((1, T, HP), lambda b, h, i: (b, 0, h)),
                  pl.BlockSpec((1, T, HP), lambda b, h, i: (b, 0, h))],
        out_specs=pl.BlockSpec((1, tq, HP), lambda b, h, i: (b, i, h)),
        compiler_params=_cp(("parallel", "parallel", "arbitrary")),
        name="mla_prompt_attention",
    )(q, k, v)


def _mla_prep_sample_kernel(zb_ref, tab_ref, qn_ref, kvn_ref, wq1_ref, wq2_ref, wuk_ref,
                            q_ref, rows_ref):
    bb, _, tt, _ = q_ref.shape
    tab = jnp.broadcast_to(tab_ref[...], (bb, tt, 4 * LANE)).reshape(bb * tt, 4 * LANE)
    q, _, _ = _mla_common(zb_ref, tab, qn_ref, kvn_ref, wq1_ref, wq2_ref, rows_ref)
    lane = lax.broadcasted_iota(jnp.int32, (1, HP), 1)
    for h in range(B_HEADS):
        qh = q[:, h * HP:(h + 1) * HP]
        q_abs = _dot(qh.astype(BF16), wuk_ref[h])
        q_pe = jnp.where(lane < B_ROPE, pltpu.roll(qh, LANE // 2, 1), 0.0)
        q_ref[:, h, :, 0:B_KV_RANK] = q_abs.reshape(bb, tt, B_KV_RANK)
        q_ref[:, h, :, B_KV_RANK:] = q_pe[:, 0:B_ROPE].reshape(bb, tt, B_ROPE)


def _mla_prep_sample(z3, tab, qn, kvn, wq1, wq2, wuk):
    B, T, _ = z3.shape
    bb = _tile(B, max(1, 512 // T), 1)
    full = lambda a: pl.BlockSpec(a.shape, lambda i: (0,) * a.ndim)
    return pl.pallas_call(
        _mla_prep_sample_kernel,
        out_shape=(jax.ShapeDtypeStruct((B, B_HEADS, T, CACHE_W), F32),
                   jax.ShapeDtypeStruct((B, T, CACHE_W), F32)),
        grid=(B // bb,),
        in_specs=[pl.BlockSpec((bb, T, Z_BW), lambda i: (i, 0, Z_B // Z_BW)),
                  pl.BlockSpec((1, T, 4 * LANE), lambda i: (0, 0, 0)),
                  full(qn), full(kvn), full(wq1), full(wq2), full(wuk)],
        out_specs=(pl.BlockSpec((bb, B_HEADS, T, CACHE_W), lambda i: (i, 0, 0, 0)),
                   pl.BlockSpec((bb, T, CACHE_W), lambda i: (i, 0, 0))),
        compiler_params=_cp(("parallel",)),
        name="mla_prep_sample",
    )(z3, tab, qn, kvn, wq1, wq2, wuk)


def _sattn_kernel(pt_ref, q_ref, new_ref, wv_ref, *rest, G, T):
    pages = rest[:G]
    y_ref = rest[G]
    m_s, l_s, acc_s = rest[G + 1:]
    j = pl.program_id(1)
    R = B_HEADS * T

    @pl.when(j == 0)
    def _():
        m_s[...] = jnp.full(m_s.shape, -jnp.inf, F32)
        l_s[...] = jnp.zeros(l_s.shape, F32)
        acc_s[...] = jnp.zeros(acc_s.shape, F32)

    q = q_ref[0].reshape(R, CACHE_W).astype(BF16)

    def update(kv, mask):
        s = _dot_nt(q, kv)
        if mask is not None:
            s = jnp.where(mask, s, -jnp.inf)
        m_prev = m_s[...]
        m_new = jnp.maximum(m_prev, jnp.max(s, axis=1, keepdims=True))
        alpha = jnp.exp(m_prev - m_new)
        p = jnp.exp(s - m_new)
        l_s[...] = alpha * l_s[...] + jnp.sum(p, axis=1, keepdims=True)
        acc_s[...] = alpha * acc_s[...] + _dot(p.astype(BF16), kv[:, 0:B_KV_RANK])
        m_s[...] = m_new

    update(jnp.concatenate([p[...].astype(BF16) for p in pages], axis=0), None)

    @pl.when(j == pl.num_programs(1) - 1)
    def _():
        pad = jnp.zeros((PAGE_SIZE - T, CACHE_W), F32)
        new = jnp.concatenate([new_ref[0], pad], axis=0).astype(BF16)
        r = lax.broadcasted_iota(jnp.int32, (R, PAGE_SIZE), 0)
        c = lax.broadcasted_iota(jnp.int32, (R, PAGE_SIZE), 1)
        update(new, c <= r % T)
        o = acc_s[...] / l_s[...]
        for h in range(B_HEADS):
            oh = o[h * T:(h + 1) * T, :].astype(BF16)
            y_ref[0, :, h * HP:(h + 1) * HP] = _dot(oh, wv_ref[h]).astype(y_ref.dtype)


def _sample_attention(layer, q, new_rows, wv, cache, page_table, ydt):
    B, H, T, _ = q.shape
    n_pages = page_table.shape[1]
    G = _tile(n_pages, 8, 1)
    R = H * T

    def page_spec(g):
        return pl.BlockSpec((None, None, PAGE_SIZE, CACHE_W),
                            lambda b, j, pt: (layer, pt[b, j * G + g], 0, 0))

    grid_spec = pltpu.PrefetchScalarGridSpec(
        num_scalar_prefetch=1,
        grid=(B, n_pages // G),
        in_specs=[pl.BlockSpec((1, H, T, CACHE_W), lambda b, j, pt: (b, 0, 0, 0)),
                  pl.BlockSpec((1, T, CACHE_W), lambda b, j, pt: (b, 0, 0)),
                  pl.BlockSpec(wv.shape, lambda b, j, pt: (0, 0, 0))]
        + [page_spec(g) for g in range(G)],
        out_specs=pl.BlockSpec((1, T, H * HP), lambda b, j, pt: (b, 0, 0)),
        scratch_shapes=[pltpu.VMEM((R, 1), F32), pltpu.VMEM((R, 1), F32),
                        pltpu.VMEM((R, B_KV_RANK), F32)],
    )
    return pl.pallas_call(
        functools.partial(_sattn_kernel, G=G, T=T),
        out_shape=jax.ShapeDtypeStruct((B, T, H * HP), ydt),
        grid_spec=grid_spec,
        compiler_params=_cp(("parallel", "arbitrary")),
        name="mla_sample_attention",
    )(page_table, q, new_rows, wv, *([cache] * G))


def _merge_kernel(ya_ref, yb_ref, yc_ref, yd_ref, gt_ref, x_ref, g1_ref, wa_ref, wb_ref, wc_ref, wd_ref,
                  wo_ref, lg_ref, lb_ref, o_ref):
    rows = o_ref.shape[0] * o_ref.shape[1]

    def flat(ref):
        return ref[...].reshape(rows, ref.shape[-1])

    gates = flat(gt_ref)
    acc = None
    for n, (y_ref, w_ref) in enumerate(((ya_ref, wa_ref), (yb_ref, wb_ref), (yc_ref, wc_ref), (yd_ref, wd_ref))):
        term = _sigmoid(gates[:, n * D_MODEL:(n + 1) * D_MODEL]) * _dot(flat(y_ref).astype(BF16), w_ref[...])
        acc = term if acc is None else acc + term
    mix = _dot(acc.astype(BF16), wo_ref[...]).reshape(o_ref.shape)
    v = DEEPNORM_ALPHA * x_ref[...] + g1_ref[...] * mix
    o_ref[...] = _ln_rows(v) * lg_ref[...] + lb_ref[...]


def _merge(ya, yb, yc, yd, z3, x, g1, wa, wb, wc, wd, wo, lg, lb):
    B, T, D = x.shape
    bb, tt = _token_blocks(B, T, 256)
    nt = T // tt
    tok = lambda w: pl.BlockSpec((bb, tt, w), lambda i: (i // nt, i % nt, 0))
    full = lambda a: pl.BlockSpec(a.shape, lambda i: (0,) * a.ndim)
    vec = lambda a: a.reshape(1, 1, D)
    return pl.pallas_call(
        _merge_kernel,
        out_shape=jax.ShapeDtypeStruct((B, T, D), F32),
        grid=((B // bb) * nt,),
        in_specs=[tok(ya.shape[-1]), tok(yb.shape[-1]), tok(yc.shape[-1]), tok(yd.shape[-1]),
                  pl.BlockSpec((bb, tt, N_BRANCH * D), lambda i: (i // nt, i % nt, Z_G // (N_BRANCH * D_MODEL))),
                  tok(D),
                  pl.BlockSpec((bb, 1, D), lambda i: (i // nt, 0, 0)),
                  full(wa), full(wb), full(wc), full(wd), full(wo),
                  pl.BlockSpec((1, 1, D), lambda i: (0, 0, 0)), pl.BlockSpec((1, 1, D), lambda i: (0, 0, 0))],
        out_specs=tok(D),
        compiler_params=_cp(("parallel",)),
        name="merge_out_ln",
    )(ya, yb, yc, yd, z3, x, g1, wa, wb, wc, wd, wo, vec(lg), vec(lb))


def _ffn_kernel(x_ref, sc_ref, sh_ref, g2_ref, w1_ref, w3_ref, w2_ref, lg_ref, lb_ref, o_ref, u_s, acc_s):
    j = pl.program_id(1)

    @pl.when(j == 0)
    def _():
        u = x_ref[...] * (1.0 + sc_ref[...]) + sh_ref[...]
        u_s[...] = u.reshape(u_s.shape).astype(BF16)
        acc_s[...] = jnp.zeros(acc_s.shape, F32)

    u = u_s[...]
    hid = _silu(_dot(u, w1_ref[...])) * _dot(u, w3_ref[...])
    acc_s[...] += _dot(hid.astype(BF16), w2_ref[...])

    @pl.when(j == pl.num_programs(1) - 1)
    def _():
        v = DEEPNORM_ALPHA * x_ref[...] + g2_ref[...] * acc_s[...].reshape(o_ref.shape)
        o_ref[...] = _ln_rows(v) * lg_ref[...] + lb_ref[...]


def _ffn(x, sc, sh, g2, w1, w3, w2, lg, lb):
    B, T, D = x.shape
    Hd = w1.shape[1]
    bb, tt = _token_blocks(B, T, 512)
    nt = T // tt
    tm = bb * tt
    th = _tile(Hd, 1408, LANE)
    tok = pl.BlockSpec((bb, tt, D), lambda i, j: (i // nt, i % nt, 0))
    per_b = pl.BlockSpec((bb, 1, D), lambda i, j: (i // nt, 0, 0))
    vspec = pl.BlockSpec((1, 1, D), lambda i, j: (0, 0, 0))
    vec = lambda a: a.reshape(1, 1, D)
    return pl.pallas_call(
        _ffn_kernel,
        out_shape=jax.ShapeDtypeStruct((B, T, D), F32),
        grid=((B // bb) * nt, Hd // th),
        in_specs=[tok, per_b, per_b, per_b,
                  pl.BlockSpec((D, th), lambda i, j: (0, j)),
                  pl.BlockSpec((D, th), lambda i, j: (0, j)),
                  pl.BlockSpec((th, D), lambda i, j: (j, 0)),
                  vspec, vspec],
        out_specs=tok,
        scratch_shapes=[pltpu.VMEM((tm, D), BF16), pltpu.VMEM((tm, D), F32)],
        compiler_params=_cp(("parallel", "arbitrary")),
        name="ffn_ln",
    )(x, sc, sh, g2, w1, w3, w2, vec(lg), vec(lb))


def _rope_tables(pos):
    posf = pos.astype(F32)[:, None]
    T = pos.shape[0]

    def cs(d):
        inv = ROPE_BASE ** (-jnp.arange(0, d, 2, dtype=F32) / d)
        ang = posf * inv[None, :]
        c, s = jnp.cos(ang), jnp.sin(ang)
        return jnp.concatenate([c, c], axis=1), jnp.concatenate([-s, s], axis=1)

    c32, s32 = cs(B_ROPE)
    z = lambda w: jnp.zeros((T, w), F32)
    cos_q = jnp.concatenate([jnp.ones((T, B_NOPE), F32), c32, z(HP - B_NOPE - B_ROPE)], axis=1)
    sin_q = jnp.concatenate([z(B_NOPE), s32, z(HP - B_NOPE - B_ROPE)], axis=1)
    cos_k = jnp.concatenate([c32, z(LANE - B_ROPE)], axis=1)
    sin_k = jnp.concatenate([s32, z(LANE - B_ROPE)], axis=1)
    tab_b = jnp.concatenate([cos_q, sin_q, cos_k, sin_k], axis=1)
    c128, s128 = cs(D_DK)
    return tab_b, jnp.concatenate([c128, s128], axis=1)


def _swap_halves(a, lo, width):
    half = width // 2
    return jnp.concatenate([a[..., lo + half:lo + width], a[..., lo:lo + half]], axis=-1)


def _layer_weights(l, w_in, b_in, mla_w_uq, mla_w_uk, mla_w_uv, w_branch):
    def reorder(a):
        zeros = lambda w: jnp.zeros(a.shape[:-1] + (w,), a.dtype)
        return jnp.concatenate([
            a[..., O_G:N_IN], a[..., O_AQ:O_AI], a[..., O_D:O_G], a[..., O_C:O_D],
            a[..., O_BQ:O_BKR + B_ROPE], a[..., O_AI:O_BQ], zeros(MISC_KRSW - MISC_AF - A_HEADS),
            _swap_halves(a, O_BKR, B_ROPE), zeros(LANE - MISC_KRSW - B_ROPE)], axis=-1)

    w_in_r = reorder(w_in[l]).astype(BF16)
    b_in_r = reorder(b_in[l]).reshape(1, Z_W)

    hd = B_NOPE + B_ROPE
    wq = mla_w_uq[l].reshape(B_Q_RANK, B_HEADS, hd)
    zq = jnp.zeros((B_Q_RANK, B_HEADS, HP - hd), F32)
    wq1 = jnp.concatenate([wq, zq], axis=-1).reshape(B_Q_RANK, B_HEADS * HP).astype(BF16)
    wq2 = jnp.concatenate([jnp.zeros((B_Q_RANK, B_HEADS, B_NOPE), F32), _swap_halves(wq, B_NOPE, B_ROPE), zq],
                          axis=-1).reshape(B_Q_RANK, B_HEADS * HP).astype(BF16)
    wuk = mla_w_uk[l]
    wuv = mla_w_uv[l]
    wk = jnp.concatenate([wuk, jnp.zeros((B_KV_RANK, B_HEADS, HP - B_NOPE), F32)], axis=-1)
    wv = jnp.concatenate([wuv, jnp.zeros((B_KV_RANK, B_HEADS, HP - B_VDIM), F32)], axis=-1)
    wk_flat = wk.reshape(B_KV_RANK, B_HEADS * HP).astype(BF16)
    wv_flat = wv.reshape(B_KV_RANK, B_HEADS * HP).astype(BF16)
    wuk_t = jnp.transpose(wk, (1, 2, 0)).astype(BF16)
    wv_h = jnp.transpose(wv, (1, 0, 2)).astype(BF16)
    wb = w_branch[l]
    wb_b = jnp.concatenate([wb[1].reshape(B_HEADS, B_VDIM, D_MODEL),
                            jnp.zeros((B_HEADS, HP - B_VDIM, D_MODEL), F32)], axis=1)
    wb_b = wb_b.reshape(B_HEADS * HP, D_MODEL).astype(BF16)
    return dict(w_in=w_in_r, b_in=b_in_r, wq1=wq1, wq2=wq2, wk=wk_flat, wv=wv_flat, wuk_t=wuk_t, wv_h=wv_h,
                wb_a=wb[0].astype(BF16), wb_b=wb_b, wb_c=wb[2].astype(BF16), wb_d=wb[3].astype(BF16))


def kernel(x_prompt, x_sample, cache_mla, state_mlstm_C, state_mlstm_n, state_mlstm_m, state_conv, state_ret,
           page_table, c_prompt, c_sample, w_ada, b_ada, w_in, b_in, mla_q_norm, mla_w_uq, mla_kv_norm,
           mla_w_uk, mla_w_uv, conv_w, conv_b, conv_ln_g, conv_ln_b, w_branch, w_out, ln1_g, ln1_b,
           w_ffn1, w_ffn3, w_ffn2, ln2_g, ln2_b):
    dt = x_prompt.dtype
    Bp, Tp, D = x_prompt.shape
    Bd, Td, _ = x_sample.shape
    past_len = page_table.shape[1] * PAGE_SIZE
    tabs_p = _rope_tables(jnp.arange(Tp))
    tabs_d = _rope_tables(past_len + jnp.arange(Td))
    c_all = jnp.concatenate([c_prompt, c_sample], axis=0).astype(F32)

    xp, xd = x_prompt.astype(F32), x_sample.astype(F32)
    p_states = [[] for _ in range(6)]
    d_states = [[] for _ in range(6)]
    for l in range(DEPTH):
        lw = _layer_weights(l, w_in, b_in, mla_w_uq, mla_w_uk, mla_w_uv, w_branch)
        qn = mla_q_norm[l].reshape(1, B_Q_RANK)
        kvn = mla_kv_norm[l].reshape(1, B_KV_RANK)
        wo = w_out[l].astype(BF16)
        w1, w3, w2 = w_ffn1[l].astype(BF16), w_ffn3[l].astype(BF16), w_ffn2[l].astype(BF16)
        mod = _ada_mod(c_all, w_ada[l], b_ada[l])

        def group(x, mod_g, tabs, st, sample):
            B, T, _ = x.shape
            sh1, sc1, g1, sh2, sc2, g2 = [mod_g[:, None, i * D:(i + 1) * D] for i in range(6)]
            ydt = F32 if sample else BF16
            z3 = _in_proj(x, sc1, sh1, lw['w_in'], lw['b_in']).reshape(B, T, Z_W)
            c0, n0, m0, buf0, s0 = st
            ya, c1, n1, m1 = _mlstm(z3, c0, n0, m0, ydt)
            yd, s1 = _retention(z3, tabs[1], s0, ydt)
            yc, buf1 = _conv(z3, buf0, conv_w[l], conv_b[l], conv_ln_g[l], conv_ln_b[l], ydt)
            if sample:
                q, rows = _mla_prep_sample(z3, tabs[0].reshape(1, T, 4 * LANE), qn, kvn,
                                           lw['wq1'], lw['wq2'], lw['wuk_t'])
                yb = _sample_attention(l, q, rows, lw['wv_h'], cache_mla, page_table, ydt)
            else:
                q, k, v, rows = _mla_prep_prompt(z3, tabs[0], qn, kvn, lw['wq1'], lw['wq2'], lw['wk'], lw['wv'])
                yb = _flash(q, k, v)
            x1 = _merge(ya, yb, yc, yd, z3, x, g1, lw['wb_a'], lw['wb_b'], lw['wb_c'], lw['wb_d'], wo,
                        ln1_g[l], ln1_b[l])
            x2 = _ffn(x1, sc2, sh2, g2, w1, w3, w2, ln2_g[l], ln2_b[l])
            return x2, (rows, c1, n1, m1, buf1, s1)

        st_p = (jnp.zeros((Bp, A_HEADS, A_DK, A_DV), F32), jnp.zeros((Bp, A_HEADS, A_DK), F32),
                jnp.zeros((Bp, A_HEADS), F32), jnp.zeros((Bp, C_WIDTH - 1, C_CH), F32),
                jnp.zeros((Bp, D_HEADS, D_DK, D_DV), F32))
        xp, new_p = group(xp, mod[:Bp], tabs_p, st_p, False)
        st_d = (state_mlstm_C[l].astype(F32), state_mlstm_n[l].astype(F32), state_mlstm_m[l].astype(F32),
                state_conv[l].astype(F32), state_ret[l].astype(F32))
        xd, new_d = group(xd, mod[Bp:], tabs_d, st_d, True)
        for i in range(6):
            p_states[i].append(new_p[i])
            d_states[i].append(new_d[i])
    ps = [jnp.stack(s, axis=0).astype(dt) for s in p_states]
    ds = [jnp.stack(s, axis=0).astype(dt) for s in d_states]
    return (xp.astype(dt), xd.astype(dt), ps[0], ds[0], ps[1], ds[1], ps[2], ds[2], ps[3], ds[3],
            ps[4], ds[4], ps[5], ds[5])
```

```python
import functools

import numpy as np
import jax
import jax.numpy as jnp
from jax import lax
from jax.experimental import pallas as pl
from jax.experimental.pallas import tpu as pltpu

F32 = jnp.float32
BF16 = jnp.bfloat16

D_MODEL = 1024
DEPTH = 2
PAGE_SIZE = 128
N_BRANCH = 4
A_HEADS, A_DK, A_DV = 4, 128, 128
B_HEADS, B_Q_RANK, B_KV_RANK, B_NOPE, B_ROPE, B_VDIM = 8, 384, 256, 64, 32, 64
C_CH, C_WIDTH = 512, 31
D_HEADS, D_DK, D_DV = 4, 128, 128
FFN_HIDDEN = -(-8 * D_MODEL // (3 * 256)) * 256
CHUNK = 128
ROPE_BASE = 10000.0
LN_EPS = 1e-5
RMS_EPS = 1e-6
DEEPNORM_ALPHA = (2 * DEPTH) ** 0.25
CACHE_W = B_KV_RANK + B_ROPE
MLA_SCALE = (B_NOPE + B_ROPE) ** -0.5
LOG2E = float(np.log2(np.e))
PAGES_PER_STEP = 32

O_AQ, O_AI, O_AF = 0, 2048, 2052
O_BQ, O_BKV, O_BKR = 2056, 2440, 2696
O_C, O_D, O_G = 2728, 3752, 5800
N_IN = 9896

Z_G, Z_A, Z_D, Z_C, Z_B = 0, 4096, 6144, 8192, 9216
Z_W = 9984
Z_BW = 768
MISC_KR, MISC_AI, MISC_AF, MISC_KRSW = 0, 32, 36, 64
LANE = 128
HP = 128

VMEM_LIMIT = 56 * 1024 * 1024


def _cp(sem, vmem=VMEM_LIMIT):
    return pltpu.CompilerParams(dimension_semantics=sem, vmem_limit_bytes=vmem)


def _sigmoid(x):
    return 1.0 / (1.0 + jnp.exp(-x))


def _silu(x):
    return x * _sigmoid(x)


def _log_sigmoid(x):
    return jnp.minimum(x, 0.0) - jnp.log(1.0 + jnp.exp(-jnp.abs(x)))


def _ln_rows(x):
    mu = jnp.mean(x, axis=-1, keepdims=True)
    xc = x - mu
    var = jnp.mean(xc * xc, axis=-1, keepdims=True)
    return xc * lax.rsqrt(var + LN_EPS)


def _rms_rows(x):
    return x * lax.rsqrt(jnp.mean(x * x, axis=-1, keepdims=True) + RMS_EPS)


def _dot(a, b):
    return jnp.dot(a, b, preferred_element_type=F32)


def _dot_nt(a, b):
    return lax.dot_general(a, b, (((1,), (1,)), ((), ())), preferred_element_type=F32)


def _dot_tn(a, b):
    return lax.dot_general(a, b, (((0,), (0,)), ((), ())), preferred_element_type=F32)


def _tile(n, pref, align=8):
    if n <= pref:
        return n
    for t in range(pref, 0, -1):
        if n % t == 0 and t % align == 0:
            return t
    return n


def _mod_kernel(c_ref, w_ref, b_ref, o_ref):
    s = _silu(c_ref[...])
    o_ref[...] = _dot(s.astype(BF16), w_ref[...].astype(BF16)) + b_ref[...]


def _ada_mod(c, w, b):
    m, d = c.shape
    n = w.shape[1]
    tn = _tile(n, 768, LANE)
    return pl.pallas_call(
        _mod_kernel,
        out_shape=jax.ShapeDtypeStruct((m, n), F32),
        grid=(n // tn,),
        in_specs=[pl.BlockSpec((m, d), lambda j: (0, 0)),
                  pl.BlockSpec((d, tn), lambda j: (0, j)),
                  pl.BlockSpec((1, tn), lambda j: (0, j))],
        out_specs=pl.BlockSpec((m, tn), lambda j: (0, j)),
        compiler_params=_cp(("arbitrary",)),
        name="ada_mod",
    )(c, w, b.reshape(1, n))


def _inproj_kernel(x_ref, sc_ref, sh_ref, w_ref, b_ref, o_ref, u_ref):
    @pl.when(pl.program_id(1) == 0)
    def _():
        u = x_ref[...] * (1.0 + sc_ref[...]) + sh_ref[...]
        u_ref[...] = u.reshape(u_ref.shape).astype(BF16)

    o_ref[...] = _dot(u_ref[...], w_ref[...]) + b_ref[...]


def _token_blocks(B, T, rows):
    if T >= rows:
        return 1, _tile(T, rows)
    return _tile(B, max(1, rows // T), 1), T


def _in_proj(x, sc, sh, w, b):
    B, T, D = x.shape
    n = w.shape[1]
    bb, tt = _token_blocks(B, T, 1024)
    tm = bb * tt
    nt = T // tt
    tn = Z_BW
    grid = ((B // bb) * nt, n // tn)
    return pl.pallas_call(
        _inproj_kernel,
        out_shape=jax.ShapeDtypeStruct((B * T, n), F32),
        grid=grid,
        in_specs=[pl.BlockSpec((bb, tt, D), lambda i, j: (i // nt, i % nt, 0)),
                  pl.BlockSpec((bb, 1, D), lambda i, j: (i // nt, 0, 0)),
                  pl.BlockSpec((bb, 1, D), lambda i, j: (i // nt, 0, 0)),
                  pl.BlockSpec((D, tn), lambda i, j: (0, j)),
                  pl.BlockSpec((1, tn), lambda i, j: (0, j))],
        out_specs=pl.BlockSpec((tm, tn), lambda i, j: (i, j)),
        scratch_shapes=[pltpu.VMEM((tm, D), BF16)],
        compiler_params=_cp(("parallel", "arbitrary")),
        name="in_proj",
    )(x, sc, sh, w, b)


def _mlstm_kernel(q_ref, k_ref, v_ref, o_ref, g_ref, c0_ref, n0_ref, m0_ref,
                  y_ref, c_out, n_out, m_out, c_s, n_s, m_s, *, L, mxu):
    ci = pl.program_id(1)

    @pl.when(ci == 0)
    def _():
        c_s[...] = c0_ref[0]
        n_s[...] = n0_ref[0]
        m_s[...] = m0_ref[0]

    g = g_ref[0]
    gt = g.T
    row = lax.broadcasted_iota(jnp.int32, (L, L), 0)
    col = lax.broadcasted_iota(jnp.int32, (L, L), 1)
    tril = col <= row
    triu = row <= col
    for h in range(A_HEADS):
        hs = slice(h * A_DK, (h + 1) * A_DK)
        ig_row = gt[MISC_AI + h:MISC_AI + h + 1, :]
        lf_row = _log_sigmoid(gt[MISC_AF + h:MISC_AF + h + 1, :])
        ig_col = g[:, MISC_AI + h:MISC_AI + h + 1]
        lf_col = _log_sigmoid(g[:, MISC_AF + h:MISC_AF + h + 1])
        b_col = jnp.sum(jnp.where(tril, lf_row, 0.0), axis=1, keepdims=True)
        b_row = jnp.sum(jnp.where(triu, lf_col, 0.0), axis=0, keepdims=True)
        m_prev = m_s[h:h + 1, 0:1]
        inter = b_col + m_prev
        intra = jnp.where(tril, b_col - b_row + ig_row, -jnp.inf)
        m_t = jnp.maximum(inter, jnp.max(intra, axis=1, keepdims=True))
        w_inter = jnp.exp(inter - m_t)
        dmat = jnp.exp(intra - m_t)
        qh = q_ref[0, :, hs] * (A_DK ** -0.5)
        kh = k_ref[0, :, hs]
        vh = v_ref[0, :, hs]
        qb, kb, vb = qh.astype(mxu), kh.astype(mxu), vh.astype(mxu)
        s = _dot_nt(qb, kb) * dmat
        c_prev = c_s[h]
        n_prev = n_s[h:h + 1, :]
        num = _dot(s.astype(mxu), vb) + w_inter * _dot(qb, c_prev.astype(mxu))
        den = jnp.sum(s, axis=1, keepdims=True) + w_inter * jnp.sum(qh * n_prev, axis=1, keepdims=True)
        hv = num / jnp.maximum(jnp.abs(den), jnp.exp(-m_t))
        y_ref[0, :, hs] = (_sigmoid(o_ref[0, :, hs]) * _ln_rows(hv)).astype(y_ref.dtype)
        b_last = b_col[L - 1:L, :]
        m_new = m_t[L - 1:L, :]
        w_c = jnp.exp(b_last + m_prev - m_new)
        w_s = jnp.exp(b_last - b_col + ig_col - m_new)
        kw = kh * w_s
        c_s[h] = w_c * c_prev + _dot_tn(kw.astype(mxu), vb)
        n_s[h:h + 1, :] = w_c * n_prev + jnp.sum(kw, axis=0, keepdims=True)
        m_s[h:h + 1, :] = jnp.broadcast_to(m_new, (1, LANE))

    @pl.when(ci == pl.num_programs(1) - 1)
    def _():
        c_out[0] = c_s[...]
        n_out[0] = n_s[...]
        m_out[0] = m_s[...]


def _mlstm(z3, c0, n0, m0, ydt):
    B, T, _ = z3.shape
    L = min(CHUNK, T)
    nc = T // L
    H = A_HEADS
    W = H * A_DK
    ab = Z_A // W
    m0b = jnp.broadcast_to(m0[..., None], (B, H, LANE))
    mxu = BF16 if L >= 16 else F32

    def zspec(k):
        return pl.BlockSpec((1, L, W), lambda b, c: (b, c, ab + k))

    y, c1, n1, m1 = pl.pallas_call(
        functools.partial(_mlstm_kernel, L=L, mxu=mxu),
        out_shape=(jax.ShapeDtypeStruct((B, T, W), ydt),
                   jax.ShapeDtypeStruct((B, H, A_DK, A_DV), F32),
                   jax.ShapeDtypeStruct((B, H, A_DK), F32),
                   jax.ShapeDtypeStruct((B, H, LANE), F32)),
        grid=(B, nc),
        in_specs=[zspec(0), zspec(1), zspec(2), zspec(3),
                  pl.BlockSpec((1, L, LANE), lambda b, c: (b, c, (Z_B + Z_BW - LANE) // LANE)),
                  pl.BlockSpec((1, H, A_DK, A_DV), lambda b, c: (b, 0, 0, 0)),
                  pl.BlockSpec((1, H, A_DK), lambda b, c: (b, 0, 0)),
                  pl.BlockSpec((1, H, LANE), lambda b, c: (b, 0, 0))],
        out_specs=(pl.BlockSpec((1, L, W), lambda b, c: (b, c, 0)),
                   pl.BlockSpec((1, H, A_DK, A_DV), lambda b, c: (b, 0, 0, 0)),
                   pl.BlockSpec((1, H, A_DK), lambda b, c: (b, 0, 0)),
                   pl.BlockSpec((1, H, LANE), lambda b, c: (b, 0, 0))),
        scratch_shapes=[pltpu.VMEM((H, A_DK, A_DV), F32), pltpu.VMEM((H, A_DK), F32),
                        pltpu.VMEM((H, LANE), F32)],
        compiler_params=_cp(("parallel", "arbitrary")),
        name="mlstm",
    )(z3, z3, z3, z3, z3, c0, n0, m0b)
    return y, c1, n1, m1[..., 0]


_LOG_GAMMA = tuple(float(np.log(1.0 - 2.0 ** (-5.0 - h))) for h in range(D_HEADS))


def _ret_kernel(q_ref, k_ref, v_ref, g_ref, tab_ref, s0_ref, y_ref, s_out, s_s, *, L, mxu):
    ci = pl.program_id(1)

    @pl.when(ci == 0)
    def _():
        s_s[...] = s0_ref[0]

    cos = tab_ref[:, 0:D_DK]
    sin = tab_ref[:, D_DK:2 * D_DK]
    row = lax.broadcasted_iota(jnp.int32, (L, L), 0)
    col = lax.broadcasted_iota(jnp.int32, (L, L), 1)
    diff = (row - col).astype(F32)
    jcol = lax.broadcasted_iota(jnp.int32, (L, 1), 0).astype(F32)
    for h in range(D_HEADS):
        hs = slice(h * D_DK, (h + 1) * D_DK)
        lg = _LOG_GAMMA[h]
        decay = jnp.where(diff >= 0.0, jnp.exp(jnp.maximum(diff, 0.0) * lg), 0.0)
        w_in = jnp.exp((jcol + 1.0) * lg)
        w_st = jnp.exp((L - 1.0 - jcol) * lg)
        w_S = float(np.exp(L * lg))
        q = q_ref[0, :, hs]
        k = k_ref[0, :, hs]
        qh = q * cos + pltpu.roll(q, D_DK // 2, 1) * sin
        kh = (k * cos + pltpu.roll(k, D_DK // 2, 1) * sin) * (D_DK ** -0.5)
        vb = v_ref[0, :, hs].astype(mxu)
        qb = qh.astype(mxu)
        s = _dot_nt(qb, kh.astype(mxu)) * decay
        s_prev = s_s[h]
        o = _dot(s.astype(mxu), vb) + w_in * _dot(qb, s_prev.astype(mxu))
        y_ref[0, :, hs] = (_silu(g_ref[0, :, hs]) * _ln_rows(o)).astype(y_ref.dtype)
        s_s[h] = w_S * s_prev + _dot_tn((kh * w_st).astype(mxu), vb)

    @pl.when(ci == pl.num_programs(1) - 1)
    def _():
        s_out[0] = s_s[...]


def _retention(z3, tab, s0, ydt):
    B, T, _ = z3.shape
    L = min(CHUNK, T)
    nc = T // L
    H = D_HEADS
    W = H * D_DK
    db = Z_D // W
    mxu = BF16 if L >= 16 else F32

    def zspec(k):
        return pl.BlockSpec((1, L, W), lambda b, c: (b, c, db + k))

    return pl.pallas_call(
        functools.partial(_ret_kernel, L=L, mxu=mxu),
        out_shape=(jax.ShapeDtypeStruct((B, T, W), ydt),
                   jax.ShapeDtypeStruct((B, H, D_DK, D_DV), F32)),
        grid=(B, nc),
        in_specs=[zspec(0), zspec(1), zspec(2), zspec(3),
                  pl.BlockSpec((L, 2 * D_DK), lambda b, c: (c, 0)),
                  pl.BlockSpec((1, H, D_DK, D_DV), lambda b, c: (b, 0, 0, 0))],
        out_specs=(pl.BlockSpec((1, L, W), lambda b, c: (b, c, 0)),
                   pl.BlockSpec((1, H, D_DK, D_DV), lambda b, c: (b, 0, 0, 0))),
        scratch_shapes=[pltpu.VMEM((H, D_DK, D_DV), F32)],
        compiler_params=_cp(("parallel", "arbitrary")),
        name="retention",
    )(z3, z3, z3, z3, tab, s0)


_CPAD = 32


def _conv_kernel(glu_ref, buf_ref, w_ref, b_ref, g_ref, be_ref, y_ref, buf_out, ext, *, tt):
    ti = pl.program_id(1)
    lead = _CPAD - (C_WIDTH - 1)

    @pl.when(ti == 0)
    def _():
        ext[0:lead, :] = jnp.zeros((lead, C_CH), F32)
        ext[lead:_CPAD, :] = buf_ref[0]

    glu = glu_ref[0]
    ext[_CPAD:_CPAD + tt, :] = glu[:, :C_CH] * _sigmoid(glu[:, C_CH:])
    acc = jnp.zeros((tt, C_CH), F32)
    for j in range(C_WIDTH):
        acc = acc + ext[lead + j:lead + j + tt, :] * w_ref[j:j + 1, :]
    y = _ln_rows(acc + b_ref[...]) * g_ref[...] + be_ref[...]
    y_ref[0] = _silu(y).astype(y_ref.dtype)

    @pl.when(ti == pl.num_programs(1) - 1)
    def _():
        buf_out[0] = ext[tt + lead:tt + _CPAD, :]

    ext[0:_CPAD, :] = ext[tt:tt + _CPAD, :]


def _conv(z3, buf0, w, b, g, be, ydt):
    B, T, _ = z3.shape
    tt = _tile(T, 256)
    nt = T // tt
    vec = lambda a: a.reshape(1, C_CH)
    return pl.pallas_call(
        functools.partial(_conv_kernel, tt=tt),
        out_shape=(jax.ShapeDtypeStruct((B, T, C_CH), ydt),
                   jax.ShapeDtypeStruct((B, C_WIDTH - 1, C_CH), F32)),
        grid=(B, nt),
        in_specs=[pl.BlockSpec((1, tt, 2 * C_CH), lambda b, t: (b, t, Z_C // (2 * C_CH))),
                  pl.BlockSpec((1, C_WIDTH - 1, C_CH), lambda b, t: (b, 0, 0)),
                  pl.BlockSpec((C_WIDTH, C_CH), lambda b, t: (0, 0)),
                  pl.BlockSpec((1, C_CH), lambda b, t: (0, 0)),
                  pl.BlockSpec((1, C_CH), lambda b, t: (0, 0)),
                  pl.BlockSpec((1, C_CH), lambda b, t: (0, 0))],
        out_specs=(pl.BlockSpec((1, tt, C_CH), lambda b, t: (b, t, 0)),
                   pl.BlockSpec((1, C_WIDTH - 1, C_CH), lambda b, t: (b, 0, 0))),
        scratch_shapes=[pltpu.VMEM((tt + _CPAD, C_CH), F32)],
        compiler_params=_cp(("parallel", "arbitrary")),
        name="conv_module",
    )(z3, buf0, w, vec(b), vec(g), vec(be))


def _mla_common(zb_ref, tab, qn_ref, kvn_ref, wq1_ref, wq2_ref, rows_ref, q_scale):
    zb = zb_ref[...].reshape(-1, Z_BW)
    bq = zb[:, 0:B_Q_RANK]
    bkv = zb[:, B_Q_RANK:B_Q_RANK + B_KV_RANK]
    misc = zb[:, B_Q_RANK + B_KV_RANK:]
    qn = (_rms_rows(bq) * qn_ref[...]).astype(BF16)
    ckv = _rms_rows(bkv) * kvn_ref[...]
    cos_q = jnp.concatenate([tab[:, 0:LANE]] * B_HEADS, axis=1)
    sin_q = jnp.concatenate([tab[:, LANE:2 * LANE]] * B_HEADS, axis=1)
    q = (_dot(qn, wq1_ref[...]) * cos_q + _dot(qn, wq2_ref[...]) * sin_q) * q_scale
    kpe = misc * tab[:, 2 * LANE:3 * LANE] + pltpu.roll(misc, LANE // 2, 1) * tab[:, 3 * LANE:]
    lead = rows_ref.shape[:-1]
    rows_ref[:, :, 0:B_KV_RANK] = ckv.reshape(lead + (B_KV_RANK,))
    rows_ref[:, :, B_KV_RANK:CACHE_W] = kpe[:, 0:B_ROPE].reshape(lead + (B_ROPE,))
    return q, ckv, kpe


def _mla_prep_prompt_kernel(zb_ref, tab_ref, qn_ref, kvn_ref, wq1_ref, wq2_ref, wk_ref, wv_ref,
                            q_ref, k_ref, v_ref, rows_ref):
    q, ckv, kpe = _mla_common(zb_ref, tab_ref[...], qn_ref, kvn_ref, wq1_ref, wq2_ref, rows_ref,
                              MLA_SCALE * LOG2E)
    q_ref[0] = q.astype(BF16)
    cb = ckv.astype(BF16)
    kpe_hi = pltpu.roll(kpe, LANE // 2, 1)
    k = _dot(cb, wk_ref[...]) + jnp.concatenate([kpe_hi] * B_HEADS, axis=1)
    k_ref[0] = k.astype(BF16)
    lane = lax.broadcasted_iota(jnp.int32, (1, B_HEADS * HP), 1)
    ones_col = jnp.where(lane % HP == B_VDIM, 1.0, 0.0)
    v_ref[0] = (_dot(cb, wv_ref[...]) + ones_col).astype(BF16)


def _mla_prep_prompt(z3, tab, qn, kvn, wq1, wq2, wk, wv):
    B, T, _ = z3.shape
    tt = _tile(T, 512)
    nt = T // tt
    HW = B_HEADS * HP
    full = lambda a: pl.BlockSpec(a.shape, lambda b, t: (0,) * a.ndim)
    big = pl.BlockSpec((1, tt, HW), lambda b, t: (b, t, 0))
    return pl.pallas_call(
        _mla_prep_prompt_kernel,
        out_shape=(jax.ShapeDtypeStruct((B, T, HW), BF16),) * 3
        + (jax.ShapeDtypeStruct((B, T, CACHE_W), F32),),
        grid=(B, nt),
        in_specs=[pl.BlockSpec((1, tt, Z_BW), lambda b, t: (b, t, Z_B // Z_BW)),
                  pl.BlockSpec((tt, 4 * LANE), lambda b, t: (t, 0)),
                  full(qn), full(kvn), full(wq1), full(wq2), full(wk), full(wv)],
        out_specs=(big, big, big, pl.BlockSpec((1, tt, CACHE_W), lambda b, t: (b, t, 0))),
        compiler_params=_cp(("parallel", "parallel")),
        name="mla_prep_prompt",
    )(z3, tab, qn, kvn, wq1, wq2, wk, wv)


def _flash_kernel(q_ref, k_ref, v_ref, o_ref, *, tq):
    qi = pl.program_id(2)
    q = q_ref[0]

    def step(kj, carry, causal):
        m, acc = carry
        start = pl.multiple_of(kj * tq, tq)
        k = k_ref[0, pl.ds(start, tq), :]
        v = v_ref[0, pl.ds(start, tq), :]
        s = _dot_nt(q, k)
        if causal:
            row = lax.broadcasted_iota(jnp.int32, (tq, tq), 0)
            col = lax.broadcasted_iota(jnp.int32, (tq, tq), 1)
            s = jnp.where(col <= row, s, -jnp.inf)
        m_new = jnp.maximum(m, jnp.max(s, axis=1, keepdims=True))
        acc = jnp.exp2(m - m_new) * acc + _dot(jnp.exp2(s - m_new).astype(BF16), v)
        return m_new, acc

    init = (jnp.full((tq, 1), -jnp.inf, F32), jnp.zeros((tq, HP), F32))
    carry = lax.fori_loop(0, qi, lambda kj, c: step(kj, c, False), init)
    _, acc = step(qi, carry, True)
    o_ref[0] = (acc / acc[:, B_VDIM:B_VDIM + 1]).astype(o_ref.dtype)


def _flash(q, k, v):
    B, T, HW = q.shape
    tq = _tile(T, 512)
    return pl.pallas_call(
        functools.partial(_flash_kernel, tq=tq),
        out_shape=jax.ShapeDtypeStruct((B, T, HW), BF16),
        grid=(B, B_HEADS, T // tq),
        in_specs=[pl.BlockSpec((1, tq, HP), lambda b, h, i: (b, i, h)),
                  pl.BlockSpec((1, T, HP), lambda b, h, i: (b, 0, h)),
                  pl.BlockSpec((1, T, HP), lambda b, h, i: (b, 0, h))],
        out_specs=pl.BlockSpec((1, tq, HP), lambda b, h, i: (b, i, h)),
        compiler_params=_cp(("parallel", "parallel", "arbitrary")),
        name="mla_prompt_attention",
    )(q, k, v)


def _mla_prep_sample_kernel(zb_ref, tab_ref, qn_ref, kvn_ref, wq1_ref, wq2_ref, wuk_ref,
                            q_ref, rows_ref):
    bb, _, tt, _ = q_ref.shape
    tab = jnp.broadcast_to(tab_ref[...], (bb, tt, 4 * LANE)).reshape(bb * tt, 4 * LANE)
    q, _, _ = _mla_common(zb_ref, tab, qn_ref, kvn_ref, wq1_ref, wq2_ref, rows_ref, MLA_SCALE)
    lane = lax.broadcasted_iota(jnp.int32, (1, HP), 1)
    for h in range(B_HEADS):
        qh = q[:, h * HP:(h + 1) * HP]
        q_abs = _dot(qh.astype(BF16), wuk_ref[h])
        q_pe = jnp.where(lane < B_ROPE, pltpu.roll(qh, LANE // 2, 1), 0.0)
        q_ref[:, h, :, 0:B_KV_RANK] = q_abs.reshape(bb, tt, B_KV_RANK)
        q_ref[:, h, :, B_KV_RANK:] = q_pe[:, 0:B_ROPE].reshape(bb, tt, B_ROPE)


def _mla_prep_sample(z3, tab, qn, kvn, wq1, wq2, wuk):
    B, T, _ = z3.shape
    bb = _tile(B, max(1, 512 // T), 1)
    full = lambda a: pl.BlockSpec(a.shape, lambda i: (0,) * a.ndim)
    return pl.pallas_call(
        _mla_prep_sample_kernel,
        out_shape=(jax.ShapeDtypeStruct((B, B_HEADS, T, CACHE_W), F32),
                   jax.ShapeDtypeStruct((B, T, CACHE_W), F32)),
        grid=(B // bb,),
        in_specs=[pl.BlockSpec((bb, T, Z_BW), lambda i: (i, 0, Z_B // Z_BW)),
                  pl.BlockSpec((1, T, 4 * LANE), lambda i: (0, 0, 0)),
                  full(qn), full(kvn), full(wq1), full(wq2), full(wuk)],
        out_specs=(pl.BlockSpec((bb, B_HEADS, T, CACHE_W), lambda i: (i, 0, 0, 0)),
                   pl.BlockSpec((bb, T, CACHE_W), lambda i: (i, 0, 0))),
        compiler_params=_cp(("parallel",)),
        name="mla_prep_sample",
    )(z3, tab, qn, kvn, wq1, wq2, wuk)


def _sattn_kernel(pt_ref, q_ref, new_ref, wv_ref, *rest, G, T):
    pages = rest[:G]
    y_ref = rest[G]
    m_s, l_s, acc_s = rest[G + 1:]
    j = pl.program_id(1)
    R = B_HEADS * T

    @pl.when(j == 0)
    def _():
        m_s[...] = jnp.full(m_s.shape, -jnp.inf, F32)
        l_s[...] = jnp.zeros(l_s.shape, F32)
        acc_s[...] = jnp.zeros(acc_s.shape, F32)

    q = q_ref[0].reshape(R, CACHE_W).astype(BF16)

    def update(s, pv):
        m_prev = m_s[...]
        m_new = jnp.maximum(m_prev, jnp.max(s, axis=1, keepdims=True))
        alpha = jnp.exp(m_prev - m_new)
        p = jnp.exp(s - m_new)
        l_s[...] = alpha * l_s[...] + jnp.sum(p, axis=1, keepdims=True)
        acc_s[...] = alpha * acc_s[...] + pv(p.astype(BF16))
        m_s[...] = m_new

    kt = jnp.concatenate([p[...].astype(BF16) for p in pages], axis=1)
    update(_dot(q, kt), lambda p: _dot_nt(p, kt[0:B_KV_RANK, :]))

    @pl.when(j == pl.num_programs(1) - 1)
    def _():
        pad = jnp.zeros((PAGE_SIZE - T, CACHE_W), F32)
        new = jnp.concatenate([new_ref[0], pad], axis=0).astype(BF16)
        r = lax.broadcasted_iota(jnp.int32, (R, PAGE_SIZE), 0)
        c = lax.broadcasted_iota(jnp.int32, (R, PAGE_SIZE), 1)
        update(jnp.where(c <= r % T, _dot_nt(q, new), -jnp.inf), lambda p: _dot(p, new[:, 0:B_KV_RANK]))
        o = acc_s[...] / l_s[...]
        for h in range(B_HEADS):
            oh = o[h * T:(h + 1) * T, :].astype(BF16)
            y_ref[0, :, h * HP:(h + 1) * HP] = _dot(oh, wv_ref[h]).astype(y_ref.dtype)


def _sample_attention(layer, q, new_rows, wv, cache_t, page_table, ydt):
    B, H, T, _ = q.shape
    n_pages = page_table.shape[1]
    G = _tile(n_pages, PAGES_PER_STEP, 1)
    R = H * T

    def page_spec(g):
        return pl.BlockSpec((None, None, CACHE_W, PAGE_SIZE),
                            lambda b, j, pt: (layer, pt[b, j * G + g], 0, 0))

    grid_spec = pltpu.PrefetchScalarGridSpec(
        num_scalar_prefetch=1,
        grid=(B, n_pages // G),
        in_specs=[pl.BlockSpec((1, H, T, CACHE_W), lambda b, j, pt: (b, 0, 0, 0)),
                  pl.BlockSpec((1, T, CACHE_W), lambda b, j, pt: (b, 0, 0)),
                  pl.BlockSpec(wv.shape, lambda b, j, pt: (0, 0, 0))]
        + [page_spec(g) for g in range(G)],
        out_specs=pl.BlockSpec((1, T, H * HP), lambda b, j, pt: (b, 0, 0)),
        scratch_shapes=[pltpu.VMEM((R, 1), F32), pltpu.VMEM((R, 1), F32),
                        pltpu.VMEM((R, B_KV_RANK), F32)],
    )
    return pl.pallas_call(
        functools.partial(_sattn_kernel, G=G, T=T),
        out_shape=jax.ShapeDtypeStruct((B, T, H * HP), ydt),
        grid_spec=grid_spec,
        compiler_params=_cp(("parallel", "arbitrary")),
        name="mla_sample_attention",
    )(page_table, q, new_rows, wv, *([cache_t] * G))


def _merge_kernel(ya_ref, yb_ref, yc_ref, yd_ref, gt_ref, x_ref, g1_ref, wa_ref, wb_ref, wc_ref, wd_ref,
                  wo_ref, lg_ref, lb_ref, o_ref):
    rows = o_ref.shape[0] * o_ref.shape[1]

    def flat(ref):
        return ref[...].reshape(rows, ref.shape[-1])

    gates = flat(gt_ref)
    acc = None
    for n, (y_ref, w_ref) in enumerate(((ya_ref, wa_ref), (yb_ref, wb_ref), (yc_ref, wc_ref), (yd_ref, wd_ref))):
        term = _sigmoid(gates[:, n * D_MODEL:(n + 1) * D_MODEL]) * _dot(flat(y_ref).astype(BF16), w_ref[...])
        acc = term if acc is None else acc + term
    mix = _dot(acc.astype(BF16), wo_ref[...]).reshape(o_ref.shape)
    v = DEEPNORM_ALPHA * x_ref[...] + g1_ref[...] * mix
    o_ref[...] = _ln_rows(v) * lg_ref[...] + lb_ref[...]


def _merge(ya, yb, yc, yd, z3, x, g1, wa, wb, wc, wd, wo, lg, lb):
    B, T, D = x.shape
    bb, tt = _token_blocks(B, T, 256)
    nt = T // tt
    tok = lambda w: pl.BlockSpec((bb, tt, w), lambda i: (i // nt, i % nt, 0))
    full = lambda a: pl.BlockSpec(a.shape, lambda i: (0,) * a.ndim)
    vec = lambda a: a.reshape(1, 1, D)
    return pl.pallas_call(
        _merge_kernel,
        out_shape=jax.ShapeDtypeStruct((B, T, D), F32),
        grid=((B // bb) * nt,),
        in_specs=[tok(ya.shape[-1]), tok(yb.shape[-1]), tok(yc.shape[-1]), tok(yd.shape[-1]),
                  pl.BlockSpec((bb, tt, N_BRANCH * D), lambda i: (i // nt, i % nt, Z_G // (N_BRANCH * D_MODEL))),
                  tok(D),
                  pl.BlockSpec((bb, 1, D), lambda i: (i // nt, 0, 0)),
                  full(wa), full(wb), full(wc), full(wd), full(wo),
                  pl.BlockSpec((1, 1, D), lambda i: (0, 0, 0)), pl.BlockSpec((1, 1, D), lambda i: (0, 0, 0))],
        out_specs=tok(D),
        compiler_params=_cp(("parallel",)),
        name="merge_out_ln",
    )(ya, yb, yc, yd, z3, x, g1, wa, wb, wc, wd, wo, vec(lg), vec(lb))


def _ffn_kernel(x_ref, sc_ref, sh_ref, g2_ref, w1_ref, w3_ref, w2_ref, lg_ref, lb_ref, o_ref, u_s, acc_s):
    j = pl.program_id(1)

    @pl.when(j == 0)
    def _():
        u = x_ref[...] * (1.0 + sc_ref[...]) + sh_ref[...]
        u_s[...] = u.reshape(u_s.shape).astype(BF16)
        acc_s[...] = jnp.zeros(acc_s.shape, F32)

    u = u_s[...]
    hid = _silu(_dot(u, w1_ref[...])) * _dot(u, w3_ref[...])
    acc_s[...] += _dot(hid.astype(BF16), w2_ref[...])

    @pl.when(j == pl.num_programs(1) - 1)
    def _():
        v = DEEPNORM_ALPHA * x_ref[...] + g2_ref[...] * acc_s[...].reshape(o_ref.shape)
        o_ref[...] = _ln_rows(v) * lg_ref[...] + lb_ref[...]


def _ffn(x, sc, sh, g2, w1, w3, w2, lg, lb):
    B, T, D = x.shape
    Hd = w1.shape[1]
    bb, tt = _token_blocks(B, T, 512)
    nt = T // tt
    tm = bb * tt
    th = _tile(Hd, 1408, LANE)
    tok = pl.BlockSpec((bb, tt, D), lambda i, j: (i // nt, i % nt, 0))
    per_b = pl.BlockSpec((bb, 1, D), lambda i, j: (i // nt, 0, 0))
    vspec = pl.BlockSpec((1, 1, D), lambda i, j: (0, 0, 0))
    vec = lambda a: a.reshape(1, 1, D)
    return pl.pallas_call(
        _ffn_kernel,
        out_shape=jax.ShapeDtypeStruct((B, T, D), F32),
        grid=((B // bb) * nt, Hd // th),
        in_specs=[tok, per_b, per_b, per_b,
                  pl.BlockSpec((D, th), lambda i, j: (0, j)),
                  pl.BlockSpec((D, th), lambda i, j: (0, j)),
                  pl.BlockSpec((th, D), lambda i, j: (j, 0)),
                  vspec, vspec],
        out_specs=tok,
        scratch_shapes=[pltpu.VMEM((tm, D), BF16), pltpu.VMEM((tm, D), F32)],
        compiler_params=_cp(("parallel", "arbitrary")),
        name="ffn_ln",
    )(x, sc, sh, g2, w1, w3, w2, vec(lg), vec(lb))


def _rope_tables(pos):
    posf = pos.astype(F32)[:, None]
    T = pos.shape[0]

    def cs(d):
        inv = ROPE_BASE ** (-jnp.arange(0, d, 2, dtype=F32) / d)
        ang = posf * inv[None, :]
        c, s = jnp.cos(ang), jnp.sin(ang)
        return jnp.concatenate([c, c], axis=1), jnp.concatenate([-s, s], axis=1)

    c32, s32 = cs(B_ROPE)
    z = lambda w: jnp.zeros((T, w), F32)
    cos_q = jnp.concatenate([jnp.ones((T, B_NOPE), F32), c32, z(HP - B_NOPE - B_ROPE)], axis=1)
    sin_q = jnp.concatenate([z(B_NOPE), s32, z(HP - B_NOPE - B_ROPE)], axis=1)
    cos_k = jnp.concatenate([c32, z(LANE - B_ROPE)], axis=1)
    sin_k = jnp.concatenate([s32, z(LANE - B_ROPE)], axis=1)
    tab_b = jnp.concatenate([cos_q, sin_q, cos_k, sin_k], axis=1)
    c128, s128 = cs(D_DK)
    return tab_b, jnp.concatenate([c128, s128], axis=1)


def _swap_halves(a, lo, width):
    half = width // 2
    return jnp.concatenate([a[..., lo + half:lo + width], a[..., lo:lo + half]], axis=-1)


def _layer_weights(l, w_in, b_in, mla_w_uq, mla_w_uk, mla_w_uv, w_branch):
    def reorder(a):
        zeros = lambda w: jnp.zeros(a.shape[:-1] + (w,), a.dtype)
        return jnp.concatenate([
            a[..., O_G:N_IN], a[..., O_AQ:O_AI], a[..., O_D:O_G], a[..., O_C:O_D],
            a[..., O_BQ:O_BKR + B_ROPE], a[..., O_AI:O_BQ], zeros(MISC_KRSW - MISC_AF - A_HEADS),
            _swap_halves(a, O_BKR, B_ROPE), zeros(LANE - MISC_KRSW - B_ROPE)], axis=-1)

    w_in_r = reorder(w_in[l]).astype(BF16)
    b_in_r = reorder(b_in[l]).reshape(1, Z_W)

    hd = B_NOPE + B_ROPE
    wq = mla_w_uq[l].reshape(B_Q_RANK, B_HEADS, hd)
    zq = jnp.zeros((B_Q_RANK, B_HEADS, HP - hd), F32)
    wq1 = jnp.concatenate([wq, zq], axis=-1).reshape(B_Q_RANK, B_HEADS * HP).astype(BF16)
    wq2 = jnp.concatenate([jnp.zeros((B_Q_RANK, B_HEADS, B_NOPE), F32), _swap_halves(wq, B_NOPE, B_ROPE), zq],
                          axis=-1).reshape(B_Q_RANK, B_HEADS * HP).astype(BF16)
    wuk = mla_w_uk[l]
    wuv = mla_w_uv[l]
    wk = jnp.concatenate([wuk, jnp.zeros((B_KV_RANK, B_HEADS, HP - B_NOPE), F32)], axis=-1)
    wv = jnp.concatenate([wuv, jnp.zeros((B_KV_RANK, B_HEADS, HP - B_VDIM), F32)], axis=-1)
    wk_flat = wk.reshape(B_KV_RANK, B_HEADS * HP).astype(BF16)
    wv_flat = wv.reshape(B_KV_RANK, B_HEADS * HP).astype(BF16)
    wuk_t = jnp.transpose(wk, (1, 2, 0)).astype(BF16)
    wv_h = jnp.transpose(wv, (1, 0, 2)).astype(BF16)
    wb = w_branch[l]
    wb_b = jnp.concatenate([wb[1].reshape(B_HEADS, B_VDIM, D_MODEL),
                            jnp.zeros((B_HEADS, HP - B_VDIM, D_MODEL), F32)], axis=1)
    wb_b = wb_b.reshape(B_HEADS * HP, D_MODEL).astype(BF16)
    return dict(w_in=w_in_r, b_in=b_in_r, wq1=wq1, wq2=wq2, wk=wk_flat, wv=wv_flat, wuk_t=wuk_t, wv_h=wv_h,
                wb_a=wb[0].astype(BF16), wb_b=wb_b, wb_c=wb[2].astype(BF16), wb_d=wb[3].astype(BF16))


def kernel(x_prompt, x_sample, cache_mla, state_mlstm_C, state_mlstm_n, state_mlstm_m, state_conv, state_ret,
           page_table, c_prompt, c_sample, w_ada, b_ada, w_in, b_in, mla_q_norm, mla_w_uq, mla_kv_norm,
           mla_w_uk, mla_w_uv, conv_w, conv_b, conv_ln_g, conv_ln_b, w_branch, w_out, ln1_g, ln1_b,
           w_ffn1, w_ffn3, w_ffn2, ln2_g, ln2_b):
    dt = x_prompt.dtype
    Bp, Tp, D = x_prompt.shape
    Bd, Td, _ = x_sample.shape
    past_len = page_table.shape[1] * PAGE_SIZE
    cache_t = jnp.swapaxes(cache_mla, 2, 3)
    tabs_p = _rope_tables(jnp.arange(Tp))
    tabs_d = _rope_tables(past_len + jnp.arange(Td))
    c_all = jnp.concatenate([c_prompt, c_sample], axis=0).astype(F32)

    xp, xd = x_prompt.astype(F32), x_sample.astype(F32)
    p_states = [[] for _ in range(6)]
    d_states = [[] for _ in range(6)]
    for l in range(DEPTH):
        lw = _layer_weights(l, w_in, b_in, mla_w_uq, mla_w_uk, mla_w_uv, w_branch)
        qn = mla_q_norm[l].reshape(1, B_Q_RANK)
        kvn = mla_kv_norm[l].reshape(1, B_KV_RANK)
        wo = w_out[l].astype(BF16)
        w1, w3, w2 = w_ffn1[l].astype(BF16), w_ffn3[l].astype(BF16), w_ffn2[l].astype(BF16)
        mod = _ada_mod(c_all, w_ada[l], b_ada[l])

        def group(x, mod_g, tabs, st, sample):
            B, T, _ = x.shape
            sh1, sc1, g1, sh2, sc2, g2 = [mod_g[:, None, i * D:(i + 1) * D] for i in range(6)]
            ydt = F32 if sample else BF16
            z3 = _in_proj(x, sc1, sh1, lw['w_in'], lw['b_in']).reshape(B, T, Z_W)
            c0, n0, m0, buf0, s0 = st
            ya, c1, n1, m1 = _mlstm(z3, c0, n0, m0, ydt)
            yd, s1 = _retention(z3, tabs[1], s0, ydt)
            yc, buf1 = _conv(z3, buf0, conv_w[l], conv_b[l], conv_ln_g[l], conv_ln_b[l], ydt)
            if sample:
                q, rows = _mla_prep_sample(z3, tabs[0].reshape(1, T, 4 * LANE), qn, kvn,
                                           lw['wq1'], lw['wq2'], lw['wuk_t'])
                yb = _sample_attention(l, q, rows, lw['wv_h'], cache_t, page_table, ydt)
            else:
                q, k, v, rows = _mla_prep_prompt(z3, tabs[0], qn, kvn, lw['wq1'], lw['wq2'], lw['wk'], lw['wv'])
                yb = _flash(q, k, v)
            x1 = _merge(ya, yb, yc, yd, z3, x, g1, lw['wb_a'], lw['wb_b'], lw['wb_c'], lw['wb_d'], wo,
                        ln1_g[l], ln1_b[l])
            x2 = _ffn(x1, sc2, sh2, g2, w1, w3, w2, ln2_g[l], ln2_b[l])
            return x2, (rows, c1, n1, m1, buf1, s1)

        st_p = (jnp.zeros((Bp, A_HEADS, A_DK, A_DV), F32), jnp.zeros((Bp, A_HEADS, A_DK), F32),
                jnp.zeros((Bp, A_HEADS), F32), jnp.zeros((Bp, C_WIDTH - 1, C_CH), F32),
                jnp.zeros((Bp, D_HEADS, D_DK, D_DV), F32))
        xp, new_p = group(xp, mod[:Bp], tabs_p, st_p, False)
        st_d = (state_mlstm_C[l].astype(F32), state_mlstm_n[l].astype(F32), state_mlstm_m[l].astype(F32),
                state_conv[l].astype(F32), state_ret[l].astype(F32))
        xd, new_d = group(xd, mod[Bp:], tabs_d, st_d, True)
        for i in range(6):
            p_states[i].append(new_p[i])
            d_states[i].append(new_d[i])
    ps = [jnp.stack(s, axis=0).astype(dt) for s in p_states]
    ds = [jnp.stack(s, axis=0).astype(dt) for s in d_states]
    return (xp.astype(dt), xd.astype(dt), ps[0], ds[0], ps[1], ds[1], ps[2], ds[2], ps[3], ds[3],
            ps[4], ds[4], ps[5], ds[5])
```

```python
import functools

import numpy as np
import jax
import jax.numpy as jnp
from jax import lax
from jax.experimental import pallas as pl
from jax.experimental.pallas import tpu as pltpu

F32 = jnp.float32
BF16 = jnp.bfloat16

D_MODEL = 1024
DEPTH = 2
PAGE_SIZE = 128
N_BRANCH = 4
A_HEADS, A_DK, A_DV = 4, 128, 128
B_HEADS, B_Q_RANK, B_KV_RANK, B_NOPE, B_ROPE, B_VDIM = 8, 384, 256, 64, 32, 64
C_CH, C_WIDTH = 512, 31
D_HEADS, D_DK, D_DV = 4, 128, 128
FFN_HIDDEN = -(-8 * D_MODEL // (3 * 256)) * 256
CHUNK = 128
ROPE_BASE = 10000.0
LN_EPS = 1e-5
RMS_EPS = 1e-6
DEEPNORM_ALPHA = (2 * DEPTH) ** 0.25
CACHE_W = B_KV_RANK + B_ROPE
MLA_SCALE = (B_NOPE + B_ROPE) ** -0.5
LOG2E = float(np.log2(np.e))
O_AQ, O_AI, O_AF = 0, 2048, 2052
O_BQ, O_BKV, O_BKR = 2056, 2440, 2696
O_C, O_D, O_G = 2728, 3752, 5800
N_IN = 9896

Z_G, Z_A, Z_D, Z_C, Z_B = 0, 4096, 6144, 8192, 9216
Z_W = 9984
Z_BW = 768
MISC_KR, MISC_AI, MISC_AF, MISC_KRSW = 0, 32, 36, 64
LANE = 128
HP = 128

VMEM_LIMIT = 56 * 1024 * 1024


def _cp(sem, vmem=VMEM_LIMIT):
    return pltpu.CompilerParams(dimension_semantics=sem, vmem_limit_bytes=vmem)


def _sigmoid(x):
    return 1.0 / (1.0 + jnp.exp(-x))


def _silu(x):
    return x * _sigmoid(x)


def _log_sigmoid(x):
    return jnp.minimum(x, 0.0) - jnp.log(1.0 + jnp.exp(-jnp.abs(x)))


def _ln_rows(x):
    mu = jnp.mean(x, axis=-1, keepdims=True)
    xc = x - mu
    var = jnp.mean(xc * xc, axis=-1, keepdims=True)
    return xc * lax.rsqrt(var + LN_EPS)


def _rms_rows(x):
    return x * lax.rsqrt(jnp.mean(x * x, axis=-1, keepdims=True) + RMS_EPS)


def _dot(a, b):
    return jnp.dot(a, b, preferred_element_type=F32)


def _dot_nt(a, b):
    return lax.dot_general(a, b, (((1,), (1,)), ((), ())), preferred_element_type=F32)


def _bdot(a, b):
    return lax.dot_general(a, b, (((2,), (1,)), ((0,), (0,))), preferred_element_type=F32)


def _bdot_nt(a, b):
    return lax.dot_general(a, b, (((2,), (2,)), ((0,), (0,))), preferred_element_type=F32)


def _bdot_tn(a, b):
    return lax.dot_general(a, b, (((1,), (1,)), ((0,), (0,))), preferred_element_type=F32)


def _tile(n, pref, align=8):
    if n <= pref:
        return n
    for t in range(pref, 0, -1):
        if n % t == 0 and t % align == 0:
            return t
    return n


def _mod_kernel(c_ref, w_ref, b_ref, o_ref):
    s = _silu(c_ref[...])
    o_ref[...] = _dot(s.astype(BF16), w_ref[...].astype(BF16)) + b_ref[...]


def _ada_mod(c, w, b):
    m, d = c.shape
    n = w.shape[1]
    tn = _tile(n, 768, LANE)
    return pl.pallas_call(
        _mod_kernel,
        out_shape=jax.ShapeDtypeStruct((m, n), F32),
        grid=(n // tn,),
        in_specs=[pl.BlockSpec((m, d), lambda j: (0, 0)),
                  pl.BlockSpec((d, tn), lambda j: (0, j)),
                  pl.BlockSpec((1, tn), lambda j: (0, j))],
        out_specs=pl.BlockSpec((m, tn), lambda j: (0, j)),
        compiler_params=_cp(("arbitrary",)),
        name="ada_mod",
    )(c, w, b.reshape(1, n))


def _inproj_kernel(x_ref, sc_ref, sh_ref, w_ref, b_ref, o_ref, u_ref):
    @pl.when(pl.program_id(1) == 0)
    def _():
        u = x_ref[...] * (1.0 + sc_ref[...]) + sh_ref[...]
        u_ref[...] = u.reshape(u_ref.shape).astype(BF16)

    o_ref[...] = _dot(u_ref[...], w_ref[...]) + b_ref[...]


def _token_blocks(B, T, rows):
    if T >= rows:
        return 1, _tile(T, rows)
    return _tile(B, max(1, rows // T), 1), T


def _in_proj(x, sc, sh, w, b):
    B, T, D = x.shape
    n = w.shape[1]
    bb, tt = _token_blocks(B, T, 1024)
    tm = bb * tt
    nt = T // tt
    tn = Z_BW
    grid = ((B // bb) * nt, n // tn)
    return pl.pallas_call(
        _inproj_kernel,
        out_shape=jax.ShapeDtypeStruct((B * T, n), F32),
        grid=grid,
        in_specs=[pl.BlockSpec((bb, tt, D), lambda i, j: (i // nt, i % nt, 0)),
                  pl.BlockSpec((bb, 1, D), lambda i, j: (i // nt, 0, 0)),
                  pl.BlockSpec((bb, 1, D), lambda i, j: (i // nt, 0, 0)),
                  pl.BlockSpec((D, tn), lambda i, j: (0, j)),
                  pl.BlockSpec((1, tn), lambda i, j: (0, j))],
        out_specs=pl.BlockSpec((tm, tn), lambda i, j: (i, j)),
        scratch_shapes=[pltpu.VMEM((tm, D), BF16)],
        compiler_params=_cp(("parallel", "arbitrary")),
        name="in_proj",
    )(x, sc, sh, w, b)


def _stack_heads(ref, n_heads, width, lo=0):
    return jnp.concatenate([ref[:, :, lo + h * width:lo + (h + 1) * width] for h in range(n_heads)], axis=0)


def _mlstm_cells(q, k, v, og, ig_col, af_col, c_prev, n_prev, m_prev, L, mxu):
    row = lax.broadcasted_iota(jnp.int32, (L, L), 0)
    col = lax.broadcasted_iota(jnp.int32, (L, L), 1)
    tril = col <= row
    eye = col == row
    lf_col = _log_sigmoid(af_col)
    ig_row = jnp.sum(jnp.where(eye, ig_col, 0.0), axis=1, keepdims=True)
    b_row = jnp.sum(jnp.where(row <= col, lf_col, 0.0), axis=1, keepdims=True)
    b_col = jnp.sum(jnp.where(eye, b_row, 0.0), axis=2, keepdims=True)
    inter = b_col + m_prev
    intra = jnp.where(tril, b_col - b_row + ig_row, -jnp.inf)
    m_t = jnp.maximum(inter, jnp.max(intra, axis=2, keepdims=True))
    w_inter = jnp.exp(inter - m_t)
    dmat = jnp.exp(intra - m_t)
    qh = q * (A_DK ** -0.5)
    qb, kb, vb = qh.astype(mxu), k.astype(mxu), v.astype(mxu)
    s = _bdot_nt(qb, kb) * dmat
    num = _bdot(s.astype(mxu), vb) + w_inter * _bdot(qb, c_prev.astype(mxu))
    den = jnp.sum(s, axis=2, keepdims=True) + w_inter * jnp.sum(qh * n_prev, axis=2, keepdims=True)
    hv = num / jnp.maximum(jnp.abs(den), jnp.exp(-m_t))
    y = _sigmoid(og) * _ln_rows(hv)
    b_last = b_col[:, L - 1:L, :]
    m_new = m_t[:, L - 1:L, :]
    w_c = jnp.exp(b_last + m_prev - m_new)
    w_s = jnp.exp(b_last - b_col + ig_col - m_new)
    kw = k * w_s
    c_new = w_c * c_prev + _bdot_tn(kw.astype(mxu), vb)
    n_new = w_c * n_prev + jnp.sum(kw, axis=1, keepdims=True)
    return y, c_new, n_new, m_new


def _mlstm_kernel(q_ref, k_ref, v_ref, o_ref, g_ref, c0_ref, n0_ref, m0_ref,
                  y_ref, c_out, n_out, m_out, c_s, n_s, m_s, *, L, mxu, bb, carry):
    H = A_HEADS
    if carry:
        @pl.when(pl.program_id(1) == 0)
        def _():
            c_s[...] = c0_ref[0]
            n_s[...] = n0_ref[0]
            m_s[...] = m0_ref[0]
        c_prev, n_prev, m_prev = c_s[...], n_s[...], m_s[:, :, 0:1]
    else:
        c_prev = jnp.concatenate([c0_ref[:, h] for h in range(H)], axis=0)
        n_prev = jnp.concatenate([n0_ref[:, h] for h in range(H)], axis=0)
        m_prev = jnp.concatenate([m0_ref[:, h, :, 0:1] for h in range(H)], axis=0)

    y, c_new, n_new, m_new = _mlstm_cells(
        _stack_heads(q_ref, H, A_DK), _stack_heads(k_ref, H, A_DK), _stack_heads(v_ref, H, A_DV),
        _stack_heads(o_ref, H, A_DV), _stack_heads(g_ref, H, 1, MISC_AI), _stack_heads(g_ref, H, 1, MISC_AF),
        c_prev, n_prev, m_prev, L, mxu)
    m_new = jnp.broadcast_to(m_new, (H * bb, 1, LANE))
    for h in range(H):
        rows = slice(h * bb, (h + 1) * bb)
        y_ref[:, :, h * A_DV:(h + 1) * A_DV] = y[rows].astype(y_ref.dtype)
        if not carry:
            c_out[:, h] = c_new[rows]
            n_out[:, h] = n_new[rows]
            m_out[:, h] = m_new[rows]
    if carry:
        c_s[...] = c_new
        n_s[...] = n_new
        m_s[...] = m_new

        @pl.when(pl.program_id(1) == pl.num_programs(1) - 1)
        def _():
            c_out[0] = c_new
            n_out[0] = n_new
            m_out[0] = m_new


SMALL_SEQ_BATCH = 8


def _mlstm(z3, c0, n0, m0, ydt):
    B, T, _ = z3.shape
    L = min(CHUNK, T)
    nc = T // L
    carry = nc > 1
    bb = 1 if carry else _tile(B, SMALL_SEQ_BATCH, 1)
    H = A_HEADS
    W = H * A_DK
    ab = Z_A // W
    n0r = n0.reshape(B, H, 1, A_DK)
    m0b = jnp.broadcast_to(m0[..., None, None], (B, H, 1, LANE))
    mxu = BF16 if L >= 16 else F32

    def zspec(k):
        return pl.BlockSpec((bb, L, W), lambda b, c: (b, c, ab + k))

    st_specs = (pl.BlockSpec((bb, H, A_DK, A_DV), lambda b, c: (b, 0, 0, 0)),
                pl.BlockSpec((bb, H, 1, A_DK), lambda b, c: (b, 0, 0, 0)),
                pl.BlockSpec((bb, H, 1, LANE), lambda b, c: (b, 0, 0, 0)))
    y, c1, n1, m1 = pl.pallas_call(
        functools.partial(_mlstm_kernel, L=L, mxu=mxu, bb=bb, carry=carry),
        out_shape=(jax.ShapeDtypeStruct((B, T, W), ydt),
                   jax.ShapeDtypeStruct((B, H, A_DK, A_DV), F32),
                   jax.ShapeDtypeStruct((B, H, 1, A_DK), F32),
                   jax.ShapeDtypeStruct((B, H, 1, LANE), F32)),
        grid=(B // bb, nc),
        in_specs=[zspec(0), zspec(1), zspec(2), zspec(3),
                  pl.BlockSpec((bb, L, LANE), lambda b, c: (b, c, (Z_B + Z_BW - LANE) // LANE))] + list(st_specs),
        out_specs=(pl.BlockSpec((bb, L, W), lambda b, c: (b, c, 0)),) + st_specs,
        scratch_shapes=[pltpu.VMEM((H, A_DK, A_DV), F32), pltpu.VMEM((H, 1, A_DK), F32),
                        pltpu.VMEM((H, 1, LANE), F32)],
        compiler_params=_cp(("parallel", "arbitrary")),
        name="mlstm",
    )(z3, z3, z3, z3, z3, c0, n0r, m0b)
    return y, c1, n1.reshape(B, H, A_DK), m1[:, :, 0, 0]


_LOG_GAMMA = tuple(float(np.log(1.0 - 2.0 ** (-5.0 - h))) for h in range(D_HEADS))


def _ret_kernel(q_ref, k_ref, v_ref, g_ref, tab_ref, s0_ref, y_ref, s_out, s_s, *, L, mxu, bb, carry):
    if carry:
        @pl.when(pl.program_id(1) == 0)
        def _():
            s_s[...] = s0_ref[0]

    H = D_HEADS
    X = H * bb
    cos = tab_ref[:, 0:D_DK]
    sin = tab_ref[:, D_DK:2 * D_DK]
    row = lax.broadcasted_iota(jnp.int32, (L, L), 0)
    col = lax.broadcasted_iota(jnp.int32, (L, L), 1)
    diff = (row - col).astype(F32)
    jcol = lax.broadcasted_iota(jnp.int32, (L, 1), 0).astype(F32)

    def per_head(fn, shape):
        return jnp.concatenate([jnp.broadcast_to(fn(_LOG_GAMMA[h])[None], (bb,) + shape) for h in range(H)], axis=0)

    decay = per_head(lambda lg: jnp.where(diff >= 0.0, jnp.exp(jnp.maximum(diff, 0.0) * lg), 0.0), (L, L))
    w_in = per_head(lambda lg: jnp.exp((jcol + 1.0) * lg), (L, 1))
    w_st = per_head(lambda lg: jnp.exp((L - 1.0 - jcol) * lg), (L, 1))
    w_S = per_head(lambda lg: jnp.full((1, 1), float(np.exp(L * lg)), F32), (1, 1))

    def rotary(x):
        swapped = pltpu.roll(x.reshape(X * L, D_DK), D_DK // 2, 1).reshape(X, L, D_DK)
        return x * cos + swapped * sin

    qh = rotary(_stack_heads(q_ref, H, D_DK))
    kh = rotary(_stack_heads(k_ref, H, D_DK)) * (D_DK ** -0.5)
    vb = _stack_heads(v_ref, H, D_DV).astype(mxu)
    qb = qh.astype(mxu)
    s = _bdot_nt(qb, kh.astype(mxu)) * decay
    s_prev = s_s[...] if carry else jnp.concatenate([s0_ref[:, h] for h in range(H)], axis=0)
    o = _bdot(s.astype(mxu), vb) + w_in * _bdot(qb, s_prev.astype(mxu))
    y = _silu(_stack_heads(g_ref, H, D_DV)) * _ln_rows(o)
    s_new = w_S * s_prev + _bdot_tn((kh * w_st).astype(mxu), vb)
    for h in range(H):
        rows = slice(h * bb, (h + 1) * bb)
        y_ref[:, :, h * D_DV:(h + 1) * D_DV] = y[rows].astype(y_ref.dtype)
        if not carry:
            s_out[:, h] = s_new[rows]
    if carry:
        s_s[...] = s_new

        @pl.when(pl.program_id(1) == pl.num_programs(1) - 1)
        def _():
            s_out[0] = s_new


def _retention(z3, tab, s0, ydt):
    B, T, _ = z3.shape
    L = min(CHUNK, T)
    nc = T // L
    carry = nc > 1
    bb = 1 if carry else _tile(B, SMALL_SEQ_BATCH, 1)
    H = D_HEADS
    W = H * D_DK
    db = Z_D // W
    mxu = BF16 if L >= 16 else F32

    def zspec(k):
        return pl.BlockSpec((bb, L, W), lambda b, c: (b, c, db + k))

    st_spec = pl.BlockSpec((bb, H, D_DK, D_DV), lambda b, c: (b, 0, 0, 0))
    return pl.pallas_call(
        functools.partial(_ret_kernel, L=L, mxu=mxu, bb=bb, carry=carry),
        out_shape=(jax.ShapeDtypeStruct((B, T, W), ydt),
                   jax.ShapeDtypeStruct((B, H, D_DK, D_DV), F32)),
        grid=(B // bb, nc),
        in_specs=[zspec(0), zspec(1), zspec(2), zspec(3),
                  pl.BlockSpec((L, 2 * D_DK), lambda b, c: (c, 0)), st_spec],
        out_specs=(pl.BlockSpec((bb, L, W), lambda b, c: (b, c, 0)), st_spec),
        scratch_shapes=[pltpu.VMEM((H, D_DK, D_DV), F32)],
        compiler_params=_cp(("parallel", "arbitrary")),
        name="retention",
    )(z3, z3, z3, z3, tab, s0)


_CPAD = 32


def _conv_kernel(glu_ref, buf_ref, w_ref, b_ref, g_ref, be_ref, y_ref, buf_out, ext, *, tt):
    ti = pl.program_id(1)
    lead = _CPAD - (C_WIDTH - 1)
    bb = ext.shape[0]

    @pl.when(ti == 0)
    def _():
        ext[:, 0:lead, :] = jnp.zeros((bb, lead, C_CH), F32)
        ext[:, lead:_CPAD, :] = buf_ref[...]

    glu = glu_ref[...]
    ext[:, _CPAD:_CPAD + tt, :] = glu[:, :, :C_CH] * _sigmoid(glu[:, :, C_CH:])
    acc = jnp.zeros((bb, tt, C_CH), F32)
    for j in range(C_WIDTH):
        acc = acc + ext[:, lead + j:lead + j + tt, :] * w_ref[j:j + 1, :]
    y = _ln_rows(acc + b_ref[...]) * g_ref[...] + be_ref[...]
    y_ref[...] = _silu(y).astype(y_ref.dtype)

    @pl.when(ti == pl.num_programs(1) - 1)
    def _():
        buf_out[...] = ext[:, tt + lead:tt + _CPAD, :]

    ext[:, 0:_CPAD, :] = ext[:, tt:tt + _CPAD, :]


def _conv(z3, buf0, w, b, g, be, ydt):
    B, T, _ = z3.shape
    bb, tt = _token_blocks(B, T, 256)
    nt = T // tt
    vec = lambda a: a.reshape(1, C_CH)
    return pl.pallas_call(
        functools.partial(_conv_kernel, tt=tt),
        out_shape=(jax.ShapeDtypeStruct((B, T, C_CH), ydt),
                   jax.ShapeDtypeStruct((B, C_WIDTH - 1, C_CH), F32)),
        grid=(B // bb, nt),
        in_specs=[pl.BlockSpec((bb, tt, 2 * C_CH), lambda b, t: (b, t, Z_C // (2 * C_CH))),
                  pl.BlockSpec((bb, C_WIDTH - 1, C_CH), lambda b, t: (b, 0, 0)),
                  pl.BlockSpec((C_WIDTH, C_CH), lambda b, t: (0, 0)),
                  pl.BlockSpec((1, C_CH), lambda b, t: (0, 0)),
                  pl.BlockSpec((1, C_CH), lambda b, t: (0, 0)),
                  pl.BlockSpec((1, C_CH), lambda b, t: (0, 0))],
        out_specs=(pl.BlockSpec((bb, tt, C_CH), lambda b, t: (b, t, 0)),
                   pl.BlockSpec((bb, C_WIDTH - 1, C_CH), lambda b, t: (b, 0, 0))),
        scratch_shapes=[pltpu.VMEM((bb, tt + _CPAD, C_CH), F32)],
        compiler_params=_cp(("parallel", "arbitrary")),
        name="conv_module",
    )(z3, buf0, w, vec(b), vec(g), vec(be))


def _mla_common(zb_ref, tab, qn_ref, kvn_ref, wq1_ref, wq2_ref, rows_ref, q_scale):
    zb = zb_ref[...].reshape(-1, Z_BW)
    bq = zb[:, 0:B_Q_RANK]
    bkv = zb[:, B_Q_RANK:B_Q_RANK + B_KV_RANK]
    misc = zb[:, B_Q_RANK + B_KV_RANK:]
    qn = (_rms_rows(bq) * qn_ref[...]).astype(BF16)
    ckv = _rms_rows(bkv) * kvn_ref[...]
    cos_q = jnp.concatenate([tab[:, 0:LANE]] * B_HEADS, axis=1)
    sin_q = jnp.concatenate([tab[:, LANE:2 * LANE]] * B_HEADS, axis=1)
    q = (_dot(qn, wq1_ref[...]) * cos_q + _dot(qn, wq2_ref[...]) * sin_q) * q_scale
    kpe = misc * tab[:, 2 * LANE:3 * LANE] + pltpu.roll(misc, LANE // 2, 1) * tab[:, 3 * LANE:]
    lead = rows_ref.shape[:-1]
    rows_ref[:, :, 0:B_KV_RANK] = ckv.reshape(lead + (B_KV_RANK,))
    rows_ref[:, :, B_KV_RANK:CACHE_W] = kpe[:, 0:B_ROPE].reshape(lead + (B_ROPE,))
    return q, ckv, kpe


def _mla_prep_prompt_kernel(zb_ref, tab_ref, qn_ref, kvn_ref, wq1_ref, wq2_ref, wk_ref, wv_ref,
                            q_ref, k_ref, v_ref, rows_ref):
    q, ckv, kpe = _mla_common(zb_ref, tab_ref[...], qn_ref, kvn_ref, wq1_ref, wq2_ref, rows_ref,
                              MLA_SCALE * LOG2E)
    q_ref[0] = q.astype(BF16)
    cb = ckv.astype(BF16)
    kpe_hi = pltpu.roll(kpe, LANE // 2, 1)
    k = _dot(cb, wk_ref[...]) + jnp.concatenate([kpe_hi] * B_HEADS, axis=1)
    k_ref[0] = k.astype(BF16)
    lane = lax.broadcasted_iota(jnp.int32, (1, B_HEADS * HP), 1)
    ones_col = jnp.where(lane % HP == B_VDIM, 1.0, 0.0)
    v_ref[0] = (_dot(cb, wv_ref[...]) + ones_col).astype(BF16)


def _mla_prep_prompt(z3, tab, qn, kvn, wq1, wq2, wk, wv):
    B, T, _ = z3.shape
    tt = _tile(T, 512)
    nt = T // tt
    HW = B_HEADS * HP
    full = lambda a: pl.BlockSpec(a.shape, lambda b, t: (0,) * a.ndim)
    big = pl.BlockSpec((1, tt, HW), lambda b, t: (b, t, 0))
    return pl.pallas_call(
        _mla_prep_prompt_kernel,
        out_shape=(jax.ShapeDtypeStruct((B, T, HW), BF16),) * 3
        + (jax.ShapeDtypeStruct((B, T, CACHE_W), F32),),
        grid=(B, nt),
        in_specs=[pl.BlockSpec((1, tt, Z_BW), lambda b, t: (b, t, Z_B // Z_BW)),
                  pl.BlockSpec((tt, 4 * LANE), lambda b, t: (t, 0)),
                  full(qn), full(kvn), full(wq1), full(wq2), full(wk), full(wv)],
        out_specs=(big, big, big, pl.BlockSpec((1, tt, CACHE_W), lambda b, t: (b, t, 0))),
        compiler_params=_cp(("parallel", "parallel")),
        name="mla_prep_prompt",
    )(z3, tab, qn, kvn, wq1, wq2, wk, wv)


def _flash_kernel(q_ref, k_ref, v_ref, o_ref, *, tq):
    qi = pl.program_id(2)
    q = q_ref[0]

    def step(kj, carry, causal):
        m, acc = carry
        start = pl.multiple_of(kj * tq, tq)
        k = k_ref[0, pl.ds(start, tq), :]
        v = v_ref[0, pl.ds(start, tq), :]
        s = _dot_nt(q, k)
        if causal:
            row = lax.broadcasted_iota(jnp.int32, (tq, tq), 0)
            col = lax.broadcasted_iota(jnp.int32, (tq, tq), 1)
            s = jnp.where(col <= row, s, -jnp.inf)
        m_new = jnp.maximum(m, jnp.max(s, axis=1, keepdims=True))
        acc = jnp.exp2(m - m_new) * acc + _dot(jnp.exp2(s - m_new).astype(BF16), v)
        return m_new, acc

    init = (jnp.full((tq, 1), -jnp.inf, F32), jnp.zeros((tq, HP), F32))
    carry = lax.fori_loop(0, qi, lambda kj, c: step(kj, c, False), init)
    _, acc = step(qi, carry, True)
    o_ref[0] = (acc / acc[:, B_VDIM:B_VDIM + 1]).astype(o_ref.dtype)


def _flash(q, k, v):
    B, T, HW = q.shape
    tq = _tile(T, 512)
    return pl.pallas_call(
        functools.partial(_flash_kernel, tq=tq),
        out_shape=jax.ShapeDtypeStruct((B, T, HW), BF16),
        grid=(B, B_HEADS, T // tq),
        in_specs=[pl.BlockSpec((1, tq, HP), lambda b, h, i: (b, i, h)),
                  pl.BlockSpec((1, T, HP), lambda b, h, i: (b, 0, h)),
                  pl.BlockSpec((1, T, HP), lambda b, h, i: (b, 0, h))],
        out_specs=pl.BlockSpec((1, tq, HP), lambda b, h, i: (b, i, h)),
        compiler_params=_cp(("parallel", "parallel", "arbitrary")),
        name="mla_prompt_attention",
    )(q, k, v)


def _mla_prep_sample_kernel(zb_ref, tab_ref, qn_ref, kvn_ref, wq1_ref, wq2_ref, wuk_ref,
                            q_ref, rows_ref):
    bb, _, tt, _ = q_ref.shape
    tab = jnp.broadcast_to(tab_ref[...], (bb, tt, 4 * LANE)).reshape(bb * tt, 4 * LANE)
    q, _, _ = _mla_common(zb_ref, tab, qn_ref, kvn_ref, wq1_ref, wq2_ref, rows_ref, MLA_SCALE)
    lane = lax.broadcasted_iota(jnp.int32, (1, HP), 1)
    for h in range(B_HEADS):
        qh = q[:, h * HP:(h + 1) * HP]
        q_abs = _dot(qh.astype(BF16), wuk_ref[h])
        q_pe = jnp.where(lane < B_ROPE, pltpu.roll(qh, LANE // 2, 1), 0.0)
        q_ref[:, h, :, 0:B_KV_RANK] = q_abs.reshape(bb, tt, B_KV_RANK)
        q_ref[:, h, :, B_KV_RANK:] = q_pe[:, 0:B_ROPE].reshape(bb, tt, B_ROPE)


def _mla_prep_sample(z3, tab, qn, kvn, wq1, wq2, wuk):
    B, T, _ = z3.shape
    bb = _tile(B, max(1, 512 // T), 1)
    full = lambda a: pl.BlockSpec(a.shape, lambda i: (0,) * a.ndim)
    return pl.pallas_call(
        _mla_prep_sample_kernel,
        out_shape=(jax.ShapeDtypeStruct((B, B_HEADS, T, CACHE_W), F32),
                   jax.ShapeDtypeStruct((B, T, CACHE_W), F32)),
        grid=(B // bb,),
        in_specs=[pl.BlockSpec((bb, T, Z_BW), lambda i: (i, 0, Z_B // Z_BW)),
                  pl.BlockSpec((1, T, 4 * LANE), lambda i: (0, 0, 0)),
                  full(qn), full(kvn), full(wq1), full(wq2), full(wuk)],
        out_specs=(pl.BlockSpec((bb, B_HEADS, T, CACHE_W), lambda i: (i, 0, 0, 0)),
                   pl.BlockSpec((bb, T, CACHE_W), lambda i: (i, 0, 0))),
        compiler_params=_cp(("parallel",)),
        name="mla_prep_sample",
    )(z3, tab, qn, kvn, wq1, wq2, wuk)


def _sattn_kernel(pt_ref, q_ref, new_ref, wv_ref, cache_ref, y_ref, kbuf, sem, *, layer, n_pages, T):
    b = pl.program_id(0)
    R = B_HEADS * T

    def page_copy(row, g, slot):
        return pltpu.make_async_copy(cache_ref.at[layer, pt_ref[row, g]], kbuf.at[slot, g], sem.at[slot])

    def fetch(row, slot):
        for g in range(n_pages):
            page_copy(row, g, slot).start()

    @pl.when(b == 0)
    def _():
        fetch(0, 0)

    @pl.when(b + 1 < pl.num_programs(0))
    def _():
        fetch(b + 1, (b + 1) % 2)

    slot = b % 2
    for g in range(n_pages):
        page_copy(b, g, slot).wait()

    q = q_ref[0].reshape(R, CACHE_W).astype(BF16)
    kt = jnp.concatenate([kbuf[slot, g].astype(BF16) for g in range(n_pages)], axis=1)
    pad = jnp.zeros((PAGE_SIZE - T, CACHE_W), F32)
    new = jnp.concatenate([new_ref[0], pad], axis=0).astype(BF16)
    r = lax.broadcasted_iota(jnp.int32, (R, PAGE_SIZE), 0)
    c = lax.broadcasted_iota(jnp.int32, (R, PAGE_SIZE), 1)
    s_past = _dot(q, kt)
    s_new = jnp.where(c <= r % T, _dot_nt(q, new), -jnp.inf)
    m = jnp.maximum(jnp.max(s_past, axis=1, keepdims=True), jnp.max(s_new, axis=1, keepdims=True))
    p_past = jnp.exp(s_past - m)
    p_new = jnp.exp(s_new - m)
    l = jnp.sum(p_past, axis=1, keepdims=True) + jnp.sum(p_new, axis=1, keepdims=True)
    acc = _dot_nt(p_past.astype(BF16), kt[0:B_KV_RANK, :]) + _dot(p_new.astype(BF16), new[:, 0:B_KV_RANK])
    o = acc / l
    for h in range(B_HEADS):
        oh = o[h * T:(h + 1) * T, :].astype(BF16)
        y_ref[0, :, h * HP:(h + 1) * HP] = _dot(oh, wv_ref[h]).astype(y_ref.dtype)


def _sample_attention(layer, q, new_rows, wv, cache_t, page_table, ydt):
    B, H, T, _ = q.shape
    n_pages = page_table.shape[1]
    grid_spec = pltpu.PrefetchScalarGridSpec(
        num_scalar_prefetch=1,
        grid=(B,),
        in_specs=[pl.BlockSpec((1, H, T, CACHE_W), lambda b, pt: (b, 0, 0, 0)),
                  pl.BlockSpec((1, T, CACHE_W), lambda b, pt: (b, 0, 0)),
                  pl.BlockSpec(wv.shape, lambda b, pt: (0, 0, 0)),
                  pl.BlockSpec(memory_space=pl.ANY)],
        out_specs=pl.BlockSpec((1, T, H * HP), lambda b, pt: (b, 0, 0)),
        scratch_shapes=[pltpu.VMEM((2, n_pages, CACHE_W, PAGE_SIZE), F32),
                        pltpu.SemaphoreType.DMA((2,))],
    )
    return pl.pallas_call(
        functools.partial(_sattn_kernel, layer=layer, n_pages=n_pages, T=T),
        out_shape=jax.ShapeDtypeStruct((B, T, H * HP), ydt),
        grid_spec=grid_spec,
        compiler_params=_cp(("arbitrary",)),
        name="mla_sample_attention",
    )(page_table, q, new_rows, wv, cache_t)


def _merge_kernel(ya_ref, yb_ref, yc_ref, yd_ref, gt_ref, x_ref, g1_ref, wa_ref, wb_ref, wc_ref, wd_ref,
                  wo_ref, lg_ref, lb_ref, o_ref):
    rows = o_ref.shape[0] * o_ref.shape[1]

    def flat(ref):
        return ref[...].reshape(rows, ref.shape[-1])

    gates = flat(gt_ref)
    acc = None
    for n, (y_ref, w_ref) in enumerate(((ya_ref, wa_ref), (yb_ref, wb_ref), (yc_ref, wc_ref), (yd_ref, wd_ref))):
        term = _sigmoid(gates[:, n * D_MODEL:(n + 1) * D_MODEL]) * _dot(flat(y_ref).astype(BF16), w_ref[...])
        acc = term if acc is None else acc + term
    mix = _dot(acc.astype(BF16), wo_ref[...]).reshape(o_ref.shape)
    v = DEEPNORM_ALPHA * x_ref[...] + g1_ref[...] * mix
    o_ref[...] = _ln_rows(v) * lg_ref[...] + lb_ref[...]


def _merge(ya, yb, yc, yd, z3, x, g1, wa, wb, wc, wd, wo, lg, lb):
    B, T, D = x.shape
    bb, tt = _token_blocks(B, T, 256)
    nt = T // tt
    tok = lambda w: pl.BlockSpec((bb, tt, w), lambda i: (i // nt, i % nt, 0))
    full = lambda a: pl.BlockSpec(a.shape, lambda i: (0,) * a.ndim)
    vec = lambda a: a.reshape(1, 1, D)
    return pl.pallas_call(
        _merge_kernel,
        out_shape=jax.ShapeDtypeStruct((B, T, D), F32),
        grid=((B // bb) * nt,),
        in_specs=[tok(ya.shape[-1]), tok(yb.shape[-1]), tok(yc.shape[-1]), tok(yd.shape[-1]),
                  pl.BlockSpec((bb, tt, N_BRANCH * D), lambda i: (i // nt, i % nt, Z_G // (N_BRANCH * D_MODEL))),
                  tok(D),
                  pl.BlockSpec((bb, 1, D), lambda i: (i // nt, 0, 0)),
                  full(wa), full(wb), full(wc), full(wd), full(wo),
                  pl.BlockSpec((1, 1, D), lambda i: (0, 0, 0)), pl.BlockSpec((1, 1, D), lambda i: (0, 0, 0))],
        out_specs=tok(D),
        compiler_params=_cp(("parallel",)),
        name="merge_out_ln",
    )(ya, yb, yc, yd, z3, x, g1, wa, wb, wc, wd, wo, vec(lg), vec(lb))


def _ffn_kernel(x_ref, sc_ref, sh_ref, g2_ref, w1_ref, w3_ref, w2_ref, lg_ref, lb_ref, o_ref, u_s, acc_s):
    j = pl.program_id(1)

    @pl.when(j == 0)
    def _():
        u = x_ref[...] * (1.0 + sc_ref[...]) + sh_ref[...]
        u_s[...] = u.reshape(u_s.shape).astype(BF16)
        acc_s[...] = jnp.zeros(acc_s.shape, F32)

    u = u_s[...]
    hid = _silu(_dot(u, w1_ref[...])) * _dot(u, w3_ref[...])
    acc_s[...] += _dot(hid.astype(BF16), w2_ref[...])

    @pl.when(j == pl.num_programs(1) - 1)
    def _():
        v = DEEPNORM_ALPHA * x_ref[...] + g2_ref[...] * acc_s[...].reshape(o_ref.shape)
        o_ref[...] = _ln_rows(v) * lg_ref[...] + lb_ref[...]


def _ffn(x, sc, sh, g2, w1, w3, w2, lg, lb):
    B, T, D = x.shape
    Hd = w1.shape[1]
    bb, tt = _token_blocks(B, T, 512)
    nt = T // tt
    tm = bb * tt
    th = _tile(Hd, 1408, LANE)
    tok = pl.BlockSpec((bb, tt, D), lambda i, j: (i // nt, i % nt, 0))
    per_b = pl.BlockSpec((bb, 1, D), lambda i, j: (i // nt, 0, 0))
    vspec = pl.BlockSpec((1, 1, D), lambda i, j: (0, 0, 0))
    vec = lambda a: a.reshape(1, 1, D)
    return pl.pallas_call(
        _ffn_kernel,
        out_shape=jax.ShapeDtypeStruct((B, T, D), F32),
        grid=((B // bb) * nt, Hd // th),
        in_specs=[tok, per_b, per_b, per_b,
                  pl.BlockSpec((D, th), lambda i, j: (0, j)),
                  pl.BlockSpec((D, th), lambda i, j: (0, j)),
                  pl.BlockSpec((th, D), lambda i, j: (j, 0)),
                  vspec, vspec],
        out_specs=tok,
        scratch_shapes=[pltpu.VMEM((tm, D), BF16), pltpu.VMEM((tm, D), F32)],
        compiler_params=_cp(("parallel", "arbitrary")),
        name="ffn_ln",
    )(x, sc, sh, g2, w1, w3, w2, vec(lg), vec(lb))


def _rope_tables(pos):
    posf = pos.astype(F32)[:, None]
    T = pos.shape[0]

    def cs(d):
        inv = ROPE_BASE ** (-jnp.arange(0, d, 2, dtype=F32) / d)
        ang = posf * inv[None, :]
        c, s = jnp.cos(ang), jnp.sin(ang)
        return jnp.concatenate([c, c], axis=1), jnp.concatenate([-s, s], axis=1)

    c32, s32 = cs(B_ROPE)
    z = lambda w: jnp.zeros((T, w), F32)
    cos_q = jnp.concatenate([jnp.ones((T, B_NOPE), F32), c32, z(HP - B_NOPE - B_ROPE)], axis=1)
    sin_q = jnp.concatenate([z(B_NOPE), s32, z(HP - B_NOPE - B_ROPE)], axis=1)
    cos_k = jnp.concatenate([c32, z(LANE - B_ROPE)], axis=1)
    sin_k = jnp.concatenate([s32, z(LANE - B_ROPE)], axis=1)
    tab_b = jnp.concatenate([cos_q, sin_q, cos_k, sin_k], axis=1)
    c128, s128 = cs(D_DK)
    return tab_b, jnp.concatenate([c128, s128], axis=1)


def _swap_halves(a, lo, width):
    half = width // 2
    return jnp.concatenate([a[..., lo + half:lo + width], a[..., lo:lo + half]], axis=-1)


def _layer_weights(l, w_in, b_in, mla_w_uq, mla_w_uk, mla_w_uv, w_branch):
    def reorder(a):
        zeros = lambda w: jnp.zeros(a.shape[:-1] + (w,), a.dtype)
        return jnp.concatenate([
            a[..., O_G:N_IN], a[..., O_AQ:O_AI], a[..., O_D:O_G], a[..., O_C:O_D],
            a[..., O_BQ:O_BKR + B_ROPE], a[..., O_AI:O_BQ], zeros(MISC_KRSW - MISC_AF - A_HEADS),
            _swap_halves(a, O_BKR, B_ROPE), zeros(LANE - MISC_KRSW - B_ROPE)], axis=-1)

    w_in_r = reorder(w_in[l]).astype(BF16)
    b_in_r = reorder(b_in[l]).reshape(1, Z_W)

    hd = B_NOPE + B_ROPE
    wq = mla_w_uq[l].reshape(B_Q_RANK, B_HEADS, hd)
    zq = jnp.zeros((B_Q_RANK, B_HEADS, HP - hd), F32)
    wq1 = jnp.concatenate([wq, zq], axis=-1).reshape(B_Q_RANK, B_HEADS * HP).astype(BF16)
    wq2 = jnp.concatenate([jnp.zeros((B_Q_RANK, B_HEADS, B_NOPE), F32), _swap_halves(wq, B_NOPE, B_ROPE), zq],
                          axis=-1).reshape(B_Q_RANK, B_HEADS * HP).astype(BF16)
    wuk = mla_w_uk[l]
    wuv = mla_w_uv[l]
    wk = jnp.concatenate([wuk, jnp.zeros((B_KV_RANK, B_HEADS, HP - B_NOPE), F32)], axis=-1)
    wv = jnp.concatenate([wuv, jnp.zeros((B_KV_RANK, B_HEADS, HP - B_VDIM), F32)], axis=-1)
    wk_flat = wk.reshape(B_KV_RANK, B_HEADS * HP).astype(BF16)
    wv_flat = wv.reshape(B_KV_RANK, B_HEADS * HP).astype(BF16)
    wuk_t = jnp.transpose(wk, (1, 2, 0)).astype(BF16)
    wv_h = jnp.transpose(wv, (1, 0, 2)).astype(BF16)
    wb = w_branch[l]
    wb_b = jnp.concatenate([wb[1].reshape(B_HEADS, B_VDIM, D_MODEL),
                            jnp.zeros((B_HEADS, HP - B_VDIM, D_MODEL), F32)], axis=1)
    wb_b = wb_b.reshape(B_HEADS * HP, D_MODEL).astype(BF16)
    return dict(w_in=w_in_r, b_in=b_in_r, wq1=wq1, wq2=wq2, wk=wk_flat, wv=wv_flat, wuk_t=wuk_t, wv_h=wv_h,
                wb_a=wb[0].astype(BF16), wb_b=wb_b, wb_c=wb[2].astype(BF16), wb_d=wb[3].astype(BF16))


def kernel(x_prompt, x_sample, cache_mla, state_mlstm_C, state_mlstm_n, state_mlstm_m, state_conv, state_ret,
           page_table, c_prompt, c_sample, w_ada, b_ada, w_in, b_in, mla_q_norm, mla_w_uq, mla_kv_norm,
           mla_w_uk, mla_w_uv, conv_w, conv_b, conv_ln_g, conv_ln_b, w_branch, w_out, ln1_g, ln1_b,
           w_ffn1, w_ffn3, w_ffn2, ln2_g, ln2_b):
    dt = x_prompt.dtype
    Bp, Tp, D = x_prompt.shape
    Bd, Td, _ = x_sample.shape
    past_len = page_table.shape[1] * PAGE_SIZE
    cache_t = jnp.swapaxes(cache_mla, 2, 3)
    tabs_p = _rope_tables(jnp.arange(Tp))
    tabs_d = _rope_tables(past_len + jnp.arange(Td))
    c_all = jnp.concatenate([c_prompt, c_sample], axis=0).astype(F32)

    xp, xd = x_prompt.astype(F32), x_sample.astype(F32)
    p_states = [[] for _ in range(6)]
    d_states = [[] for _ in range(6)]
    for l in range(DEPTH):
        lw = _layer_weights(l, w_in, b_in, mla_w_uq, mla_w_uk, mla_w_uv, w_branch)
        qn = mla_q_norm[l].reshape(1, B_Q_RANK)
        kvn = mla_kv_norm[l].reshape(1, B_KV_RANK)
        wo = w_out[l].astype(BF16)
        w1, w3, w2 = w_ffn1[l].astype(BF16), w_ffn3[l].astype(BF16), w_ffn2[l].astype(BF16)
        mod = _ada_mod(c_all, w_ada[l], b_ada[l])

        def group(x, mod_g, tabs, st, sample):
            B, T, _ = x.shape
            sh1, sc1, g1, sh2, sc2, g2 = [mod_g[:, None, i * D:(i + 1) * D] for i in range(6)]
            ydt = F32 if sample else BF16
            z3 = _in_proj(x, sc1, sh1, lw['w_in'], lw['b_in']).reshape(B, T, Z_W)
            c0, n0, m0, buf0, s0 = st
            ya, c1, n1, m1 = _mlstm(z3, c0, n0, m0, ydt)
            yd, s1 = _retention(z3, tabs[1], s0, ydt)
            yc, buf1 = _conv(z3, buf0, conv_w[l], conv_b[l], conv_ln_g[l], conv_ln_b[l], ydt)
            if sample:
                q, rows = _mla_prep_sample(z3, tabs[0].reshape(1, T, 4 * LANE), qn, kvn,
                                           lw['wq1'], lw['wq2'], lw['wuk_t'])
                yb = _sample_attention(l, q, rows, lw['wv_h'], cache_t, page_table, ydt)
            else:
                q, k, v, rows = _mla_prep_prompt(z3, tabs[0], qn, kvn, lw['wq1'], lw['wq2'], lw['wk'], lw['wv'])
                yb = _flash(q, k, v)
            x1 = _merge(ya, yb, yc, yd, z3, x, g1, lw['wb_a'], lw['wb_b'], lw['wb_c'], lw['wb_d'], wo,
                        ln1_g[l], ln1_b[l])
            x2 = _ffn(x1, sc2, sh2, g2, w1, w3, w2, ln2_g[l], ln2_b[l])
            return x2, (rows, c1, n1, m1, buf1, s1)

        st_p = (jnp.zeros((Bp, A_HEADS, A_DK, A_DV), F32), jnp.zeros((Bp, A_HEADS, A_DK), F32),
                jnp.zeros((Bp, A_HEADS), F32), jnp.zeros((Bp, C_WIDTH - 1, C_CH), F32),
                jnp.zeros((Bp, D_HEADS, D_DK, D_DV), F32))
        xp, new_p = group(xp, mod[:Bp], tabs_p, st_p, False)
        st_d = (state_mlstm_C[l].astype(F32), state_mlstm_n[l].astype(F32), state_mlstm_m[l].astype(F32),
                state_conv[l].astype(F32), state_ret[l].astype(F32))
        xd, new_d = group(xd, mod[Bp:], tabs_d, st_d, True)
        for i in range(6):
            p_states[i].append(new_p[i])
            d_states[i].append(new_d[i])
    ps = [jnp.stack(s, axis=0).astype(dt) for s in p_states]
    ds = [jnp.stack(s, axis=0).astype(dt) for s in d_states]
    return (xp.astype(dt), xd.astype(dt), ps[0], ds[0], ps[1], ds[1], ps[2], ds[2], ps[3], ds[3],
            ps[4], ds[4], ps[5], ds[5])
```

```python
import functools

import numpy as np
import jax
import jax.numpy as jnp
from jax import lax
from jax.experimental import pallas as pl
from jax.experimental.pallas import tpu as pltpu

F32 = jnp.float32
BF16 = jnp.bfloat16

D_MODEL = 1024
DEPTH = 2
PAGE_SIZE = 128
N_BRANCH = 4
A_HEADS, A_DK, A_DV = 4, 128, 128
B_HEADS, B_Q_RANK, B_KV_RANK, B_NOPE, B_ROPE, B_VDIM = 8, 384, 256, 64, 32, 64
C_CH, C_WIDTH = 512, 31
D_HEADS, D_DK, D_DV = 4, 128, 128
FFN_HIDDEN = -(-8 * D_MODEL // (3 * 256)) * 256
CHUNK = 128
ROPE_BASE = 10000.0
LN_EPS = 1e-5
RMS_EPS = 1e-6
DEEPNORM_ALPHA = (2 * DEPTH) ** 0.25
CACHE_W = B_KV_RANK + B_ROPE
MLA_SCALE = (B_NOPE + B_ROPE) ** -0.5
LOG2E = float(np.log2(np.e))
O_AQ, O_AI, O_AF = 0, 2048, 2052
O_BQ, O_BKV, O_BKR = 2056, 2440, 2696
O_C, O_D, O_G = 2728, 3752, 5800
N_IN = 9896

Z_G, Z_A, Z_D, Z_C, Z_B = 0, 4096, 6144, 8192, 9216
Z_W = 9984
Z_BW = 768
MISC_KR, MISC_AI, MISC_AF, MISC_KRSW = 0, 32, 36, 64
LANE = 128
HP = 128

VMEM_LIMIT = 56 * 1024 * 1024


def _cp(sem, vmem=VMEM_LIMIT):
    return pltpu.CompilerParams(dimension_semantics=sem, vmem_limit_bytes=vmem)


def _sigmoid(x):
    return 1.0 / (1.0 + jnp.exp(-x))


def _silu(x):
    return x * _sigmoid(x)


def _log_sigmoid(x):
    return jnp.minimum(x, 0.0) - jnp.log(1.0 + jnp.exp(-jnp.abs(x)))


def _ln_rows(x):
    mu = jnp.mean(x, axis=-1, keepdims=True)
    xc = x - mu
    var = jnp.mean(xc * xc, axis=-1, keepdims=True)
    return xc * lax.rsqrt(var + LN_EPS)


def _rms_rows(x):
    return x * lax.rsqrt(jnp.mean(x * x, axis=-1, keepdims=True) + RMS_EPS)


def _dot(a, b):
    return jnp.dot(a, b, preferred_element_type=F32)


def _dot_nt(a, b):
    return lax.dot_general(a, b, (((1,), (1,)), ((), ())), preferred_element_type=F32)


def _bdot(a, b):
    return lax.dot_general(a, b, (((2,), (1,)), ((0,), (0,))), preferred_element_type=F32)


def _bdot_nt(a, b):
    return lax.dot_general(a, b, (((2,), (2,)), ((0,), (0,))), preferred_element_type=F32)


def _bdot_tn(a, b):
    return lax.dot_general(a, b, (((1,), (1,)), ((0,), (0,))), preferred_element_type=F32)


def _tile(n, pref, align=8):
    if n <= pref:
        return n
    for t in range(pref, 0, -1):
        if n % t == 0 and t % align == 0:
            return t
    return n


def _mod_kernel(c_ref, w_ref, b_ref, o_ref):
    s = _silu(c_ref[...])
    o_ref[...] = _dot(s.astype(BF16), w_ref[...].astype(BF16)) + b_ref[...]


def _ada_mod(c, w, b):
    m, d = c.shape
    n = w.shape[1]
    tn = _tile(n, 768, LANE)
    return pl.pallas_call(
        _mod_kernel,
        out_shape=jax.ShapeDtypeStruct((m, n), F32),
        grid=(n // tn,),
        in_specs=[pl.BlockSpec((m, d), lambda j: (0, 0)),
                  pl.BlockSpec((d, tn), lambda j: (0, j)),
                  pl.BlockSpec((1, tn), lambda j: (0, j))],
        out_specs=pl.BlockSpec((m, tn), lambda j: (0, j)),
        compiler_params=_cp(("arbitrary",)),
        name="ada_mod",
    )(c, w, b.reshape(1, n))


def _inproj_kernel(x_ref, sc_ref, sh_ref, w_ref, b_ref, o_ref, u_ref):
    @pl.when(pl.program_id(1) == 0)
    def _():
        u = x_ref[...] * (1.0 + sc_ref[...]) + sh_ref[...]
        u_ref[...] = u.reshape(u_ref.shape).astype(BF16)

    o_ref[...] = _dot(u_ref[...], w_ref[...]) + b_ref[...]


def _token_blocks(B, T, rows):
    if T >= rows:
        return 1, _tile(T, rows)
    return _tile(B, max(1, rows // T), 1), T


def _in_proj(x, sc, sh, w, b):
    B, T, D = x.shape
    n = w.shape[1]
    bb, tt = _token_blocks(B, T, 1024)
    tm = bb * tt
    nt = T // tt
    tn = _tile(n, 1664, LANE)
    grid = ((B // bb) * nt, n // tn)
    return pl.pallas_call(
        _inproj_kernel,
        out_shape=jax.ShapeDtypeStruct((B * T, n), F32),
        grid=grid,
        in_specs=[pl.BlockSpec((bb, tt, D), lambda i, j: (i // nt, i % nt, 0)),
                  pl.BlockSpec((bb, 1, D), lambda i, j: (i // nt, 0, 0)),
                  pl.BlockSpec((bb, 1, D), lambda i, j: (i // nt, 0, 0)),
                  pl.BlockSpec((D, tn), lambda i, j: (0, j)),
                  pl.BlockSpec((1, tn), lambda i, j: (0, j))],
        out_specs=pl.BlockSpec((tm, tn), lambda i, j: (i, j)),
        scratch_shapes=[pltpu.VMEM((tm, D), BF16)],
        compiler_params=_cp(("parallel", "arbitrary")),
        name="in_proj",
    )(x, sc, sh, w, b)


def _stack_heads(ref, n_heads, width, lo=0):
    return jnp.concatenate([ref[:, :, lo + h * width:lo + (h + 1) * width] for h in range(n_heads)], axis=0)


def _mlstm_cells(q, k, v, og, ig_col, af_col, c_prev, n_prev, m_prev, L, mxu):
    row = lax.broadcasted_iota(jnp.int32, (L, L), 0)
    col = lax.broadcasted_iota(jnp.int32, (L, L), 1)
    tril = col <= row
    eye = col == row
    lf_col = _log_sigmoid(af_col)
    ig_row = jnp.sum(jnp.where(eye, ig_col, 0.0), axis=1, keepdims=True)
    b_row = jnp.sum(jnp.where(row <= col, lf_col, 0.0), axis=1, keepdims=True)
    b_col = jnp.sum(jnp.where(eye, b_row, 0.0), axis=2, keepdims=True)
    inter = b_col + m_prev
    intra = jnp.where(tril, b_col - b_row + ig_row, -jnp.inf)
    m_t = jnp.maximum(inter, jnp.max(intra, axis=2, keepdims=True))
    w_inter = jnp.exp(inter - m_t)
    dmat = jnp.exp(intra - m_t)
    qh = q * (A_DK ** -0.5)
    qb, kb, vb = qh.astype(mxu), k.astype(mxu), v.astype(mxu)
    s = _bdot_nt(qb, kb) * dmat
    num = _bdot(s.astype(mxu), vb) + w_inter * _bdot(qb, c_prev.astype(mxu))
    den = jnp.sum(s, axis=2, keepdims=True) + w_inter * jnp.sum(qh * n_prev, axis=2, keepdims=True)
    hv = num / jnp.maximum(jnp.abs(den), jnp.exp(-m_t))
    y = _sigmoid(og) * _ln_rows(hv)
    b_last = b_col[:, L - 1:L, :]
    m_new = m_t[:, L - 1:L, :]
    w_c = jnp.exp(b_last + m_prev - m_new)
    w_s = jnp.exp(b_last - b_col + ig_col - m_new)
    kw = k * w_s
    c_new = w_c * c_prev + _bdot_tn(kw.astype(mxu), vb)
    n_new = w_c * n_prev + jnp.sum(kw, axis=1, keepdims=True)
    return y, c_new, n_new, m_new


def _mlstm_kernel(q_ref, k_ref, v_ref, o_ref, g_ref, c0_ref, n0_ref, m0_ref,
                  y_ref, c_out, n_out, m_out, c_s, n_s, m_s, *, L, mxu, bb, carry):
    H = A_HEADS
    if carry:
        @pl.when(pl.program_id(1) == 0)
        def _():
            c_s[...] = c0_ref[0]
            n_s[...] = n0_ref[0]
            m_s[...] = m0_ref[0]
        c_prev, n_prev, m_prev = c_s[...], n_s[...], m_s[:, :, 0:1]
    else:
        c_prev = jnp.concatenate([c0_ref[:, h] for h in range(H)], axis=0)
        n_prev = jnp.concatenate([n0_ref[:, h] for h in range(H)], axis=0)
        m_prev = jnp.concatenate([m0_ref[:, h, :, 0:1] for h in range(H)], axis=0)

    y, c_new, n_new, m_new = _mlstm_cells(
        _stack_heads(q_ref, H, A_DK), _stack_heads(k_ref, H, A_DK), _stack_heads(v_ref, H, A_DV),
        _stack_heads(o_ref, H, A_DV), _stack_heads(g_ref, H, 1, MISC_AI), _stack_heads(g_ref, H, 1, MISC_AF),
        c_prev, n_prev, m_prev, L, mxu)
    m_new = jnp.broadcast_to(m_new, (H * bb, 1, LANE))
    for h in range(H):
        rows = slice(h * bb, (h + 1) * bb)
        y_ref[:, :, h * A_DV:(h + 1) * A_DV] = y[rows].astype(y_ref.dtype)
        if not carry:
            c_out[:, h] = c_new[rows]
            n_out[:, h] = n_new[rows]
            m_out[:, h] = m_new[rows]
    if carry:
        c_s[...] = c_new
        n_s[...] = n_new
        m_s[...] = m_new

        @pl.when(pl.program_id(1) == pl.num_programs(1) - 1)
        def _():
            c_out[0] = c_new
            n_out[0] = n_new
            m_out[0] = m_new


SMALL_SEQ_BATCH = 8


def _mlstm(z3, c0, n0, m0, ydt):
    B, T, _ = z3.shape
    L = min(CHUNK, T)
    nc = T // L
    carry = nc > 1
    bb = 1 if carry else _tile(B, SMALL_SEQ_BATCH, 1)
    H = A_HEADS
    W = H * A_DK
    ab = Z_A // W
    n0r = n0.reshape(B, H, 1, A_DK)
    m0b = jnp.broadcast_to(m0[..., None, None], (B, H, 1, LANE))
    mxu = BF16 if L >= 16 else F32

    def zspec(k):
        return pl.BlockSpec((bb, L, W), lambda b, c: (b, c, ab + k))

    st_specs = (pl.BlockSpec((bb, H, A_DK, A_DV), lambda b, c: (b, 0, 0, 0)),
                pl.BlockSpec((bb, H, 1, A_DK), lambda b, c: (b, 0, 0, 0)),
                pl.BlockSpec((bb, H, 1, LANE), lambda b, c: (b, 0, 0, 0)))
    y, c1, n1, m1 = pl.pallas_call(
        functools.partial(_mlstm_kernel, L=L, mxu=mxu, bb=bb, carry=carry),
        out_shape=(jax.ShapeDtypeStruct((B, T, W), ydt),
                   jax.ShapeDtypeStruct((B, H, A_DK, A_DV), F32),
                   jax.ShapeDtypeStruct((B, H, 1, A_DK), F32),
                   jax.ShapeDtypeStruct((B, H, 1, LANE), F32)),
        grid=(B // bb, nc),
        in_specs=[zspec(0), zspec(1), zspec(2), zspec(3),
                  pl.BlockSpec((bb, L, LANE), lambda b, c: (b, c, (Z_B + Z_BW - LANE) // LANE))] + list(st_specs),
        out_specs=(pl.BlockSpec((bb, L, W), lambda b, c: (b, c, 0)),) + st_specs,
        scratch_shapes=[pltpu.VMEM((H, A_DK, A_DV), F32), pltpu.VMEM((H, 1, A_DK), F32),
                        pltpu.VMEM((H, 1, LANE), F32)],
        compiler_params=_cp(("parallel", "arbitrary")),
        name="mlstm",
    )(z3, z3, z3, z3, z3, c0, n0r, m0b)
    return y, c1, n1.reshape(B, H, A_DK), m1[:, :, 0, 0]


_LOG_GAMMA = tuple(float(np.log(1.0 - 2.0 ** (-5.0 - h))) for h in range(D_HEADS))


def _ret_kernel(q_ref, k_ref, v_ref, g_ref, tab_ref, s0_ref, y_ref, s_out, s_s, *, L, mxu, bb, carry):
    if carry:
        @pl.when(pl.program_id(1) == 0)
        def _():
            s_s[...] = s0_ref[0]

    H = D_HEADS
    X = H * bb
    cos = tab_ref[:, 0:D_DK]
    sin = tab_ref[:, D_DK:2 * D_DK]
    row = lax.broadcasted_iota(jnp.int32, (L, L), 0)
    col = lax.broadcasted_iota(jnp.int32, (L, L), 1)
    diff = (row - col).astype(F32)
    jcol = lax.broadcasted_iota(jnp.int32, (L, 1), 0).astype(F32)

    def per_head(fn, shape):
        return jnp.concatenate([jnp.broadcast_to(fn(_LOG_GAMMA[h])[None], (bb,) + shape) for h in range(H)], axis=0)

    decay = per_head(lambda lg: jnp.where(diff >= 0.0, jnp.exp(jnp.maximum(diff, 0.0) * lg), 0.0), (L, L))
    w_in = per_head(lambda lg: jnp.exp((jcol + 1.0) * lg), (L, 1))
    w_st = per_head(lambda lg: jnp.exp((L - 1.0 - jcol) * lg), (L, 1))
    w_S = per_head(lambda lg: jnp.full((1, 1), float(np.exp(L * lg)), F32), (1, 1))

    def rotary(x):
        swapped = pltpu.roll(x.reshape(X * L, D_DK), D_DK // 2, 1).reshape(X, L, D_DK)
        return x * cos + swapped * sin

    qh = rotary(_stack_heads(q_ref, H, D_DK))
    kh = rotary(_stack_heads(k_ref, H, D_DK)) * (D_DK ** -0.5)
    vb = _stack_heads(v_ref, H, D_DV).astype(mxu)
    qb = qh.astype(mxu)
    s = _bdot_nt(qb, kh.astype(mxu)) * decay
    s_prev = s_s[...] if carry else jnp.concatenate([s0_ref[:, h] for h in range(H)], axis=0)
    o = _bdot(s.astype(mxu), vb) + w_in * _bdot(qb, s_prev.astype(mxu))
    y = _silu(_stack_heads(g_ref, H, D_DV)) * _ln_rows(o)
    s_new = w_S * s_prev + _bdot_tn((kh * w_st).astype(mxu), vb)
    for h in range(H):
        rows = slice(h * bb, (h + 1) * bb)
        y_ref[:, :, h * D_DV:(h + 1) * D_DV] = y[rows].astype(y_ref.dtype)
        if not carry:
            s_out[:, h] = s_new[rows]
    if carry:
        s_s[...] = s_new

        @pl.when(pl.program_id(1) == pl.num_programs(1) - 1)
        def _():
            s_out[0] = s_new


def _retention(z3, tab, s0, ydt):
    B, T, _ = z3.shape
    L = min(CHUNK, T)
    nc = T // L
    carry = nc > 1
    bb = 1 if carry else _tile(B, SMALL_SEQ_BATCH, 1)
    H = D_HEADS
    W = H * D_DK
    db = Z_D // W
    mxu = BF16 if L >= 16 else F32

    def zspec(k):
        return pl.BlockSpec((bb, L, W), lambda b, c: (b, c, db + k))

    st_spec = pl.BlockSpec((bb, H, D_DK, D_DV), lambda b, c: (b, 0, 0, 0))
    return pl.pallas_call(
        functools.partial(_ret_kernel, L=L, mxu=mxu, bb=bb, carry=carry),
        out_shape=(jax.ShapeDtypeStruct((B, T, W), ydt),
                   jax.ShapeDtypeStruct((B, H, D_DK, D_DV), F32)),
        grid=(B // bb, nc),
        in_specs=[zspec(0), zspec(1), zspec(2), zspec(3),
                  pl.BlockSpec((L, 2 * D_DK), lambda b, c: (c, 0)), st_spec],
        out_specs=(pl.BlockSpec((bb, L, W), lambda b, c: (b, c, 0)), st_spec),
        scratch_shapes=[pltpu.VMEM((H, D_DK, D_DV), F32)],
        compiler_params=_cp(("parallel", "arbitrary")),
        name="retention",
    )(z3, z3, z3, z3, tab, s0)


_CPAD = 32


_SUB = 8
_CROWS = 64


def _conv_kernel(glu_ref, buf_ref, w_ref, b_ref, g_ref, be_ref, y_ref, buf_out, ext, *shifted, tt):
    ti = pl.program_id(1)
    lead = _CPAD - (C_WIDTH - 1)
    bb = ext.shape[0]

    @pl.when(ti == 0)
    def _():
        ext[:, 0:lead, :] = jnp.zeros((bb, lead, C_CH), F32)
        ext[:, lead:_CPAD, :] = buf_ref[...]
        if shifted:
            ext[:, _CPAD + tt:, :] = jnp.zeros((bb, ext.shape[1] - _CPAD - tt, C_CH), F32)

    glu = glu_ref[...]
    ext[:, _CPAD:_CPAD + tt, :] = glu[:, :, :C_CH] * _sigmoid(glu[:, :, C_CH:])

    def finish(acc):
        return _silu(_ln_rows(acc + b_ref[...]) * g_ref[...] + be_ref[...]).astype(y_ref.dtype)

    if shifted:
        sh = shifted[0]
        for r in range(1, _SUB):
            sh[r - 1] = ext[:, r:r + tt + _CPAD, :]
        for c0 in range(0, tt, _CROWS):
            acc = jnp.zeros((bb, _CROWS, C_CH), F32)
            for j in range(C_WIDTH):
                r = (lead + j) % _SUB
                a = lead + j - r + c0
                tap = ext[:, a:a + _CROWS, :] if r == 0 else sh[r - 1, :, a:a + _CROWS, :]
                acc = acc + tap * w_ref[j:j + 1, :]
            y_ref[:, c0:c0 + _CROWS, :] = finish(acc)
    else:
        acc = jnp.zeros((bb, tt, C_CH), F32)
        for j in range(C_WIDTH):
            acc = acc + ext[:, lead + j:lead + j + tt, :] * w_ref[j:j + 1, :]
        y_ref[...] = finish(acc)

    @pl.when(ti == pl.num_programs(1) - 1)
    def _():
        buf_out[...] = ext[:, tt + lead:tt + _CPAD, :]

    ext[:, 0:_CPAD, :] = ext[:, tt:tt + _CPAD, :]


def _conv(z3, buf0, w, b, g, be, ydt):
    B, T, _ = z3.shape
    bb, tt = _token_blocks(B, T, 256)
    nt = T // tt
    use_shifted = tt % _CROWS == 0
    ext_rows = tt + _CPAD + (_SUB if use_shifted else 0)
    scratch = [pltpu.VMEM((bb, ext_rows, C_CH), F32)]
    if use_shifted:
        scratch.append(pltpu.VMEM((_SUB - 1, bb, tt + _CPAD, C_CH), F32))
    vec = lambda a: a.reshape(1, C_CH)
    return pl.pallas_call(
        functools.partial(_conv_kernel, tt=tt),
        out_shape=(jax.ShapeDtypeStruct((B, T, C_CH), ydt),
                   jax.ShapeDtypeStruct((B, C_WIDTH - 1, C_CH), F32)),
        grid=(B // bb, nt),
        in_specs=[pl.BlockSpec((bb, tt, 2 * C_CH), lambda b, t: (b, t, Z_C // (2 * C_CH))),
                  pl.BlockSpec((bb, C_WIDTH - 1, C_CH), lambda b, t: (b, 0, 0)),
                  pl.BlockSpec((C_WIDTH, C_CH), lambda b, t: (0, 0)),
                  pl.BlockSpec((1, C_CH), lambda b, t: (0, 0)),
                  pl.BlockSpec((1, C_CH), lambda b, t: (0, 0)),
                  pl.BlockSpec((1, C_CH), lambda b, t: (0, 0))],
        out_specs=(pl.BlockSpec((bb, tt, C_CH), lambda b, t: (b, t, 0)),
                   pl.BlockSpec((bb, C_WIDTH - 1, C_CH), lambda b, t: (b, 0, 0))),
        scratch_shapes=scratch,
        compiler_params=_cp(("parallel", "arbitrary")),
        name="conv_module",
    )(z3, buf0, w, vec(b), vec(g), vec(be))


def _mla_common(zb_ref, tab, qn_ref, kvn_ref, wq1_ref, wq2_ref, rows_ref, q_scale):
    zb = zb_ref[...].reshape(-1, Z_BW)
    bq = zb[:, 0:B_Q_RANK]
    bkv = zb[:, B_Q_RANK:B_Q_RANK + B_KV_RANK]
    misc = zb[:, B_Q_RANK + B_KV_RANK:]
    qn = (_rms_rows(bq) * qn_ref[...]).astype(BF16)
    ckv = _rms_rows(bkv) * kvn_ref[...]
    cos_q = jnp.concatenate([tab[:, 0:LANE]] * B_HEADS, axis=1)
    sin_q = jnp.concatenate([tab[:, LANE:2 * LANE]] * B_HEADS, axis=1)
    q = (_dot(qn, wq1_ref[...]) * cos_q + _dot(qn, wq2_ref[...]) * sin_q) * q_scale
    kpe = misc * tab[:, 2 * LANE:3 * LANE] + pltpu.roll(misc, LANE // 2, 1) * tab[:, 3 * LANE:]
    lead = rows_ref.shape[:-1]
    rows_ref[:, :, 0:B_KV_RANK] = ckv.reshape(lead + (B_KV_RANK,))
    rows_ref[:, :, B_KV_RANK:CACHE_W] = kpe[:, 0:B_ROPE].reshape(lead + (B_ROPE,))
    return q, ckv, kpe


def _mla_prep_prompt_kernel(zb_ref, tab_ref, qn_ref, kvn_ref, wq1_ref, wq2_ref, wk_ref, wv_ref,
                            q_ref, k_ref, v_ref, rows_ref):
    q, ckv, kpe = _mla_common(zb_ref, tab_ref[...], qn_ref, kvn_ref, wq1_ref, wq2_ref, rows_ref,
                              MLA_SCALE * LOG2E)
    q_ref[0] = q.astype(BF16)
    cb = ckv.astype(BF16)
    kpe_hi = pltpu.roll(kpe, LANE // 2, 1)
    k = _dot(cb, wk_ref[...]) + jnp.concatenate([kpe_hi] * B_HEADS, axis=1)
    k_ref[0] = k.astype(BF16)
    lane = lax.broadcasted_iota(jnp.int32, (1, B_HEADS * HP), 1)
    ones_col = jnp.where(lane % HP == B_VDIM, 1.0, 0.0)
    v_ref[0] = (_dot(cb, wv_ref[...]) + ones_col).astype(BF16)


def _mla_prep_prompt(z3, tab, qn, kvn, wq1, wq2, wk, wv):
    B, T, _ = z3.shape
    tt = _tile(T, 512)
    nt = T // tt
    HW = B_HEADS * HP
    full = lambda a: pl.BlockSpec(a.shape, lambda b, t: (0,) * a.ndim)
    big = pl.BlockSpec((1, tt, HW), lambda b, t: (b, t, 0))
    return pl.pallas_call(
        _mla_prep_prompt_kernel,
        out_shape=(jax.ShapeDtypeStruct((B, T, HW), BF16),) * 3
        + (jax.ShapeDtypeStruct((B, T, CACHE_W), F32),),
        grid=(B, nt),
        in_specs=[pl.BlockSpec((1, tt, Z_BW), lambda b, t: (b, t, Z_B // Z_BW)),
                  pl.BlockSpec((tt, 4 * LANE), lambda b, t: (t, 0)),
                  full(qn), full(kvn), full(wq1), full(wq2), full(wk), full(wv)],
        out_specs=(big, big, big, pl.BlockSpec((1, tt, CACHE_W), lambda b, t: (b, t, 0))),
        compiler_params=_cp(("parallel", "parallel")),
        name="mla_prep_prompt",
    )(z3, tab, qn, kvn, wq1, wq2, wk, wv)


def _flash_kernel(q_ref, k_ref, v_ref, o_ref, *, tq, nq):
    row = lax.broadcasted_iota(jnp.int32, (tq, tq), 0)
    col = lax.broadcasted_iota(jnp.int32, (tq, tq), 1)
    causal = col <= row
    for qi in range(nq):
        q = q_ref[0, qi * tq:(qi + 1) * tq, :]
        m = acc = None
        for kj in range(qi + 1):
            k = k_ref[0, kj * tq:(kj + 1) * tq, :]
            v = v_ref[0, kj * tq:(kj + 1) * tq, :]
            s = _dot_nt(q, k)
            if kj == qi:
                s = jnp.where(causal, s, -jnp.inf)
            s_max = jnp.max(s, axis=1, keepdims=True)
            if kj == 0:
                m = s_max
                acc = _dot(jnp.exp2(s - m).astype(BF16), v)
            else:
                m_new = jnp.maximum(m, s_max)
                acc = jnp.exp2(m - m_new) * acc + _dot(jnp.exp2(s - m_new).astype(BF16), v)
                m = m_new
        o_ref[0, qi * tq:(qi + 1) * tq, :] = (acc / acc[:, B_VDIM:B_VDIM + 1]).astype(o_ref.dtype)


def _flash(q, k, v):
    B, T, HW = q.shape
    tq = _tile(T, 512)
    spec = pl.BlockSpec((1, T, HP), lambda b, h: (b, 0, h))
    return pl.pallas_call(
        functools.partial(_flash_kernel, tq=tq, nq=T // tq),
        out_shape=jax.ShapeDtypeStruct((B, T, HW), BF16),
        grid=(B, B_HEADS),
        in_specs=[spec, spec, spec],
        out_specs=spec,
        compiler_params=_cp(("parallel", "parallel")),
        name="mla_prompt_attention",
    )(q, k, v)


def _mla_prep_sample_kernel(zb_ref, tab_ref, qn_ref, kvn_ref, wq1_ref, wq2_ref, wuk_ref,
                            q_ref, rows_ref):
    bb, _, tt, _ = q_ref.shape
    tab = jnp.broadcast_to(tab_ref[...], (bb, tt, 4 * LANE)).reshape(bb * tt, 4 * LANE)
    q, _, _ = _mla_common(zb_ref, tab, qn_ref, kvn_ref, wq1_ref, wq2_ref, rows_ref, MLA_SCALE)
    lane = lax.broadcasted_iota(jnp.int32, (1, HP), 1)
    for h in range(B_HEADS):
        qh = q[:, h * HP:(h + 1) * HP]
        q_abs = _dot(qh.astype(BF16), wuk_ref[h])
        q_pe = jnp.where(lane < B_ROPE, pltpu.roll(qh, LANE // 2, 1), 0.0)
        q_ref[:, h, :, 0:B_KV_RANK] = q_abs.reshape(bb, tt, B_KV_RANK)
        q_ref[:, h, :, B_KV_RANK:] = q_pe[:, 0:B_ROPE].reshape(bb, tt, B_ROPE)


def _mla_prep_sample(z3, tab, qn, kvn, wq1, wq2, wuk):
    B, T, _ = z3.shape
    bb = _tile(B, max(1, 512 // T), 1)
    full = lambda a: pl.BlockSpec(a.shape, lambda i: (0,) * a.ndim)
    return pl.pallas_call(
        _mla_prep_sample_kernel,
        out_shape=(jax.ShapeDtypeStruct((B, B_HEADS, T, CACHE_W), F32),
                   jax.ShapeDtypeStruct((B, T, CACHE_W), F32)),
        grid=(B // bb,),
        in_specs=[pl.BlockSpec((bb, T, Z_BW), lambda i: (i, 0, Z_B // Z_BW)),
                  pl.BlockSpec((1, T, 4 * LANE), lambda i: (0, 0, 0)),
                  full(qn), full(kvn), full(wq1), full(wq2), full(wuk)],
        out_specs=(pl.BlockSpec((bb, B_HEADS, T, CACHE_W), lambda i: (i, 0, 0, 0)),
                   pl.BlockSpec((bb, T, CACHE_W), lambda i: (i, 0, 0))),
        compiler_params=_cp(("parallel",)),
        name="mla_prep_sample",
    )(z3, tab, qn, kvn, wq1, wq2, wuk)


def _sattn_kernel(pt_ref, q_ref, new_ref, wv_ref, cache_ref, y_ref, kbuf, sem, *, layer, n_pages, T):
    b = pl.program_id(0)
    R = B_HEADS * T

    def page_copy(row, g, slot):
        return pltpu.make_async_copy(cache_ref.at[layer, pt_ref[row, g]], kbuf.at[slot, g], sem.at[slot])

    def fetch(row, slot):
        for g in range(n_pages):
            page_copy(row, g, slot).start()

    @pl.when(b == 0)
    def _():
        fetch(0, 0)

    @pl.when(b + 1 < pl.num_programs(0))
    def _():
        fetch(b + 1, (b + 1) % 2)

    slot = b % 2
    for g in range(n_pages):
        page_copy(b, g, slot).wait()

    q = q_ref[0].reshape(R, CACHE_W).astype(BF16)
    kt = jnp.concatenate([kbuf[slot, g].astype(BF16) for g in range(n_pages)], axis=1)
    pad = jnp.zeros((PAGE_SIZE - T, CACHE_W), F32)
    new = jnp.concatenate([new_ref[0], pad], axis=0).astype(BF16)
    r = lax.broadcasted_iota(jnp.int32, (R, PAGE_SIZE), 0)
    c = lax.broadcasted_iota(jnp.int32, (R, PAGE_SIZE), 1)
    s_past = _dot(q, kt)
    s_new = jnp.where(c <= r % T, _dot_nt(q, new), -jnp.inf)
    m = jnp.maximum(jnp.max(s_past, axis=1, keepdims=True), jnp.max(s_new, axis=1, keepdims=True))
    p_past = jnp.exp(s_past - m)
    p_new = jnp.exp(s_new - m)
    l = jnp.sum(p_past, axis=1, keepdims=True) + jnp.sum(p_new, axis=1, keepdims=True)
    acc = _dot_nt(p_past.astype(BF16), kt[0:B_KV_RANK, :]) + _dot(p_new.astype(BF16), new[:, 0:B_KV_RANK])
    o = acc / l
    for h in range(B_HEADS):
        oh = o[h * T:(h + 1) * T, :].astype(BF16)
        y_ref[0, :, h * HP:(h + 1) * HP] = _dot(oh, wv_ref[h]).astype(y_ref.dtype)


def _sample_attention(layer, q, new_rows, wv, cache_t, page_table, ydt):
    B, H, T, _ = q.shape
    n_pages = page_table.shape[1]
    grid_spec = pltpu.PrefetchScalarGridSpec(
        num_scalar_prefetch=1,
        grid=(B,),
        in_specs=[pl.BlockSpec((1, H, T, CACHE_W), lambda b, pt: (b, 0, 0, 0)),
                  pl.BlockSpec((1, T, CACHE_W), lambda b, pt: (b, 0, 0)),
                  pl.BlockSpec(wv.shape, lambda b, pt: (0, 0, 0)),
                  pl.BlockSpec(memory_space=pl.ANY)],
        out_specs=pl.BlockSpec((1, T, H * HP), lambda b, pt: (b, 0, 0)),
        scratch_shapes=[pltpu.VMEM((2, n_pages, CACHE_W, PAGE_SIZE), F32),
                        pltpu.SemaphoreType.DMA((2,))],
    )
    return pl.pallas_call(
        functools.partial(_sattn_kernel, layer=layer, n_pages=n_pages, T=T),
        out_shape=jax.ShapeDtypeStruct((B, T, H * HP), ydt),
        grid_spec=grid_spec,
        compiler_params=_cp(("arbitrary",)),
        name="mla_sample_attention",
    )(page_table, q, new_rows, wv, cache_t)


def _merge_kernel(ya_ref, yb_ref, yc_ref, yd_ref, gt_ref, x_ref, g1_ref, wa_ref, wb_ref, wc_ref, wd_ref,
                  wo_ref, lg_ref, lb_ref, o_ref):
    rows = o_ref.shape[0] * o_ref.shape[1]

    def flat(ref):
        return ref[...].reshape(rows, ref.shape[-1])

    gates = flat(gt_ref)
    acc = None
    for n, (y_ref, w_ref) in enumerate(((ya_ref, wa_ref), (yb_ref, wb_ref), (yc_ref, wc_ref), (yd_ref, wd_ref))):
        term = _sigmoid(gates[:, n * D_MODEL:(n + 1) * D_MODEL]) * _dot(flat(y_ref).astype(BF16), w_ref[...])
        acc = term if acc is None else acc + term
    mix = _dot(acc.astype(BF16), wo_ref[...]).reshape(o_ref.shape)
    v = DEEPNORM_ALPHA * x_ref[...] + g1_ref[...] * mix
    o_ref[...] = _ln_rows(v) * lg_ref[...] + lb_ref[...]


def _merge(ya, yb, yc, yd, z3, x, g1, wa, wb, wc, wd, wo, lg, lb):
    B, T, D = x.shape
    bb, tt = _token_blocks(B, T, 256)
    nt = T // tt
    tok = lambda w: pl.BlockSpec((bb, tt, w), lambda i: (i // nt, i % nt, 0))
    full = lambda a: pl.BlockSpec(a.shape, lambda i: (0,) * a.ndim)
    vec = lambda a: a.reshape(1, 1, D)
    return pl.pallas_call(
        _merge_kernel,
        out_shape=jax.ShapeDtypeStruct((B, T, D), F32),
        grid=((B // bb) * nt,),
        in_specs=[tok(ya.shape[-1]), tok(yb.shape[-1]), tok(yc.shape[-1]), tok(yd.shape[-1]),
                  pl.BlockSpec((bb, tt, N_BRANCH * D), lambda i: (i // nt, i % nt, Z_G // (N_BRANCH * D_MODEL))),
                  tok(D),
                  pl.BlockSpec((bb, 1, D), lambda i: (i // nt, 0, 0)),
                  full(wa), full(wb), full(wc), full(wd), full(wo),
                  pl.BlockSpec((1, 1, D), lambda i: (0, 0, 0)), pl.BlockSpec((1, 1, D), lambda i: (0, 0, 0))],
        out_specs=tok(D),
        compiler_params=_cp(("parallel",)),
        name="merge_out_ln",
    )(ya, yb, yc, yd, z3, x, g1, wa, wb, wc, wd, wo, vec(lg), vec(lb))


def _ffn_kernel(x_ref, sc_ref, sh_ref, g2_ref, w1_ref, w3_ref, w2_ref, lg_ref, lb_ref, o_ref, u_s, acc_s):
    j = pl.program_id(1)

    @pl.when(j == 0)
    def _():
        u = x_ref[...] * (1.0 + sc_ref[...]) + sh_ref[...]
        u_s[...] = u.reshape(u_s.shape).astype(BF16)
        acc_s[...] = jnp.zeros(acc_s.shape, F32)

    u = u_s[...]
    hid = _silu(_dot(u, w1_ref[...])) * _dot(u, w3_ref[...])
    acc_s[...] += _dot(hid.astype(BF16), w2_ref[...])

    @pl.when(j == pl.num_programs(1) - 1)
    def _():
        v = DEEPNORM_ALPHA * x_ref[...] + g2_ref[...] * acc_s[...].reshape(o_ref.shape)
        o_ref[...] = _ln_rows(v) * lg_ref[...] + lb_ref[...]


def _ffn(x, sc, sh, g2, w1, w3, w2, lg, lb):
    B, T, D = x.shape
    Hd = w1.shape[1]
    bb, tt = _token_blocks(B, T, 512)
    nt = T // tt
    tm = bb * tt
    th = _tile(Hd, 1408, LANE)
    tok = pl.BlockSpec((bb, tt, D), lambda i, j: (i // nt, i % nt, 0))
    per_b = pl.BlockSpec((bb, 1, D), lambda i, j: (i // nt, 0, 0))
    vspec = pl.BlockSpec((1, 1, D), lambda i, j: (0, 0, 0))
    vec = lambda a: a.reshape(1, 1, D)
    return pl.pallas_call(
        _ffn_kernel,
        out_shape=jax.ShapeDtypeStruct((B, T, D), F32),
        grid=((B // bb) * nt, Hd // th),
        in_specs=[tok, per_b, per_b, per_b,
                  pl.BlockSpec((D, th), lambda i, j: (0, j)),
                  pl.BlockSpec((D, th), lambda i, j: (0, j)),
                  pl.BlockSpec((th, D), lambda i, j: (j, 0)),
                  vspec, vspec],
        out_specs=tok,
        scratch_shapes=[pltpu.VMEM((tm, D), BF16), pltpu.VMEM((tm, D), F32)],
        compiler_params=_cp(("parallel", "arbitrary")),
        name="ffn_ln",
    )(x, sc, sh, g2, w1, w3, w2, vec(lg), vec(lb))


def _rope_tables(pos):
    posf = pos.astype(F32)[:, None]
    T = pos.shape[0]

    def cs(d):
        inv = ROPE_BASE ** (-jnp.arange(0, d, 2, dtype=F32) / d)
        ang = posf * inv[None, :]
        c, s = jnp.cos(ang), jnp.sin(ang)
        return jnp.concatenate([c, c], axis=1), jnp.concatenate([-s, s], axis=1)

    c32, s32 = cs(B_ROPE)
    z = lambda w: jnp.zeros((T, w), F32)
    cos_q = jnp.concatenate([jnp.ones((T, B_NOPE), F32), c32, z(HP - B_NOPE - B_ROPE)], axis=1)
    sin_q = jnp.concatenate([z(B_NOPE), s32, z(HP - B_NOPE - B_ROPE)], axis=1)
    cos_k = jnp.concatenate([c32, z(LANE - B_ROPE)], axis=1)
    sin_k = jnp.concatenate([s32, z(LANE - B_ROPE)], axis=1)
    tab_b = jnp.concatenate([cos_q, sin_q, cos_k, sin_k], axis=1)
    c128, s128 = cs(D_DK)
    return tab_b, jnp.concatenate([c128, s128], axis=1)


def _swap_halves(a, lo, width):
    half = width // 2
    return jnp.concatenate([a[..., lo + half:lo + width], a[..., lo:lo + half]], axis=-1)


def _layer_weights(l, w_in, b_in, mla_w_uq, mla_w_uk, mla_w_uv, w_branch):
    def reorder(a):
        zeros = lambda w: jnp.zeros(a.shape[:-1] + (w,), a.dtype)
        return jnp.concatenate([
            a[..., O_G:N_IN], a[..., O_AQ:O_AI], a[..., O_D:O_G], a[..., O_C:O_D],
            a[..., O_BQ:O_BKR + B_ROPE], a[..., O_AI:O_BQ], zeros(MISC_KRSW - MISC_AF - A_HEADS),
            _swap_halves(a, O_BKR, B_ROPE), zeros(LANE - MISC_KRSW - B_ROPE)], axis=-1)

    w_in_r = reorder(w_in[l]).astype(BF16)
    b_in_r = reorder(b_in[l]).reshape(1, Z_W)

    hd = B_NOPE + B_ROPE
    wq = mla_w_uq[l].reshape(B_Q_RANK, B_HEADS, hd)
    zq = jnp.zeros((B_Q_RANK, B_HEADS, HP - hd), F32)
    wq1 = jnp.concatenate([wq, zq], axis=-1).reshape(B_Q_RANK, B_HEADS * HP).astype(BF16)
    wq2 = jnp.concatenate([jnp.zeros((B_Q_RANK, B_HEADS, B_NOPE), F32), _swap_halves(wq, B_NOPE, B_ROPE), zq],
                          axis=-1).reshape(B_Q_RANK, B_HEADS * HP).astype(BF16)
    wuk = mla_w_uk[l]
    wuv = mla_w_uv[l]
    wk = jnp.concatenate([wuk, jnp.zeros((B_KV_RANK, B_HEADS, HP - B_NOPE), F32)], axis=-1)
    wv = jnp.concatenate([wuv, jnp.zeros((B_KV_RANK, B_HEADS, HP - B_VDIM), F32)], axis=-1)
    wk_flat = wk.reshape(B_KV_RANK, B_HEADS * HP).astype(BF16)
    wv_flat = wv.reshape(B_KV_RANK, B_HEADS * HP).astype(BF16)
    wuk_t = jnp.transpose(wk, (1, 2, 0)).astype(BF16)
    wv_h = jnp.transpose(wv, (1, 0, 2)).astype(BF16)
    wb = w_branch[l]
    wb_b = jnp.concatenate([wb[1].reshape(B_HEADS, B_VDIM, D_MODEL),
                            jnp.zeros((B_HEADS, HP - B_VDIM, D_MODEL), F32)], axis=1)
    wb_b = wb_b.reshape(B_HEADS * HP, D_MODEL).astype(BF16)
    return dict(w_in=w_in_r, b_in=b_in_r, wq1=wq1, wq2=wq2, wk=wk_flat, wv=wv_flat, wuk_t=wuk_t, wv_h=wv_h,
                wb_a=wb[0].astype(BF16), wb_b=wb_b, wb_c=wb[2].astype(BF16), wb_d=wb[3].astype(BF16))


def kernel(x_prompt, x_sample, cache_mla, state_mlstm_C, state_mlstm_n, state_mlstm_m, state_conv, state_ret,
           page_table, c_prompt, c_sample, w_ada, b_ada, w_in, b_in, mla_q_norm, mla_w_uq, mla_kv_norm,
           mla_w_uk, mla_w_uv, conv_w, conv_b, conv_ln_g, conv_ln_b, w_branch, w_out, ln1_g, ln1_b,
           w_ffn1, w_ffn3, w_ffn2, ln2_g, ln2_b):
    dt = x_prompt.dtype
    Bp, Tp, D = x_prompt.shape
    Bd, Td, _ = x_sample.shape
    past_len = page_table.shape[1] * PAGE_SIZE
    cache_t = jnp.swapaxes(cache_mla, 2, 3)
    tabs_p = _rope_tables(jnp.arange(Tp))
    tabs_d = _rope_tables(past_len + jnp.arange(Td))
    c_all = jnp.concatenate([c_prompt, c_sample], axis=0).astype(F32)

    xp, xd = x_prompt.astype(F32), x_sample.astype(F32)
    p_states = [[] for _ in range(6)]
    d_states = [[] for _ in range(6)]
    for l in range(DEPTH):
        lw = _layer_weights(l, w_in, b_in, mla_w_uq, mla_w_uk, mla_w_uv, w_branch)
        qn = mla_q_norm[l].reshape(1, B_Q_RANK)
        kvn = mla_kv_norm[l].reshape(1, B_KV_RANK)
        wo = w_out[l].astype(BF16)
        w1, w3, w2 = w_ffn1[l].astype(BF16), w_ffn3[l].astype(BF16), w_ffn2[l].astype(BF16)
        mod = _ada_mod(c_all, w_ada[l], b_ada[l])

        def group(x, mod_g, tabs, st, sample):
            B, T, _ = x.shape
            sh1, sc1, g1, sh2, sc2, g2 = [mod_g[:, None, i * D:(i + 1) * D] for i in range(6)]
            ydt = F32 if sample else BF16
            z3 = _in_proj(x, sc1, sh1, lw['w_in'], lw['b_in']).reshape(B, T, Z_W)
            c0, n0, m0, buf0, s0 = st
            ya, c1, n1, m1 = _mlstm(z3, c0, n0, m0, ydt)
            yd, s1 = _retention(z3, tabs[1], s0, ydt)
            yc, buf1 = _conv(z3, buf0, conv_w[l], conv_b[l], conv_ln_g[l], conv_ln_b[l], ydt)
            if sample:
                q, rows = _mla_prep_sample(z3, tabs[0].reshape(1, T, 4 * LANE), qn, kvn,
                                           lw['wq1'], lw['wq2'], lw['wuk_t'])
                yb = _sample_attention(l, q, rows, lw['wv_h'], cache_t, page_table, ydt)
            else:
                q, k, v, rows = _mla_prep_prompt(z3, tabs[0], qn, kvn, lw['wq1'], lw['wq2'], lw['wk'], lw['wv'])
                yb = _flash(q, k, v)
            x1 = _merge(ya, yb, yc, yd, z3, x, g1, lw['wb_a'], lw['wb_b'], lw['wb_c'], lw['wb_d'], wo,
                        ln1_g[l], ln1_b[l])
            x2 = _ffn(x1, sc2, sh2, g2, w1, w3, w2, ln2_g[l], ln2_b[l])
            return x2, (rows, c1, n1, m1, buf1, s1)

        st_p = (jnp.zeros((Bp, A_HEADS, A_DK, A_DV), F32), jnp.zeros((Bp, A_HEADS, A_DK), F32),
                jnp.zeros((Bp, A_HEADS), F32), jnp.zeros((Bp, C_WIDTH - 1, C_CH), F32),
                jnp.zeros((Bp, D_HEADS, D_DK, D_DV), F32))
        xp, new_p = group(xp, mod[:Bp], tabs_p, st_p, False)
        st_d = (state_mlstm_C[l].astype(F32), state_mlstm_n[l].astype(F32), state_mlstm_m[l].astype(F32),
                state_conv[l].astype(F32), state_ret[l].astype(F32))
        xd, new_d = group(xd, mod[Bp:], tabs_d, st_d, True)
        for i in range(6):
            p_states[i].append(new_p[i])
            d_states[i].append(new_d[i])
    ps = [jnp.stack(s, axis=0).astype(dt) for s in p_states]
    ds = [jnp.stack(s, axis=0).astype(dt) for s in d_states]
    return (xp.astype(dt), xd.astype(dt), ps[0], ds[0], ps[1], ds[1], ps[2], ds[2], ps[3], ds[3],
            ps[4], ds[4], ps[5], ds[5])
```

```python
import functools

import numpy as np
import jax
import jax.numpy as jnp
from jax import lax
from jax.experimental import pallas as pl
from jax.experimental.pallas import tpu as pltpu

F32 = jnp.float32
BF16 = jnp.bfloat16

D_MODEL = 1024
DEPTH = 2
PAGE_SIZE = 128
N_BRANCH = 4
A_HEADS, A_DK, A_DV = 4, 128, 128
B_HEADS, B_Q_RANK, B_KV_RANK, B_NOPE, B_ROPE, B_VDIM = 8, 384, 256, 64, 32, 64
C_CH, C_WIDTH = 512, 31
D_HEADS, D_DK, D_DV = 4, 128, 128
FFN_HIDDEN = -(-8 * D_MODEL // (3 * 256)) * 256
CHUNK = 128
ROPE_BASE = 10000.0
LN_EPS = 1e-5
RMS_EPS = 1e-6
DEEPNORM_ALPHA = (2 * DEPTH) ** 0.25
CACHE_W = B_KV_RANK + B_ROPE
MLA_SCALE = (B_NOPE + B_ROPE) ** -0.5
LOG2E = float(np.log2(np.e))
GATHER_SLOTS = 3
O_AQ, O_AI, O_AF = 0, 2048, 2052
O_BQ, O_BKV, O_BKR = 2056, 2440, 2696
O_C, O_D, O_G = 2728, 3752, 5800
N_IN = 9896

Z_G, Z_A, Z_D, Z_C, Z_B = 0, 4096, 6144, 8192, 9216
Z_W = 9984
Z_BW = 768
MISC_KR, MISC_AI, MISC_AF, MISC_KRSW = 0, 32, 36, 64
LANE = 128
HP = 128

VMEM_LIMIT = 56 * 1024 * 1024


def _cp(sem, vmem=VMEM_LIMIT):
    return pltpu.CompilerParams(dimension_semantics=sem, vmem_limit_bytes=vmem)


def _sigmoid(x):
    return 1.0 / (1.0 + jnp.exp(-x))


def _silu(x):
    return x * _sigmoid(x)


def _log_sigmoid(x):
    return jnp.minimum(x, 0.0) - jnp.log(1.0 + jnp.exp(-jnp.abs(x)))


def _ln_rows(x):
    mu = jnp.mean(x, axis=-1, keepdims=True)
    xc = x - mu
    var = jnp.mean(xc * xc, axis=-1, keepdims=True)
    return xc * lax.rsqrt(var + LN_EPS)


def _rms_rows(x):
    return x * lax.rsqrt(jnp.mean(x * x, axis=-1, keepdims=True) + RMS_EPS)


def _dot(a, b):
    return jnp.dot(a, b, preferred_element_type=F32)


def _dot_nt(a, b):
    return lax.dot_general(a, b, (((1,), (1,)), ((), ())), preferred_element_type=F32)


def _bdot(a, b):
    return lax.dot_general(a, b, (((2,), (1,)), ((0,), (0,))), preferred_element_type=F32)


def _bdot_nt(a, b):
    return lax.dot_general(a, b, (((2,), (2,)), ((0,), (0,))), preferred_element_type=F32)


def _bdot_tn(a, b):
    return lax.dot_general(a, b, (((1,), (1,)), ((0,), (0,))), preferred_element_type=F32)


def _tile(n, pref, align=8):
    if n <= pref:
        return n
    for t in range(pref, 0, -1):
        if n % t == 0 and t % align == 0:
            return t
    return n


def _mod_kernel(c_ref, w_ref, b_ref, o_ref):
    s = _silu(c_ref[...])
    o_ref[...] = _dot(s.astype(BF16), w_ref[...].astype(BF16)) + b_ref[...]


def _ada_mod(c, w_all, layer, b):
    m, d = c.shape
    n = w_all.shape[2]
    tn = _tile(n, 768, LANE)
    w = w_all
    return pl.pallas_call(
        _mod_kernel,
        out_shape=jax.ShapeDtypeStruct((m, n), F32),
        grid=(n // tn,),
        in_specs=[pl.BlockSpec((m, d), lambda j: (0, 0)),
                  pl.BlockSpec((None, d, tn), lambda j: (layer, 0, j)),
                  pl.BlockSpec((1, tn), lambda j: (0, j))],
        out_specs=pl.BlockSpec((m, tn), lambda j: (0, j)),
        compiler_params=_cp(("arbitrary",)),
        name="ada_mod",
    )(c, w, b.reshape(1, n))


def _inproj_kernel(x_ref, sc_ref, sh_ref, w_ref, b_ref, o_ref, u_ref):
    @pl.when(pl.program_id(1) == 0)
    def _():
        u = x_ref[...] * (1.0 + sc_ref[...]) + sh_ref[...]
        u_ref[...] = u.reshape(u_ref.shape).astype(BF16)

    o_ref[...] = _dot(u_ref[...], w_ref[...]) + b_ref[...]


def _token_blocks(B, T, rows):
    if T >= rows:
        return 1, _tile(T, rows)
    return _tile(B, max(1, rows // T), 1), T


def _in_proj(x, sc, sh, w, b):
    B, T, D = x.shape
    n = w.shape[1]
    bb, tt = _token_blocks(B, T, 1024)
    tm = bb * tt
    nt = T // tt
    tn = _tile(n, 1664, LANE)
    grid = ((B // bb) * nt, n // tn)
    return pl.pallas_call(
        _inproj_kernel,
        out_shape=jax.ShapeDtypeStruct((B * T, n), F32),
        grid=grid,
        in_specs=[pl.BlockSpec((bb, tt, D), lambda i, j: (i // nt, i % nt, 0)),
                  pl.BlockSpec((bb, 1, D), lambda i, j: (i // nt, 0, 0)),
                  pl.BlockSpec((bb, 1, D), lambda i, j: (i // nt, 0, 0)),
                  pl.BlockSpec((D, tn), lambda i, j: (0, j)),
                  pl.BlockSpec((1, tn), lambda i, j: (0, j))],
        out_specs=pl.BlockSpec((tm, tn), lambda i, j: (i, j)),
        scratch_shapes=[pltpu.VMEM((tm, D), BF16)],
        compiler_params=_cp(("parallel", "arbitrary")),
        name="in_proj",
    )(x, sc, sh, w, b)


def _stack_heads(ref, n_heads, width, lo=0):
    return jnp.concatenate([ref[:, :, lo + h * width:lo + (h + 1) * width] for h in range(n_heads)], axis=0)


def _mlstm_cells(q, k, v, og, ig_col, af_col, c_prev, n_prev, m_prev, L, mxu):
    row = lax.broadcasted_iota(jnp.int32, (L, L), 0)
    col = lax.broadcasted_iota(jnp.int32, (L, L), 1)
    tril = col <= row
    eye = col == row
    lf_col = _log_sigmoid(af_col)
    ig_row = jnp.sum(jnp.where(eye, ig_col, 0.0), axis=1, keepdims=True)
    b_row = jnp.sum(jnp.where(row <= col, lf_col, 0.0), axis=1, keepdims=True)
    b_col = jnp.sum(jnp.where(eye, b_row, 0.0), axis=2, keepdims=True)
    inter = b_col + m_prev
    intra = jnp.where(tril, b_col - b_row + ig_row, -jnp.inf)
    m_t = jnp.maximum(inter, jnp.max(intra, axis=2, keepdims=True))
    w_inter = jnp.exp(inter - m_t)
    dmat = jnp.exp(intra - m_t)
    qh = q * (A_DK ** -0.5)
    qb, kb, vb = qh.astype(mxu), k.astype(mxu), v.astype(mxu)
    s = _bdot_nt(qb, kb) * dmat
    num = _bdot(s.astype(mxu), vb) + w_inter * _bdot(qb, c_prev.astype(mxu))
    den = jnp.sum(s, axis=2, keepdims=True) + w_inter * jnp.sum(qh * n_prev, axis=2, keepdims=True)
    hv = num / jnp.maximum(jnp.abs(den), jnp.exp(-m_t))
    y = _sigmoid(og) * _ln_rows(hv)
    b_last = b_col[:, L - 1:L, :]
    m_new = m_t[:, L - 1:L, :]
    w_c = jnp.exp(b_last + m_prev - m_new)
    w_s = jnp.exp(b_last - b_col + ig_col - m_new)
    kw = k * w_s
    c_new = w_c * c_prev + _bdot_tn(kw.astype(mxu), vb)
    n_new = w_c * n_prev + jnp.sum(kw, axis=1, keepdims=True)
    return y, c_new, n_new, m_new


def _mlstm_kernel(q_ref, k_ref, v_ref, o_ref, g_ref, c0_ref, n0_ref, m0_ref,
                  y_ref, c_out, n_out, m_out, c_s, n_s, m_s, *, L, mxu, bb, carry):
    H = A_HEADS
    if carry:
        @pl.when(pl.program_id(1) == 0)
        def _():
            c_s[...] = c0_ref[0]
            n_s[...] = n0_ref[0]
            m_s[...] = m0_ref[0]
        c_prev, n_prev, m_prev = c_s[...], n_s[...], m_s[:, :, 0:1]
    else:
        c_prev = jnp.concatenate([c0_ref[:, h] for h in range(H)], axis=0)
        n_prev = jnp.concatenate([n0_ref[:, h] for h in range(H)], axis=0)
        m_prev = jnp.concatenate([m0_ref[:, h, :, 0:1] for h in range(H)], axis=0)

    y, c_new, n_new, m_new = _mlstm_cells(
        _stack_heads(q_ref, H, A_DK), _stack_heads(k_ref, H, A_DK), _stack_heads(v_ref, H, A_DV),
        _stack_heads(o_ref, H, A_DV), _stack_heads(g_ref, H, 1, MISC_AI), _stack_heads(g_ref, H, 1, MISC_AF),
        c_prev, n_prev, m_prev, L, mxu)
    m_new = jnp.broadcast_to(m_new, (H * bb, 1, LANE))
    for h in range(H):
        rows = slice(h * bb, (h + 1) * bb)
        y_ref[:, :, h * A_DV:(h + 1) * A_DV] = y[rows].astype(y_ref.dtype)
        if not carry:
            c_out[:, h] = c_new[rows]
            n_out[:, h] = n_new[rows]
            m_out[:, h] = m_new[rows]
    if carry:
        c_s[...] = c_new
        n_s[...] = n_new
        m_s[...] = m_new

        @pl.when(pl.program_id(1) == pl.num_programs(1) - 1)
        def _():
            c_out[0] = c_new
            n_out[0] = n_new
            m_out[0] = m_new


SMALL_SEQ_BATCH = 8


def _mlstm(z3, c0_all, layer, n0, m0, ydt):
    B, T, _ = z3.shape
    L = min(CHUNK, T)
    nc = T // L
    carry = nc > 1
    bb = 1 if carry else _tile(B, SMALL_SEQ_BATCH, 1)
    H = A_HEADS
    W = H * A_DK
    ab = Z_A // W
    n0r = n0.reshape(B, H, 1, A_DK)
    m0b = jnp.broadcast_to(m0[..., None, None], (B, H, 1, LANE))
    mxu = BF16 if L >= 16 else F32

    def zspec(k):
        return pl.BlockSpec((bb, L, W), lambda b, c: (b, c, ab + k))

    st_specs = (pl.BlockSpec((bb, H, A_DK, A_DV), lambda b, c: (b, 0, 0, 0)),
                pl.BlockSpec((bb, H, 1, A_DK), lambda b, c: (b, 0, 0, 0)),
                pl.BlockSpec((bb, H, 1, LANE), lambda b, c: (b, 0, 0, 0)))
    y, c1, n1, m1 = pl.pallas_call(
        functools.partial(_mlstm_kernel, L=L, mxu=mxu, bb=bb, carry=carry),
        out_shape=(jax.ShapeDtypeStruct((B, T, W), ydt),
                   jax.ShapeDtypeStruct((B, H, A_DK, A_DV), F32),
                   jax.ShapeDtypeStruct((B, H, 1, A_DK), F32),
                   jax.ShapeDtypeStruct((B, H, 1, LANE), F32)),
        grid=(B // bb, nc),
        in_specs=[zspec(0), zspec(1), zspec(2), zspec(3),
                  pl.BlockSpec((bb, L, LANE), lambda b, c: (b, c, (Z_B + Z_BW - LANE) // LANE)),
                  pl.BlockSpec((None, bb, H, A_DK, A_DV), lambda b, c: (layer, b, 0, 0, 0))] + list(st_specs[1:]),
        out_specs=(pl.BlockSpec((bb, L, W), lambda b, c: (b, c, 0)),) + st_specs,
        scratch_shapes=[pltpu.VMEM((H, A_DK, A_DV), F32), pltpu.VMEM((H, 1, A_DK), F32),
                        pltpu.VMEM((H, 1, LANE), F32)],
        compiler_params=_cp(("parallel", "arbitrary")),
        name="mlstm",
    )(z3, z3, z3, z3, z3, c0_all, n0r, m0b)
    return y, c1, n1.reshape(B, H, A_DK), m1[:, :, 0, 0]


_LOG_GAMMA = tuple(float(np.log(1.0 - 2.0 ** (-5.0 - h))) for h in range(D_HEADS))


def _ret_kernel(q_ref, k_ref, v_ref, g_ref, tab_ref, s0_ref, y_ref, s_out, s_s, *, L, mxu, bb, carry):
    if carry:
        @pl.when(pl.program_id(1) == 0)
        def _():
            s_s[...] = s0_ref[0]

    H = D_HEADS
    X = H * bb
    cos = tab_ref[:, 0:D_DK]
    sin = tab_ref[:, D_DK:2 * D_DK]
    row = lax.broadcasted_iota(jnp.int32, (L, L), 0)
    col = lax.broadcasted_iota(jnp.int32, (L, L), 1)
    diff = (row - col).astype(F32)
    jcol = lax.broadcasted_iota(jnp.int32, (L, 1), 0).astype(F32)

    def per_head(fn, shape):
        return jnp.concatenate([jnp.broadcast_to(fn(_LOG_GAMMA[h])[None], (bb,) + shape) for h in range(H)], axis=0)

    decay = per_head(lambda lg: jnp.where(diff >= 0.0, jnp.exp(jnp.maximum(diff, 0.0) * lg), 0.0), (L, L))
    w_in = per_head(lambda lg: jnp.exp((jcol + 1.0) * lg), (L, 1))
    w_st = per_head(lambda lg: jnp.exp((L - 1.0 - jcol) * lg), (L, 1))
    w_S = per_head(lambda lg: jnp.full((1, 1), float(np.exp(L * lg)), F32), (1, 1))

    def rotary(x):
        swapped = pltpu.roll(x.reshape(X * L, D_DK), D_DK // 2, 1).reshape(X, L, D_DK)
        return x * cos + swapped * sin

    qh = rotary(_stack_heads(q_ref, H, D_DK))
    kh = rotary(_stack_heads(k_ref, H, D_DK)) * (D_DK ** -0.5)
    vb = _stack_heads(v_ref, H, D_DV).astype(mxu)
    qb = qh.astype(mxu)
    s = _bdot_nt(qb, kh.astype(mxu)) * decay
    s_prev = s_s[...] if carry else jnp.concatenate([s0_ref[:, h] for h in range(H)], axis=0)
    o = _bdot(s.astype(mxu), vb) + w_in * _bdot(qb, s_prev.astype(mxu))
    y = _silu(_stack_heads(g_ref, H, D_DV)) * _ln_rows(o)
    s_new = w_S * s_prev + _bdot_tn((kh * w_st).astype(mxu), vb)
    for h in range(H):
        rows = slice(h * bb, (h + 1) * bb)
        y_ref[:, :, h * D_DV:(h + 1) * D_DV] = y[rows].astype(y_ref.dtype)
        if not carry:
            s_out[:, h] = s_new[rows]
    if carry:
        s_s[...] = s_new

        @pl.when(pl.program_id(1) == pl.num_programs(1) - 1)
        def _():
            s_out[0] = s_new


def _retention(z3, tab, s0_all, layer, ydt):
    B, T, _ = z3.shape
    L = min(CHUNK, T)
    nc = T // L
    carry = nc > 1
    bb = 1 if carry else _tile(B, SMALL_SEQ_BATCH, 1)
    H = D_HEADS
    W = H * D_DK
    db = Z_D // W
    mxu = BF16 if L >= 16 else F32

    def zspec(k):
        return pl.BlockSpec((bb, L, W), lambda b, c: (b, c, db + k))

    st_spec = pl.BlockSpec((bb, H, D_DK, D_DV), lambda b, c: (b, 0, 0, 0))
    return pl.pallas_call(
        functools.partial(_ret_kernel, L=L, mxu=mxu, bb=bb, carry=carry),
        out_shape=(jax.ShapeDtypeStruct((B, T, W), ydt),
                   jax.ShapeDtypeStruct((B, H, D_DK, D_DV), F32)),
        grid=(B // bb, nc),
        in_specs=[zspec(0), zspec(1), zspec(2), zspec(3),
                  pl.BlockSpec((L, 2 * D_DK), lambda b, c: (c, 0)),
                  pl.BlockSpec((None, bb, H, D_DK, D_DV), lambda b, c: (layer, b, 0, 0, 0))],
        out_specs=(pl.BlockSpec((bb, L, W), lambda b, c: (b, c, 0)), st_spec),
        scratch_shapes=[pltpu.VMEM((H, D_DK, D_DV), F32)],
        compiler_params=_cp(("parallel", "arbitrary")),
        name="retention",
    )(z3, z3, z3, z3, tab, s0_all)


_CPAD = 32


_SUB = 8
_CROWS = 64


def _conv_kernel(glu_ref, buf_ref, w_ref, b_ref, g_ref, be_ref, y_ref, buf_out, ext, *shifted, tt):
    ti = pl.program_id(1)
    lead = _CPAD - (C_WIDTH - 1)
    bb = ext.shape[0]

    @pl.when(ti == 0)
    def _():
        ext[:, 0:lead, :] = jnp.zeros((bb, lead, C_CH), F32)
        ext[:, lead:_CPAD, :] = buf_ref[...]
        if shifted:
            ext[:, _CPAD + tt:, :] = jnp.zeros((bb, ext.shape[1] - _CPAD - tt, C_CH), F32)

    glu = glu_ref[...]
    ext[:, _CPAD:_CPAD + tt, :] = glu[:, :, :C_CH] * _sigmoid(glu[:, :, C_CH:])

    def finish(acc):
        return _silu(_ln_rows(acc + b_ref[...]) * g_ref[...] + be_ref[...]).astype(y_ref.dtype)

    if shifted:
        sh = shifted[0]
        for r in range(1, _SUB):
            sh[r - 1] = ext[:, r:r + tt + _CPAD, :]
        for c0 in range(0, tt, _CROWS):
            acc = jnp.zeros((bb, _CROWS, C_CH), F32)
            for j in range(C_WIDTH):
                r = (lead + j) % _SUB
                a = lead + j - r + c0
                tap = ext[:, a:a + _CROWS, :] if r == 0 else sh[r - 1, :, a:a + _CROWS, :]
                acc = acc + tap * w_ref[j:j + 1, :]
            y_ref[:, c0:c0 + _CROWS, :] = finish(acc)
    else:
        acc = jnp.zeros((bb, tt, C_CH), F32)
        for j in range(C_WIDTH):
            acc = acc + ext[:, lead + j:lead + j + tt, :] * w_ref[j:j + 1, :]
        y_ref[...] = finish(acc)

    @pl.when(ti == pl.num_programs(1) - 1)
    def _():
        buf_out[...] = ext[:, tt + lead:tt + _CPAD, :]

    ext[:, 0:_CPAD, :] = ext[:, tt:tt + _CPAD, :]


def _conv(z3, buf0, w, b, g, be, ydt):
    B, T, _ = z3.shape
    bb, tt = _token_blocks(B, T, 256)
    nt = T // tt
    use_shifted = tt % _CROWS == 0
    ext_rows = tt + _CPAD + (_SUB if use_shifted else 0)
    scratch = [pltpu.VMEM((bb, ext_rows, C_CH), F32)]
    if use_shifted:
        scratch.append(pltpu.VMEM((_SUB - 1, bb, tt + _CPAD, C_CH), F32))
    vec = lambda a: a.reshape(1, C_CH)
    return pl.pallas_call(
        functools.partial(_conv_kernel, tt=tt),
        out_shape=(jax.ShapeDtypeStruct((B, T, C_CH), ydt),
                   jax.ShapeDtypeStruct((B, C_WIDTH - 1, C_CH), F32)),
        grid=(B // bb, nt),
        in_specs=[pl.BlockSpec((bb, tt, 2 * C_CH), lambda b, t: (b, t, Z_C // (2 * C_CH))),
                  pl.BlockSpec((bb, C_WIDTH - 1, C_CH), lambda b, t: (b, 0, 0)),
                  pl.BlockSpec((C_WIDTH, C_CH), lambda b, t: (0, 0)),
                  pl.BlockSpec((1, C_CH), lambda b, t: (0, 0)),
                  pl.BlockSpec((1, C_CH), lambda b, t: (0, 0)),
                  pl.BlockSpec((1, C_CH), lambda b, t: (0, 0))],
        out_specs=(pl.BlockSpec((bb, tt, C_CH), lambda b, t: (b, t, 0)),
                   pl.BlockSpec((bb, C_WIDTH - 1, C_CH), lambda b, t: (b, 0, 0))),
        scratch_shapes=scratch,
        compiler_params=_cp(("parallel", "arbitrary")),
        name="conv_module",
    )(z3, buf0, w, vec(b), vec(g), vec(be))


def _mla_common(zb_ref, tab, qn_ref, kvn_ref, wq1_ref, wq2_ref, rows_ref, q_scale):
    zb = zb_ref[...].reshape(-1, Z_BW)
    bq = zb[:, 0:B_Q_RANK]
    bkv = zb[:, B_Q_RANK:B_Q_RANK + B_KV_RANK]
    misc = zb[:, B_Q_RANK + B_KV_RANK:]
    qn = (_rms_rows(bq) * qn_ref[...]).astype(BF16)
    ckv = _rms_rows(bkv) * kvn_ref[...]
    cos_q = jnp.concatenate([tab[:, 0:LANE]] * B_HEADS, axis=1)
    sin_q = jnp.concatenate([tab[:, LANE:2 * LANE]] * B_HEADS, axis=1)
    q = (_dot(qn, wq1_ref[...]) * cos_q + _dot(qn, wq2_ref[...]) * sin_q) * q_scale
    kpe = misc * tab[:, 2 * LANE:3 * LANE] + pltpu.roll(misc, LANE // 2, 1) * tab[:, 3 * LANE:]
    lead = rows_ref.shape[:-1]
    rows_ref[:, :, 0:B_KV_RANK] = ckv.reshape(lead + (B_KV_RANK,))
    rows_ref[:, :, B_KV_RANK:CACHE_W] = kpe[:, 0:B_ROPE].reshape(lead + (B_ROPE,))
    return q, ckv, kpe


def _mla_prep_prompt_kernel(zb_ref, tab_ref, qn_ref, kvn_ref, wq1_ref, wq2_ref, wk_ref, wv_ref,
                            q_ref, k_ref, v_ref, rows_ref):
    q, ckv, kpe = _mla_common(zb_ref, tab_ref[...], qn_ref, kvn_ref, wq1_ref, wq2_ref, rows_ref,
                              MLA_SCALE * LOG2E)
    q_ref[0] = q.astype(BF16)
    cb = ckv.astype(BF16)
    kpe_hi = pltpu.roll(kpe, LANE // 2, 1)
    k = _dot(cb, wk_ref[...]) + jnp.concatenate([kpe_hi] * B_HEADS, axis=1)
    k_ref[0] = k.astype(BF16)
    lane = lax.broadcasted_iota(jnp.int32, (1, B_HEADS * HP), 1)
    ones_col = jnp.where(lane % HP == B_VDIM, 1.0, 0.0)
    v_ref[0] = (_dot(cb, wv_ref[...]) + ones_col).astype(BF16)


def _mla_prep_prompt(z3, tab, qn, kvn, wq1, wq2, wk, wv):
    B, T, _ = z3.shape
    tt = _tile(T, 512)
    nt = T // tt
    HW = B_HEADS * HP
    full = lambda a: pl.BlockSpec(a.shape, lambda b, t: (0,) * a.ndim)
    big = pl.BlockSpec((1, tt, HW), lambda b, t: (b, t, 0))
    return pl.pallas_call(
        _mla_prep_prompt_kernel,
        out_shape=(jax.ShapeDtypeStruct((B, T, HW), BF16),) * 3
        + (jax.ShapeDtypeStruct((B, T, CACHE_W), F32),),
        grid=(B, nt),
        in_specs=[pl.BlockSpec((1, tt, Z_BW), lambda b, t: (b, t, Z_B // Z_BW)),
                  pl.BlockSpec((tt, 4 * LANE), lambda b, t: (t, 0)),
                  full(qn), full(kvn), full(wq1), full(wq2), full(wk), full(wv)],
        out_specs=(big, big, big, pl.BlockSpec((1, tt, CACHE_W), lambda b, t: (b, t, 0))),
        compiler_params=_cp(("parallel", "parallel")),
        name="mla_prep_prompt",
    )(z3, tab, qn, kvn, wq1, wq2, wk, wv)


def _flash_kernel(q_ref, k_ref, v_ref, o_ref, *, tq, nq):
    row = lax.broadcasted_iota(jnp.int32, (tq, tq), 0)
    col = lax.broadcasted_iota(jnp.int32, (tq, tq), 1)
    causal = col <= row
    for qi in range(nq):
        q = q_ref[0, qi * tq:(qi + 1) * tq, :]
        m = acc = None
        for kj in range(qi + 1):
            k = k_ref[0, kj * tq:(kj + 1) * tq, :]
            v = v_ref[0, kj * tq:(kj + 1) * tq, :]
            s = _dot_nt(q, k)
            if kj == qi:
                s = jnp.where(causal, s, -jnp.inf)
            s_max = jnp.max(s, axis=1, keepdims=True)
            if kj == 0:
                m = s_max
                acc = _dot(jnp.exp2(s - m).astype(BF16), v)
            else:
                m_new = jnp.maximum(m, s_max)
                acc = jnp.exp2(m - m_new) * acc + _dot(jnp.exp2(s - m_new).astype(BF16), v)
                m = m_new
        o_ref[0, qi * tq:(qi + 1) * tq, :] = (acc / acc[:, B_VDIM:B_VDIM + 1]).astype(o_ref.dtype)


def _flash(q, k, v):
    B, T, HW = q.shape
    tq = _tile(T, 512)
    spec = pl.BlockSpec((1, T, HP), lambda b, h: (b, 0, h))
    return pl.pallas_call(
        functools.partial(_flash_kernel, tq=tq, nq=T // tq),
        out_shape=jax.ShapeDtypeStruct((B, T, HW), BF16),
        grid=(B, B_HEADS),
        in_specs=[spec, spec, spec],
        out_specs=spec,
        compiler_params=_cp(("parallel", "parallel")),
        name="mla_prompt_attention",
    )(q, k, v)


def _mla_prep_sample_kernel(zb_ref, tab_ref, qn_ref, kvn_ref, wq1_ref, wq2_ref, wuk_ref,
                            q_ref, rows_ref):
    bb, _, tt, _ = q_ref.shape
    tab = jnp.broadcast_to(tab_ref[...], (bb, tt, 4 * LANE)).reshape(bb * tt, 4 * LANE)
    q, _, _ = _mla_common(zb_ref, tab, qn_ref, kvn_ref, wq1_ref, wq2_ref, rows_ref, MLA_SCALE)
    lane = lax.broadcasted_iota(jnp.int32, (1, HP), 1)
    for h in range(B_HEADS):
        qh = q[:, h * HP:(h + 1) * HP]
        q_abs = _dot(qh.astype(BF16), wuk_ref[h])
        q_pe = jnp.where(lane < B_ROPE, pltpu.roll(qh, LANE // 2, 1), 0.0)
        q_ref[:, h, :, 0:B_KV_RANK] = q_abs.reshape(bb, tt, B_KV_RANK)
        q_ref[:, h, :, B_KV_RANK:] = q_pe[:, 0:B_ROPE].reshape(bb, tt, B_ROPE)


def _mla_prep_sample(z3, tab, qn, kvn, wq1, wq2, wuk):
    B, T, _ = z3.shape
    bb = _tile(B, max(1, 512 // T), 1)
    full = lambda a: pl.BlockSpec(a.shape, lambda i: (0,) * a.ndim)
    return pl.pallas_call(
        _mla_prep_sample_kernel,
        out_shape=(jax.ShapeDtypeStruct((B, B_HEADS, T, CACHE_W), F32),
                   jax.ShapeDtypeStruct((B, T, CACHE_W), F32)),
        grid=(B // bb,),
        in_specs=[pl.BlockSpec((bb, T, Z_BW), lambda i: (i, 0, Z_B // Z_BW)),
                  pl.BlockSpec((1, T, 4 * LANE), lambda i: (0, 0, 0)),
                  full(qn), full(kvn), full(wq1), full(wq2), full(wuk)],
        out_specs=(pl.BlockSpec((bb, B_HEADS, T, CACHE_W), lambda i: (i, 0, 0, 0)),
                   pl.BlockSpec((bb, T, CACHE_W), lambda i: (i, 0, 0))),
        compiler_params=_cp(("parallel",)),
        name="mla_prep_sample",
    )(z3, tab, qn, kvn, wq1, wq2, wuk)


def _sattn_kernel(pt_ref, q_ref, new_ref, wv_ref, cache_ref, y_ref, kbuf, sem, *, layer, n_pages, n_rows, T):
    b = pl.program_id(0)
    R = B_HEADS * T

    def page_copy(row, g, slot):
        return pltpu.make_async_copy(cache_ref.at[layer, pt_ref[row, g]], kbuf.at[slot, g], sem.at[slot])

    def fetch(row, slot):
        for g in range(n_pages):
            page_copy(row, g, slot).start()

    ahead = kbuf.shape[0] - 1

    @pl.when(b == 0)
    def _():
        for row in range(min(ahead, n_rows)):
            fetch(row, row)

    @pl.when(b + ahead < n_rows)
    def _():
        fetch(b + ahead, (b + ahead) % (ahead + 1))

    slot = b % (ahead + 1)
    for g in range(n_pages):
        page_copy(b, g, slot).wait()

    q = q_ref[0].reshape(R, CACHE_W).astype(BF16)
    kt = jnp.concatenate([kbuf[slot, g].astype(BF16) for g in range(n_pages)], axis=1)
    pad = jnp.zeros((PAGE_SIZE - T, CACHE_W), F32)
    new = jnp.concatenate([new_ref[0], pad], axis=0).astype(BF16)
    r = lax.broadcasted_iota(jnp.int32, (R, PAGE_SIZE), 0)
    c = lax.broadcasted_iota(jnp.int32, (R, PAGE_SIZE), 1)
    s_past = _dot(q, kt)
    s_new = jnp.where(c <= r % T, _dot_nt(q, new), -jnp.inf)
    m = jnp.maximum(jnp.max(s_past, axis=1, keepdims=True), jnp.max(s_new, axis=1, keepdims=True))
    p_past = jnp.exp(s_past - m)
    p_new = jnp.exp(s_new - m)
    l = jnp.sum(p_past, axis=1, keepdims=True) + jnp.sum(p_new, axis=1, keepdims=True)
    acc = _dot_nt(p_past.astype(BF16), kt[0:B_KV_RANK, :]) + _dot(p_new.astype(BF16), new[:, 0:B_KV_RANK])
    o = acc / l
    for h in range(B_HEADS):
        oh = o[h * T:(h + 1) * T, :].astype(BF16)
        y_ref[0, :, h * HP:(h + 1) * HP] = _dot(oh, wv_ref[h]).astype(y_ref.dtype)


def _sample_attention(layer, q, new_rows, wv, cache_t, page_table, ydt):
    B, H, T, _ = q.shape
    n_pages = page_table.shape[1]
    grid_spec = pltpu.PrefetchScalarGridSpec(
        num_scalar_prefetch=1,
        grid=(B,),
        in_specs=[pl.BlockSpec((1, H, T, CACHE_W), lambda b, pt: (b, 0, 0, 0)),
                  pl.BlockSpec((1, T, CACHE_W), lambda b, pt: (b, 0, 0)),
                  pl.BlockSpec(wv.shape, lambda b, pt: (0, 0, 0)),
                  pl.BlockSpec(memory_space=pl.ANY)],
        out_specs=pl.BlockSpec((1, T, H * HP), lambda b, pt: (b, 0, 0)),
        scratch_shapes=[pltpu.VMEM((GATHER_SLOTS, n_pages, CACHE_W, PAGE_SIZE), F32),
                        pltpu.SemaphoreType.DMA((GATHER_SLOTS,))],
    )
    return pl.pallas_call(
        functools.partial(_sattn_kernel, layer=layer, n_pages=n_pages, n_rows=B, T=T),
        out_shape=jax.ShapeDtypeStruct((B, T, H * HP), ydt),
        grid_spec=grid_spec,
        compiler_params=_cp(("arbitrary",)),
        name="mla_sample_attention",
    )(page_table, q, new_rows, wv, cache_t)


def _merge_kernel(ya_ref, yb_ref, yc_ref, yd_ref, gt_ref, x_ref, g1_ref, wa_ref, wb_ref, wc_ref, wd_ref,
                  wo_ref, lg_ref, lb_ref, o_ref):
    rows = o_ref.shape[0] * o_ref.shape[1]

    def flat(ref):
        return ref[...].reshape(rows, ref.shape[-1])

    gates = flat(gt_ref)
    acc = None
    for n, (y_ref, w_ref) in enumerate(((ya_ref, wa_ref), (yb_ref, wb_ref), (yc_ref, wc_ref), (yd_ref, wd_ref))):
        term = _sigmoid(gates[:, n * D_MODEL:(n + 1) * D_MODEL]) * _dot(flat(y_ref).astype(BF16), w_ref[...])
        acc = term if acc is None else acc + term
    mix = _dot(acc.astype(BF16), wo_ref[...]).reshape(o_ref.shape)
    v = DEEPNORM_ALPHA * x_ref[...] + g1_ref[...] * mix
    o_ref[...] = _ln_rows(v) * lg_ref[...] + lb_ref[...]


def _merge(ya, yb, yc, yd, z3, x, g1, wa, wb, wc, wd, wo, lg, lb):
    B, T, D = x.shape
    bb, tt = _token_blocks(B, T, 256)
    nt = T // tt
    tok = lambda w: pl.BlockSpec((bb, tt, w), lambda i: (i // nt, i % nt, 0))
    full = lambda a: pl.BlockSpec(a.shape, lambda i: (0,) * a.ndim)
    vec = lambda a: a.reshape(1, 1, D)
    return pl.pallas_call(
        _merge_kernel,
        out_shape=jax.ShapeDtypeStruct((B, T, D), F32),
        grid=((B // bb) * nt,),
        in_specs=[tok(ya.shape[-1]), tok(yb.shape[-1]), tok(yc.shape[-1]), tok(yd.shape[-1]),
                  pl.BlockSpec((bb, tt, N_BRANCH * D), lambda i: (i // nt, i % nt, Z_G // (N_BRANCH * D_MODEL))),
                  tok(D),
                  pl.BlockSpec((bb, 1, D), lambda i: (i // nt, 0, 0)),
                  full(wa), full(wb), full(wc), full(wd), full(wo),
                  pl.BlockSpec((1, 1, D), lambda i: (0, 0, 0)), pl.BlockSpec((1, 1, D), lambda i: (0, 0, 0))],
        out_specs=tok(D),
        compiler_params=_cp(("parallel",)),
        name="merge_out_ln",
    )(ya, yb, yc, yd, z3, x, g1, wa, wb, wc, wd, wo, vec(lg), vec(lb))


def _ffn_kernel(x_ref, sc_ref, sh_ref, g2_ref, w1_ref, w3_ref, w2_ref, lg_ref, lb_ref, o_ref, u_s, acc_s):
    j = pl.program_id(1)

    @pl.when(j == 0)
    def _():
        u = x_ref[...] * (1.0 + sc_ref[...]) + sh_ref[...]
        u_s[...] = u.reshape(u_s.shape).astype(BF16)
        acc_s[...] = jnp.zeros(acc_s.shape, F32)

    u = u_s[...]
    hid = _silu(_dot(u, w1_ref[...])) * _dot(u, w3_ref[...])
    acc_s[...] += _dot(hid.astype(BF16), w2_ref[...])

    @pl.when(j == pl.num_programs(1) - 1)
    def _():
        v = DEEPNORM_ALPHA * x_ref[...] + g2_ref[...] * acc_s[...].reshape(o_ref.shape)
        o_ref[...] = _ln_rows(v) * lg_ref[...] + lb_ref[...]


def _ffn(x, sc, sh, g2, w1, w3, w2, lg, lb):
    B, T, D = x.shape
    Hd = w1.shape[1]
    bb, tt = _token_blocks(B, T, 512)
    nt = T // tt
    tm = bb * tt
    th = _tile(Hd, 1408, LANE)
    tok = pl.BlockSpec((bb, tt, D), lambda i, j: (i // nt, i % nt, 0))
    per_b = pl.BlockSpec((bb, 1, D), lambda i, j: (i // nt, 0, 0))
    vspec = pl.BlockSpec((1, 1, D), lambda i, j: (0, 0, 0))
    vec = lambda a: a.reshape(1, 1, D)
    return pl.pallas_call(
        _ffn_kernel,
        out_shape=jax.ShapeDtypeStruct((B, T, D), F32),
        grid=((B // bb) * nt, Hd // th),
        in_specs=[tok, per_b, per_b, per_b,
                  pl.BlockSpec((D, th), lambda i, j: (0, j)),
                  pl.BlockSpec((D, th), lambda i, j: (0, j)),
                  pl.BlockSpec((th, D), lambda i, j: (j, 0)),
                  vspec, vspec],
        out_specs=tok,
        scratch_shapes=[pltpu.VMEM((tm, D), BF16), pltpu.VMEM((tm, D), F32)],
        compiler_params=_cp(("parallel", "arbitrary")),
        name="ffn_ln",
    )(x, sc, sh, g2, w1, w3, w2, vec(lg), vec(lb))


def _rope_tables(pos):
    posf = pos.astype(F32)[:, None]
    T = pos.shape[0]

    def cs(d):
        inv = ROPE_BASE ** (-jnp.arange(0, d, 2, dtype=F32) / d)
        ang = posf * inv[None, :]
        c, s = jnp.cos(ang), jnp.sin(ang)
        return jnp.concatenate([c, c], axis=1), jnp.concatenate([-s, s], axis=1)

    c32, s32 = cs(B_ROPE)
    z = lambda w: jnp.zeros((T, w), F32)
    cos_q = jnp.concatenate([jnp.ones((T, B_NOPE), F32), c32, z(HP - B_NOPE - B_ROPE)], axis=1)
    sin_q = jnp.concatenate([z(B_NOPE), s32, z(HP - B_NOPE - B_ROPE)], axis=1)
    cos_k = jnp.concatenate([c32, z(LANE - B_ROPE)], axis=1)
    sin_k = jnp.concatenate([s32, z(LANE - B_ROPE)], axis=1)
    tab_b = jnp.concatenate([cos_q, sin_q, cos_k, sin_k], axis=1)
    c128, s128 = cs(D_DK)
    return tab_b, jnp.concatenate([c128, s128], axis=1)


def _swap_halves(a, lo, width):
    half = width // 2
    return jnp.concatenate([a[..., lo + half:lo + width], a[..., lo:lo + half]], axis=-1)


def _layer_weights(l, w_in, b_in, mla_w_uq, mla_w_uk, mla_w_uv, w_branch):
    def reorder(a):
        zeros = lambda w: jnp.zeros(a.shape[:-1] + (w,), a.dtype)
        return jnp.concatenate([
            a[..., O_G:N_IN], a[..., O_AQ:O_AI], a[..., O_D:O_G], a[..., O_C:O_D],
            a[..., O_BQ:O_BKR + B_ROPE], a[..., O_AI:O_BQ], zeros(MISC_KRSW - MISC_AF - A_HEADS),
            _swap_halves(a, O_BKR, B_ROPE), zeros(LANE - MISC_KRSW - B_ROPE)], axis=-1)

    w_in_r = reorder(w_in[l]).astype(BF16)
    b_in_r = reorder(b_in[l]).reshape(1, Z_W)

    hd = B_NOPE + B_ROPE
    wq = mla_w_uq[l].reshape(B_Q_RANK, B_HEADS, hd)
    zq = jnp.zeros((B_Q_RANK, B_HEADS, HP - hd), F32)
    wq1 = jnp.concatenate([wq, zq], axis=-1).reshape(B_Q_RANK, B_HEADS * HP).astype(BF16)
    wq2 = jnp.concatenate([jnp.zeros((B_Q_RANK, B_HEADS, B_NOPE), F32), _swap_halves(wq, B_NOPE, B_ROPE), zq],
                          axis=-1).reshape(B_Q_RANK, B_HEADS * HP).astype(BF16)
    wuk = mla_w_uk[l]
    wuv = mla_w_uv[l]
    wk = jnp.concatenate([wuk, jnp.zeros((B_KV_RANK, B_HEADS, HP - B_NOPE), F32)], axis=-1)
    wv = jnp.concatenate([wuv, jnp.zeros((B_KV_RANK, B_HEADS, HP - B_VDIM), F32)], axis=-1)
    wk_flat = wk.reshape(B_KV_RANK, B_HEADS * HP).astype(BF16)
    wv_flat = wv.reshape(B_KV_RANK, B_HEADS * HP).astype(BF16)
    wuk_t = jnp.transpose(wk, (1, 2, 0)).astype(BF16)
    wv_h = jnp.transpose(wv, (1, 0, 2)).astype(BF16)
    wb = w_branch[l]
    wb_b = jnp.concatenate([wb[1].reshape(B_HEADS, B_VDIM, D_MODEL),
                            jnp.zeros((B_HEADS, HP - B_VDIM, D_MODEL), F32)], axis=1)
    wb_b = wb_b.reshape(B_HEADS * HP, D_MODEL).astype(BF16)
    return dict(w_in=w_in_r, b_in=b_in_r, wq1=wq1, wq2=wq2, wk=wk_flat, wv=wv_flat, wuk_t=wuk_t, wv_h=wv_h,
                wb_a=wb[0].astype(BF16), wb_b=wb_b, wb_c=wb[2].astype(BF16), wb_d=wb[3].astype(BF16))


def kernel(x_prompt, x_sample, cache_mla, state_mlstm_C, state_mlstm_n, state_mlstm_m, state_conv, state_ret,
           page_table, c_prompt, c_sample, w_ada, b_ada, w_in, b_in, mla_q_norm, mla_w_uq, mla_kv_norm,
           mla_w_uk, mla_w_uv, conv_w, conv_b, conv_ln_g, conv_ln_b, w_branch, w_out, ln1_g, ln1_b,
           w_ffn1, w_ffn3, w_ffn2, ln2_g, ln2_b):
    dt = x_prompt.dtype
    Bp, Tp, D = x_prompt.shape
    Bd, Td, _ = x_sample.shape
    past_len = page_table.shape[1] * PAGE_SIZE
    cache_t = jnp.swapaxes(cache_mla, 2, 3)
    tabs_p = _rope_tables(jnp.arange(Tp))
    tabs_d = _rope_tables(past_len + jnp.arange(Td))
    c_all = jnp.concatenate([c_prompt, c_sample], axis=0).astype(F32)

    xp, xd = x_prompt.astype(F32), x_sample.astype(F32)
    p_states = [[] for _ in range(6)]
    d_states = [[] for _ in range(6)]
    for l in range(DEPTH):
        lw = _layer_weights(l, w_in, b_in, mla_w_uq, mla_w_uk, mla_w_uv, w_branch)
        qn = mla_q_norm[l].reshape(1, B_Q_RANK)
        kvn = mla_kv_norm[l].reshape(1, B_KV_RANK)
        wo = w_out[l].astype(BF16)
        w1, w3, w2 = w_ffn1[l].astype(BF16), w_ffn3[l].astype(BF16), w_ffn2[l].astype(BF16)
        mod = _ada_mod(c_all, w_ada.astype(F32), l, b_ada[l])

        def group(x, mod_g, tabs, st, sample):
            B, T, _ = x.shape
            sh1, sc1, g1, sh2, sc2, g2 = [mod_g[:, None, i * D:(i + 1) * D] for i in range(6)]
            ydt = F32 if sample else BF16
            z3 = _in_proj(x, sc1, sh1, lw['w_in'], lw['b_in']).reshape(B, T, Z_W)
            (c0, c_layer), n0, m0, buf0, (s0, s_layer) = st
            ya, c1, n1, m1 = _mlstm(z3, c0, c_layer, n0, m0, ydt)
            yd, s1 = _retention(z3, tabs[1], s0, s_layer, ydt)
            yc, buf1 = _conv(z3, buf0, conv_w[l], conv_b[l], conv_ln_g[l], conv_ln_b[l], ydt)
            if sample:
                q, rows = _mla_prep_sample(z3, tabs[0].reshape(1, T, 4 * LANE), qn, kvn,
                                           lw['wq1'], lw['wq2'], lw['wuk_t'])
                yb = _sample_attention(l, q, rows, lw['wv_h'], cache_t, page_table, ydt)
            else:
                q, k, v, rows = _mla_prep_prompt(z3, tabs[0], qn, kvn, lw['wq1'], lw['wq2'], lw['wk'], lw['wv'])
                yb = _flash(q, k, v)
            x1 = _merge(ya, yb, yc, yd, z3, x, g1, lw['wb_a'], lw['wb_b'], lw['wb_c'], lw['wb_d'], wo,
                        ln1_g[l], ln1_b[l])
            x2 = _ffn(x1, sc2, sh2, g2, w1, w3, w2, ln2_g[l], ln2_b[l])
            return x2, (rows, c1, n1, m1, buf1, s1)

        st_p = ((jnp.zeros((1, Bp, A_HEADS, A_DK, A_DV), F32), 0), jnp.zeros((Bp, A_HEADS, A_DK), F32),
                jnp.zeros((Bp, A_HEADS), F32), jnp.zeros((Bp, C_WIDTH - 1, C_CH), F32),
                (jnp.zeros((1, Bp, D_HEADS, D_DK, D_DV), F32), 0))
        xp, new_p = group(xp, mod[:Bp], tabs_p, st_p, False)
        st_d = ((state_mlstm_C.astype(F32), l), state_mlstm_n[l].astype(F32), state_mlstm_m[l].astype(F32),
                state_conv[l].astype(F32), (state_ret.astype(F32), l))
        xd, new_d = group(xd, mod[Bp:], tabs_d, st_d, True)
        for i in range(6):
            p_states[i].append(new_p[i])
            d_states[i].append(new_d[i])
    ps = [jnp.stack(s, axis=0).astype(dt) for s in p_states]
    ds = [jnp.stack(s, axis=0).astype(dt) for s in d_states]
    return (xp.astype(dt), xd.astype(dt), ps[0], ds[0], ps[1], ds[1], ps[2], ds[2], ps[3], ds[3],
            ps[4], ds[4], ps[5], ds[5])
```

```python
import functools

import numpy as np
import jax
import jax.numpy as jnp
from jax import lax
from jax.experimental import pallas as pl
from jax.experimental.pallas import tpu as pltpu

F32 = jnp.float32
BF16 = jnp.bfloat16

D_MODEL = 1024
DEPTH = 2
PAGE_SIZE = 128
N_BRANCH = 4
A_HEADS, A_DK, A_DV = 4, 128, 128
B_HEADS, B_Q_RANK, B_KV_RANK, B_NOPE, B_ROPE, B_VDIM = 8, 384, 256, 64, 32, 64
C_CH, C_WIDTH = 512, 31
D_HEADS, D_DK, D_DV = 4, 128, 128
FFN_HIDDEN = -(-8 * D_MODEL // (3 * 256)) * 256
CHUNK = 128
ROPE_BASE = 10000.0
LN_EPS = 1e-5
RMS_EPS = 1e-6
DEEPNORM_ALPHA = (2 * DEPTH) ** 0.25
CACHE_W = B_KV_RANK + B_ROPE
MLA_SCALE = (B_NOPE + B_ROPE) ** -0.5
LOG2E = float(np.log2(np.e))
GATHER_SLOTS = 3
O_AQ, O_AI, O_AF = 0, 2048, 2052
O_BQ, O_BKV, O_BKR = 2056, 2440, 2696
O_C, O_D, O_G = 2728, 3752, 5800
N_IN = 9896

Z_G, Z_A, Z_D, Z_C = 0, 4096, 6144, 8192
Z_MAIN = 9216
Z_BW = 768
MISC_KR, MISC_AI, MISC_AF, MISC_KRSW = 0, 32, 36, 64
LANE = 128
HP = 128

VMEM_LIMIT = 56 * 1024 * 1024


def _cp(sem, vmem=VMEM_LIMIT):
    return pltpu.CompilerParams(dimension_semantics=sem, vmem_limit_bytes=vmem)


def _sigmoid(x):
    return 1.0 / (1.0 + jnp.exp(-x))


def _silu(x):
    return x * _sigmoid(x)


def _log_sigmoid(x):
    return jnp.minimum(x, 0.0) - jnp.log(1.0 + jnp.exp(-jnp.abs(x)))


def _ln_rows(x):
    mu = jnp.mean(x, axis=-1, keepdims=True)
    xc = x - mu
    var = jnp.mean(xc * xc, axis=-1, keepdims=True)
    return xc * lax.rsqrt(var + LN_EPS)


def _rms_rows(x):
    return x * lax.rsqrt(jnp.mean(x * x, axis=-1, keepdims=True) + RMS_EPS)


def _dot(a, b):
    return jnp.dot(a, b, preferred_element_type=F32)


def _dot_nt(a, b):
    return lax.dot_general(a, b, (((1,), (1,)), ((), ())), preferred_element_type=F32)


def _bdot(a, b):
    return lax.dot_general(a, b, (((2,), (1,)), ((0,), (0,))), preferred_element_type=F32)


def _bdot_nt(a, b):
    return lax.dot_general(a, b, (((2,), (2,)), ((0,), (0,))), preferred_element_type=F32)


def _bdot_tn(a, b):
    return lax.dot_general(a, b, (((1,), (1,)), ((0,), (0,))), preferred_element_type=F32)


def _tile(n, pref, align=8):
    if n <= pref:
        return n
    for t in range(pref, 0, -1):
        if n % t == 0 and t % align == 0:
            return t
    return n


def _mod_kernel(c_ref, w_ref, b_ref, o_ref):
    s = _silu(c_ref[...])
    o_ref[...] = _dot(s.astype(BF16), w_ref[...].astype(BF16)) + b_ref[...]


def _ada_mod(c, w_all, layer, b):
    m, d = c.shape
    n = w_all.shape[2]
    tn = _tile(n, 768, LANE)
    w = w_all
    return pl.pallas_call(
        _mod_kernel,
        out_shape=jax.ShapeDtypeStruct((m, n), F32),
        grid=(n // tn,),
        in_specs=[pl.BlockSpec((m, d), lambda j: (0, 0)),
                  pl.BlockSpec((None, d, tn), lambda j: (layer, 0, j)),
                  pl.BlockSpec((1, tn), lambda j: (0, j))],
        out_specs=pl.BlockSpec((m, tn), lambda j: (0, j)),
        compiler_params=_cp(("arbitrary",)),
        name="ada_mod",
    )(c, w, b.reshape(1, n))


def _inproj_kernel(x_ref, sc_ref, sh_ref, wm_ref, bm_ref, wz_ref, bz_ref, om_ref, oz_ref, u_ref, *, n_main):
    j = pl.program_id(1)

    @pl.when(j == 0)
    def _():
        u = x_ref[...] * (1.0 + sc_ref[...]) + sh_ref[...]
        u_ref[...] = u.reshape(u_ref.shape).astype(BF16)

    @pl.when(j < n_main)
    def _():
        om_ref[...] = (_dot(u_ref[...], wm_ref[...]) + bm_ref[...]).astype(om_ref.dtype)

    @pl.when(j == n_main)
    def _():
        oz_ref[...] = _dot(u_ref[...], wz_ref[...]) + bz_ref[...]


def _token_blocks(B, T, rows):
    if T >= rows:
        return 1, _tile(T, rows)
    return _tile(B, max(1, rows // T), 1), T


def _in_proj(x, sc, sh, wm, bm, wz, bz, main_dtype):
    B, T, D = x.shape
    bb, tt = _token_blocks(B, T, 1024)
    tm = bb * tt
    nt = T // tt
    tn = _tile(Z_MAIN, 1536, LANE)
    n_main = Z_MAIN // tn
    last = n_main - 1
    main, misc = pl.pallas_call(
        functools.partial(_inproj_kernel, n_main=n_main),
        out_shape=(jax.ShapeDtypeStruct((B * T, Z_MAIN), main_dtype),
                   jax.ShapeDtypeStruct((B * T, Z_BW), F32)),
        grid=((B // bb) * nt, n_main + 1),
        in_specs=[pl.BlockSpec((bb, tt, D), lambda i, j: (i // nt, i % nt, 0)),
                  pl.BlockSpec((bb, 1, D), lambda i, j: (i // nt, 0, 0)),
                  pl.BlockSpec((bb, 1, D), lambda i, j: (i // nt, 0, 0)),
                  pl.BlockSpec((D, tn), lambda i, j: (0, jnp.minimum(j, last))),
                  pl.BlockSpec((1, tn), lambda i, j: (0, jnp.minimum(j, last))),
                  pl.BlockSpec((D, Z_BW), lambda i, j: (0, 0)),
                  pl.BlockSpec((1, Z_BW), lambda i, j: (0, 0))],
        out_specs=(pl.BlockSpec((tm, tn), lambda i, j: (i, jnp.minimum(j, last))),
                   pl.BlockSpec((tm, Z_BW), lambda i, j: (i, 0))),
        scratch_shapes=[pltpu.VMEM((tm, D), BF16)],
        compiler_params=_cp(("parallel", "arbitrary")),
        name="in_proj",
    )(x, sc, sh, wm, bm, wz, bz)
    return main.reshape(B, T, Z_MAIN), misc.reshape(B, T, Z_BW)


def _stack_heads(ref, n_heads, width, lo=0):
    return jnp.concatenate([ref[:, :, lo + h * width:lo + (h + 1) * width].astype(F32) for h in range(n_heads)],
                           axis=0)


def _mlstm_cells(q, k, v, og, ig_col, af_col, c_prev, n_prev, m_prev, L, mxu):
    row = lax.broadcasted_iota(jnp.int32, (L, L), 0)
    col = lax.broadcasted_iota(jnp.int32, (L, L), 1)
    tril = col <= row
    eye = col == row
    lf_col = _log_sigmoid(af_col)
    ig_row = jnp.sum(jnp.where(eye, ig_col, 0.0), axis=1, keepdims=True)
    b_row = jnp.sum(jnp.where(row <= col, lf_col, 0.0), axis=1, keepdims=True)
    b_col = jnp.sum(jnp.where(eye, b_row, 0.0), axis=2, keepdims=True)
    inter = b_col + m_prev
    intra = jnp.where(tril, b_col - b_row + ig_row, -jnp.inf)
    m_t = jnp.maximum(inter, jnp.max(intra, axis=2, keepdims=True))
    w_inter = jnp.exp(inter - m_t)
    dmat = jnp.exp(intra - m_t)
    qh = q * (A_DK ** -0.5)
    qb, kb, vb = qh.astype(mxu), k.astype(mxu), v.astype(mxu)
    s = _bdot_nt(qb, kb) * dmat
    num = _bdot(s.astype(mxu), vb) + w_inter * _bdot(qb, c_prev.astype(mxu))
    den = jnp.sum(s, axis=2, keepdims=True) + w_inter * jnp.sum(qh * n_prev, axis=2, keepdims=True)
    hv = num / jnp.maximum(jnp.abs(den), jnp.exp(-m_t))
    y = _sigmoid(og) * _ln_rows(hv)
    b_last = b_col[:, L - 1:L, :]
    m_new = m_t[:, L - 1:L, :]
    w_c = jnp.exp(b_last + m_prev - m_new)
    w_s = jnp.exp(b_last - b_col + ig_col - m_new)
    kw = k * w_s
    c_new = w_c * c_prev + _bdot_tn(kw.astype(mxu), vb)
    n_new = w_c * n_prev + jnp.sum(kw, axis=1, keepdims=True)
    return y, c_new, n_new, m_new


def _mlstm_kernel(q_ref, k_ref, v_ref, o_ref, g_ref, c0_ref, n0_ref, m0_ref,
                  y_ref, c_out, n_out, m_out, c_s, n_s, m_s, *, L, mxu, bb, carry):
    H = A_HEADS
    if carry:
        @pl.when(pl.program_id(1) == 0)
        def _():
            c_s[...] = c0_ref[0]
            n_s[...] = n0_ref[0]
            m_s[...] = m0_ref[0]
        c_prev, n_prev, m_prev = c_s[...], n_s[...], m_s[:, :, 0:1]
    else:
        c_prev = jnp.concatenate([c0_ref[:, h] for h in range(H)], axis=0)
        n_prev = jnp.concatenate([n0_ref[:, h] for h in range(H)], axis=0)
        m_prev = jnp.concatenate([m0_ref[:, h, :, 0:1] for h in range(H)], axis=0)

    y, c_new, n_new, m_new = _mlstm_cells(
        _stack_heads(q_ref, H, A_DK), _stack_heads(k_ref, H, A_DK), _stack_heads(v_ref, H, A_DV),
        _stack_heads(o_ref, H, A_DV), _stack_heads(g_ref, H, 1, MISC_AI), _stack_heads(g_ref, H, 1, MISC_AF),
        c_prev, n_prev, m_prev, L, mxu)
    m_new = jnp.broadcast_to(m_new, (H * bb, 1, LANE))
    for h in range(H):
        rows = slice(h * bb, (h + 1) * bb)
        y_ref[:, :, h * A_DV:(h + 1) * A_DV] = y[rows].astype(y_ref.dtype)
        if not carry:
            c_out[:, h] = c_new[rows]
            n_out[:, h] = n_new[rows]
            m_out[:, h] = m_new[rows]
    if carry:
        c_s[...] = c_new
        n_s[...] = n_new
        m_s[...] = m_new

        @pl.when(pl.program_id(1) == pl.num_programs(1) - 1)
        def _():
            c_out[0] = c_new
            n_out[0] = n_new
            m_out[0] = m_new


SMALL_SEQ_BATCH = 8


def _mlstm(z3, zb, c0_all, layer, n0, m0, ydt):
    B, T, _ = z3.shape
    L = min(CHUNK, T)
    nc = T // L
    carry = nc > 1
    bb = 1 if carry else _tile(B, SMALL_SEQ_BATCH, 1)
    H = A_HEADS
    W = H * A_DK
    ab = Z_A // W
    n0r = n0.reshape(B, H, 1, A_DK)
    m0b = jnp.broadcast_to(m0[..., None, None], (B, H, 1, LANE))
    mxu = BF16 if L >= 16 else F32

    def zspec(k):
        return pl.BlockSpec((bb, L, W), lambda b, c: (b, c, ab + k))

    st_specs = (pl.BlockSpec((bb, H, A_DK, A_DV), lambda b, c: (b, 0, 0, 0)),
                pl.BlockSpec((bb, H, 1, A_DK), lambda b, c: (b, 0, 0, 0)),
                pl.BlockSpec((bb, H, 1, LANE), lambda b, c: (b, 0, 0, 0)))
    y, c1, n1, m1 = pl.pallas_call(
        functools.partial(_mlstm_kernel, L=L, mxu=mxu, bb=bb, carry=carry),
        out_shape=(jax.ShapeDtypeStruct((B, T, W), ydt),
                   jax.ShapeDtypeStruct((B, H, A_DK, A_DV), F32),
                   jax.ShapeDtypeStruct((B, H, 1, A_DK), F32),
                   jax.ShapeDtypeStruct((B, H, 1, LANE), F32)),
        grid=(B // bb, nc),
        in_specs=[zspec(0), zspec(1), zspec(2), zspec(3),
                  pl.BlockSpec((bb, L, LANE), lambda b, c: (b, c, (Z_BW - LANE) // LANE)),
                  pl.BlockSpec((None, bb, H, A_DK, A_DV), lambda b, c: (layer, b, 0, 0, 0))] + list(st_specs[1:]),
        out_specs=(pl.BlockSpec((bb, L, W), lambda b, c: (b, c, 0)),) + st_specs,
        scratch_shapes=[pltpu.VMEM((H, A_DK, A_DV), F32), pltpu.VMEM((H, 1, A_DK), F32),
                        pltpu.VMEM((H, 1, LANE), F32)],
        compiler_params=_cp(("parallel", "arbitrary")),
        name="mlstm",
    )(z3, z3, z3, z3, zb, c0_all, n0r, m0b)
    return y, c1, n1.reshape(B, H, A_DK), m1[:, :, 0, 0]


_LOG_GAMMA = tuple(float(np.log(1.0 - 2.0 ** (-5.0 - h))) for h in range(D_HEADS))


def _ret_kernel(q_ref, k_ref, v_ref, g_ref, tab_ref, s0_ref, y_ref, s_out, s_s, *, L, mxu, bb, carry):
    if carry:
        @pl.when(pl.program_id(1) == 0)
        def _():
            s_s[...] = s0_ref[0]

    H = D_HEADS
    X = H * bb
    cos = tab_ref[:, 0:D_DK]
    sin = tab_ref[:, D_DK:2 * D_DK]
    row = lax.broadcasted_iota(jnp.int32, (L, L), 0)
    col = lax.broadcasted_iota(jnp.int32, (L, L), 1)
    diff = (row - col).astype(F32)
    jcol = lax.broadcasted_iota(jnp.int32, (L, 1), 0).astype(F32)

    def per_head(fn, shape):
        return jnp.concatenate([jnp.broadcast_to(fn(_LOG_GAMMA[h])[None], (bb,) + shape) for h in range(H)], axis=0)

    decay = per_head(lambda lg: jnp.where(diff >= 0.0, jnp.exp(jnp.maximum(diff, 0.0) * lg), 0.0), (L, L))
    w_in = per_head(lambda lg: jnp.exp((jcol + 1.0) * lg), (L, 1))
    w_st = per_head(lambda lg: jnp.exp((L - 1.0 - jcol) * lg), (L, 1))
    w_S = per_head(lambda lg: jnp.full((1, 1), float(np.exp(L * lg)), F32), (1, 1))

    def rotary(x):
        swapped = pltpu.roll(x.reshape(X * L, D_DK), D_DK // 2, 1).reshape(X, L, D_DK)
        return x * cos + swapped * sin

    qh = rotary(_stack_heads(q_ref, H, D_DK))
    kh = rotary(_stack_heads(k_ref, H, D_DK)) * (D_DK ** -0.5)
    vb = _stack_heads(v_ref, H, D_DV).astype(mxu)
    qb = qh.astype(mxu)
    s = _bdot_nt(qb, kh.astype(mxu)) * decay
    s_prev = s_s[...] if carry else jnp.concatenate([s0_ref[:, h] for h in range(H)], axis=0)
    o = _bdot(s.astype(mxu), vb) + w_in * _bdot(qb, s_prev.astype(mxu))
    y = _silu(_stack_heads(g_ref, H, D_DV)) * _ln_rows(o)
    s_new = w_S * s_prev + _bdot_tn((kh * w_st).astype(mxu), vb)
    for h in range(H):
        rows = slice(h * bb, (h + 1) * bb)
        y_ref[:, :, h * D_DV:(h + 1) * D_DV] = y[rows].astype(y_ref.dtype)
        if not carry:
            s_out[:, h] = s_new[rows]
    if carry:
        s_s[...] = s_new

        @pl.when(pl.program_id(1) == pl.num_programs(1) - 1)
        def _():
            s_out[0] = s_new


def _retention(z3, tab, s0_all, layer, ydt):
    B, T, _ = z3.shape
    L = min(CHUNK, T)
    nc = T // L
    carry = nc > 1
    bb = 1 if carry else _tile(B, SMALL_SEQ_BATCH, 1)
    H = D_HEADS
    W = H * D_DK
    db = Z_D // W
    mxu = BF16 if L >= 16 else F32

    def zspec(k):
        return pl.BlockSpec((bb, L, W), lambda b, c: (b, c, db + k))

    st_spec = pl.BlockSpec((bb, H, D_DK, D_DV), lambda b, c: (b, 0, 0, 0))
    return pl.pallas_call(
        functools.partial(_ret_kernel, L=L, mxu=mxu, bb=bb, carry=carry),
        out_shape=(jax.ShapeDtypeStruct((B, T, W), ydt),
                   jax.ShapeDtypeStruct((B, H, D_DK, D_DV), F32)),
        grid=(B // bb, nc),
        in_specs=[zspec(0), zspec(1), zspec(2), zspec(3),
                  pl.BlockSpec((L, 2 * D_DK), lambda b, c: (c, 0)),
                  pl.BlockSpec((None, bb, H, D_DK, D_DV), lambda b, c: (layer, b, 0, 0, 0))],
        out_specs=(pl.BlockSpec((bb, L, W), lambda b, c: (b, c, 0)), st_spec),
        scratch_shapes=[pltpu.VMEM((H, D_DK, D_DV), F32)],
        compiler_params=_cp(("parallel", "arbitrary")),
        name="retention",
    )(z3, z3, z3, z3, tab, s0_all)


_CPAD = 32


_SUB = 8
_CROWS = 64


def _conv_kernel(glu_ref, buf_ref, w_ref, b_ref, g_ref, be_ref, y_ref, buf_out, ext, *shifted, tt):
    ti = pl.program_id(1)
    lead = _CPAD - (C_WIDTH - 1)
    bb = ext.shape[0]

    @pl.when(ti == 0)
    def _():
        ext[:, 0:lead, :] = jnp.zeros((bb, lead, C_CH), F32)
        ext[:, lead:_CPAD, :] = buf_ref[...]
        if shifted:
            ext[:, _CPAD + tt:, :] = jnp.zeros((bb, ext.shape[1] - _CPAD - tt, C_CH), F32)

    glu = glu_ref[...].astype(F32)
    ext[:, _CPAD:_CPAD + tt, :] = glu[:, :, :C_CH] * _sigmoid(glu[:, :, C_CH:])

    def finish(acc):
        return _silu(_ln_rows(acc + b_ref[...]) * g_ref[...] + be_ref[...]).astype(y_ref.dtype)

    if shifted:
        sh = shifted[0]
        for r in range(1, _SUB):
            sh[r - 1] = ext[:, r:r + tt + _CPAD, :]
        for c0 in range(0, tt, _CROWS):
            acc = jnp.zeros((bb, _CROWS, C_CH), F32)
            for j in range(C_WIDTH):
                r = (lead + j) % _SUB
                a = lead + j - r + c0
                tap = ext[:, a:a + _CROWS, :] if r == 0 else sh[r - 1, :, a:a + _CROWS, :]
                acc = acc + tap * w_ref[j:j + 1, :]
            y_ref[:, c0:c0 + _CROWS, :] = finish(acc)
    else:
        acc = jnp.zeros((bb, tt, C_CH), F32)
        for j in range(C_WIDTH):
            acc = acc + ext[:, lead + j:lead + j + tt, :] * w_ref[j:j + 1, :]
        y_ref[...] = finish(acc)

    @pl.when(ti == pl.num_programs(1) - 1)
    def _():
        buf_out[...] = ext[:, tt + lead:tt + _CPAD, :]

    ext[:, 0:_CPAD, :] = ext[:, tt:tt + _CPAD, :]


def _conv(z3, buf0, w, b, g, be, ydt):
    B, T, _ = z3.shape
    bb, tt = _token_blocks(B, T, 256)
    nt = T // tt
    use_shifted = tt % _CROWS == 0
    ext_rows = tt + _CPAD + (_SUB if use_shifted else 0)
    scratch = [pltpu.VMEM((bb, ext_rows, C_CH), F32)]
    if use_shifted:
        scratch.append(pltpu.VMEM((_SUB - 1, bb, tt + _CPAD, C_CH), F32))
    vec = lambda a: a.reshape(1, C_CH)
    return pl.pallas_call(
        functools.partial(_conv_kernel, tt=tt),
        out_shape=(jax.ShapeDtypeStruct((B, T, C_CH), ydt),
                   jax.ShapeDtypeStruct((B, C_WIDTH - 1, C_CH), F32)),
        grid=(B // bb, nt),
        in_specs=[pl.BlockSpec((bb, tt, 2 * C_CH), lambda b, t: (b, t, Z_C // (2 * C_CH))),
                  pl.BlockSpec((bb, C_WIDTH - 1, C_CH), lambda b, t: (b, 0, 0)),
                  pl.BlockSpec((C_WIDTH, C_CH), lambda b, t: (0, 0)),
                  pl.BlockSpec((1, C_CH), lambda b, t: (0, 0)),
                  pl.BlockSpec((1, C_CH), lambda b, t: (0, 0)),
                  pl.BlockSpec((1, C_CH), lambda b, t: (0, 0))],
        out_specs=(pl.BlockSpec((bb, tt, C_CH), lambda b, t: (b, t, 0)),
                   pl.BlockSpec((bb, C_WIDTH - 1, C_CH), lambda b, t: (b, 0, 0))),
        scratch_shapes=scratch,
        compiler_params=_cp(("parallel", "arbitrary")),
        name="conv_module",
    )(z3, buf0, w, vec(b), vec(g), vec(be))


def _mla_common(zb_ref, tab, qn_ref, kvn_ref, wq1_ref, wq2_ref, rows_ref, q_scale):
    zb = zb_ref[...].reshape(-1, Z_BW)
    bq = zb[:, 0:B_Q_RANK]
    bkv = zb[:, B_Q_RANK:B_Q_RANK + B_KV_RANK]
    misc = zb[:, B_Q_RANK + B_KV_RANK:]
    qn = (_rms_rows(bq) * qn_ref[...]).astype(BF16)
    ckv = _rms_rows(bkv) * kvn_ref[...]
    cos_q = jnp.concatenate([tab[:, 0:LANE]] * B_HEADS, axis=1)
    sin_q = jnp.concatenate([tab[:, LANE:2 * LANE]] * B_HEADS, axis=1)
    q = (_dot(qn, wq1_ref[...]) * cos_q + _dot(qn, wq2_ref[...]) * sin_q) * q_scale
    kpe = misc * tab[:, 2 * LANE:3 * LANE] + pltpu.roll(misc, LANE // 2, 1) * tab[:, 3 * LANE:]
    lead = rows_ref.shape[:-1]
    rows_ref[:, :, 0:B_KV_RANK] = ckv.reshape(lead + (B_KV_RANK,))
    rows_ref[:, :, B_KV_RANK:CACHE_W] = kpe[:, 0:B_ROPE].reshape(lead + (B_ROPE,))
    return q, ckv, kpe


def _mla_prep_prompt_kernel(zb_ref, tab_ref, qn_ref, kvn_ref, wq1_ref, wq2_ref, wk_ref, wv_ref,
                            q_ref, k_ref, v_ref, rows_ref):
    q, ckv, kpe = _mla_common(zb_ref, tab_ref[...], qn_ref, kvn_ref, wq1_ref, wq2_ref, rows_ref,
                              MLA_SCALE * LOG2E)
    q_ref[0] = q.astype(BF16)
    cb = ckv.astype(BF16)
    kpe_hi = pltpu.roll(kpe, LANE // 2, 1)
    k = _dot(cb, wk_ref[...]) + jnp.concatenate([kpe_hi] * B_HEADS, axis=1)
    k_ref[0] = k.astype(BF16)
    lane = lax.broadcasted_iota(jnp.int32, (1, B_HEADS * HP), 1)
    ones_col = jnp.where(lane % HP == B_VDIM, 1.0, 0.0)
    v_ref[0] = (_dot(cb, wv_ref[...]) + ones_col).astype(BF16)


def _mla_prep_prompt(z3, tab, qn, kvn, wq1, wq2, wk, wv):
    B, T, _ = z3.shape
    tt = _tile(T, 512)
    nt = T // tt
    HW = B_HEADS * HP
    full = lambda a: pl.BlockSpec(a.shape, lambda b, t: (0,) * a.ndim)
    big = pl.BlockSpec((1, tt, HW), lambda b, t: (b, t, 0))
    return pl.pallas_call(
        _mla_prep_prompt_kernel,
        out_shape=(jax.ShapeDtypeStruct((B, T, HW), BF16),) * 3
        + (jax.ShapeDtypeStruct((B, T, CACHE_W), F32),),
        grid=(B, nt),
        in_specs=[pl.BlockSpec((1, tt, Z_BW), lambda b, t: (b, t, 0)),
                  pl.BlockSpec((tt, 4 * LANE), lambda b, t: (t, 0)),
                  full(qn), full(kvn), full(wq1), full(wq2), full(wk), full(wv)],
        out_specs=(big, big, big, pl.BlockSpec((1, tt, CACHE_W), lambda b, t: (b, t, 0))),
        compiler_params=_cp(("parallel", "parallel")),
        name="mla_prep_prompt",
    )(z3, tab, qn, kvn, wq1, wq2, wk, wv)


def _flash_kernel(q_ref, k_ref, v_ref, o_ref, *, tq, nq):
    row = lax.broadcasted_iota(jnp.int32, (tq, tq), 0)
    col = lax.broadcasted_iota(jnp.int32, (tq, tq), 1)
    causal = col <= row
    for qi in range(nq):
        q = q_ref[0, qi * tq:(qi + 1) * tq, :]
        m = acc = None
        for kj in range(qi + 1):
            k = k_ref[0, kj * tq:(kj + 1) * tq, :]
            v = v_ref[0, kj * tq:(kj + 1) * tq, :]
            s = _dot_nt(q, k)
            if kj == qi:
                s = jnp.where(causal, s, -jnp.inf)
            s_max = jnp.max(s, axis=1, keepdims=True)
            if kj == 0:
                m = s_max
                acc = _dot(jnp.exp2(s - m).astype(BF16), v)
            else:
                m_new = jnp.maximum(m, s_max)
                acc = jnp.exp2(m - m_new) * acc + _dot(jnp.exp2(s - m_new).astype(BF16), v)
                m = m_new
        o_ref[0, qi * tq:(qi + 1) * tq, :] = (acc / acc[:, B_VDIM:B_VDIM + 1]).astype(o_ref.dtype)


def _flash(q, k, v):
    B, T, HW = q.shape
    tq = _tile(T, 512)
    spec = pl.BlockSpec((1, T, HP), lambda b, h: (b, 0, h))
    return pl.pallas_call(
        functools.partial(_flash_kernel, tq=tq, nq=T // tq),
        out_shape=jax.ShapeDtypeStruct((B, T, HW), BF16),
        grid=(B, B_HEADS),
        in_specs=[spec, spec, spec],
        out_specs=spec,
        compiler_params=_cp(("parallel", "parallel")),
        name="mla_prompt_attention",
    )(q, k, v)


def _mla_prep_sample_kernel(zb_ref, tab_ref, qn_ref, kvn_ref, wq1_ref, wq2_ref, wuk_ref,
                            q_ref, rows_ref):
    bb, _, tt, _ = q_ref.shape
    tab = jnp.broadcast_to(tab_ref[...], (bb, tt, 4 * LANE)).reshape(bb * tt, 4 * LANE)
    q, _, _ = _mla_common(zb_ref, tab, qn_ref, kvn_ref, wq1_ref, wq2_ref, rows_ref, MLA_SCALE)
    lane = lax.broadcasted_iota(jnp.int32, (1, HP), 1)
    for h in range(B_HEADS):
        qh = q[:, h * HP:(h + 1) * HP]
        q_abs = _dot(qh.astype(BF16), wuk_ref[h])
        q_pe = jnp.where(lane < B_ROPE, pltpu.roll(qh, LANE // 2, 1), 0.0)
        q_ref[:, h, :, 0:B_KV_RANK] = q_abs.reshape(bb, tt, B_KV_RANK)
        q_ref[:, h, :, B_KV_RANK:] = q_pe[:, 0:B_ROPE].reshape(bb, tt, B_ROPE)


def _mla_prep_sample(z3, tab, qn, kvn, wq1, wq2, wuk):
    B, T, _ = z3.shape
    bb = _tile(B, max(1, 512 // T), 1)
    full = lambda a: pl.BlockSpec(a.shape, lambda i: (0,) * a.ndim)
    return pl.pallas_call(
        _mla_prep_sample_kernel,
        out_shape=(jax.ShapeDtypeStruct((B, B_HEADS, T, CACHE_W), F32),
                   jax.ShapeDtypeStruct((B, T, CACHE_W), F32)),
        grid=(B // bb,),
        in_specs=[pl.BlockSpec((bb, T, Z_BW), lambda i: (i, 0, 0)),
                  pl.BlockSpec((1, T, 4 * LANE), lambda i: (0, 0, 0)),
                  full(qn), full(kvn), full(wq1), full(wq2), full(wuk)],
        out_specs=(pl.BlockSpec((bb, B_HEADS, T, CACHE_W), lambda i: (i, 0, 0, 0)),
                   pl.BlockSpec((bb, T, CACHE_W), lambda i: (i, 0, 0))),
        compiler_params=_cp(("parallel",)),
        name="mla_prep_sample",
    )(z3, tab, qn, kvn, wq1, wq2, wuk)


def _sattn_kernel(pt_ref, q_ref, new_ref, wv_ref, cache_ref, y_ref, kbuf, sem, *, layer, n_pages, n_rows, T):
    b = pl.program_id(0)
    R = B_HEADS * T

    def page_copy(row, g, slot):
        return pltpu.make_async_copy(cache_ref.at[layer, pt_ref[row, g]], kbuf.at[slot, g], sem.at[slot])

    def fetch(row, slot):
        for g in range(n_pages):
            page_copy(row, g, slot).start()

    ahead = kbuf.shape[0] - 1

    @pl.when(b == 0)
    def _():
        for row in range(min(ahead, n_rows)):
            fetch(row, row)

    @pl.when(b + ahead < n_rows)
    def _():
        fetch(b + ahead, (b + ahead) % (ahead + 1))

    slot = b % (ahead + 1)
    for g in range(n_pages):
        page_copy(b, g, slot).wait()

    q = q_ref[0].reshape(R, CACHE_W).astype(BF16)
    kt = jnp.concatenate([kbuf[slot, g].astype(BF16) for g in range(n_pages)], axis=1)
    pad = jnp.zeros((PAGE_SIZE - T, CACHE_W), F32)
    new = jnp.concatenate([new_ref[0], pad], axis=0).astype(BF16)
    r = lax.broadcasted_iota(jnp.int32, (R, PAGE_SIZE), 0)
    c = lax.broadcasted_iota(jnp.int32, (R, PAGE_SIZE), 1)
    s_past = _dot(q, kt)
    s_new = jnp.where(c <= r % T, _dot_nt(q, new), -jnp.inf)
    m = jnp.maximum(jnp.max(s_past, axis=1, keepdims=True), jnp.max(s_new, axis=1, keepdims=True))
    p_past = jnp.exp(s_past - m)
    p_new = jnp.exp(s_new - m)
    l = jnp.sum(p_past, axis=1, keepdims=True) + jnp.sum(p_new, axis=1, keepdims=True)
    acc = _dot_nt(p_past.astype(BF16), kt[0:B_KV_RANK, :]) + _dot(p_new.astype(BF16), new[:, 0:B_KV_RANK])
    o = acc / l
    for h in range(B_HEADS):
        oh = o[h * T:(h + 1) * T, :].astype(BF16)
        y_ref[0, :, h * HP:(h + 1) * HP] = _dot(oh, wv_ref[h]).astype(y_ref.dtype)


def _sample_attention(layer, q, new_rows, wv, cache_t, page_table, ydt):
    B, H, T, _ = q.shape
    n_pages = page_table.shape[1]
    grid_spec = pltpu.PrefetchScalarGridSpec(
        num_scalar_prefetch=1,
        grid=(B,),
        in_specs=[pl.BlockSpec((1, H, T, CACHE_W), lambda b, pt: (b, 0, 0, 0)),
                  pl.BlockSpec((1, T, CACHE_W), lambda b, pt: (b, 0, 0)),
                  pl.BlockSpec(wv.shape, lambda b, pt: (0, 0, 0)),
                  pl.BlockSpec(memory_space=pl.ANY)],
        out_specs=pl.BlockSpec((1, T, H * HP), lambda b, pt: (b, 0, 0)),
        scratch_shapes=[pltpu.VMEM((GATHER_SLOTS, n_pages, CACHE_W, PAGE_SIZE), F32),
                        pltpu.SemaphoreType.DMA((GATHER_SLOTS,))],
    )
    return pl.pallas_call(
        functools.partial(_sattn_kernel, layer=layer, n_pages=n_pages, n_rows=B, T=T),
        out_shape=jax.ShapeDtypeStruct((B, T, H * HP), ydt),
        grid_spec=grid_spec,
        compiler_params=_cp(("arbitrary",)),
        name="mla_sample_attention",
    )(page_table, q, new_rows, wv, cache_t)


def _merge_kernel(ya_ref, yb_ref, yc_ref, yd_ref, gt_ref, x_ref, g1_ref, wa_ref, wb_ref, wc_ref, wd_ref,
                  wo_ref, lg_ref, lb_ref, o_ref):
    rows = o_ref.shape[0] * o_ref.shape[1]

    def flat(ref):
        return ref[...].reshape(rows, ref.shape[-1])

    gates = flat(gt_ref).astype(F32)
    acc = None
    for n, (y_ref, w_ref) in enumerate(((ya_ref, wa_ref), (yb_ref, wb_ref), (yc_ref, wc_ref), (yd_ref, wd_ref))):
        term = _sigmoid(gates[:, n * D_MODEL:(n + 1) * D_MODEL]) * _dot(flat(y_ref).astype(BF16), w_ref[...])
        acc = term if acc is None else acc + term
    mix = _dot(acc.astype(BF16), wo_ref[...]).reshape(o_ref.shape)
    v = DEEPNORM_ALPHA * x_ref[...] + g1_ref[...] * mix
    o_ref[...] = _ln_rows(v) * lg_ref[...] + lb_ref[...]


def _merge(ya, yb, yc, yd, z3, x, g1, wa, wb, wc, wd, wo, lg, lb):
    B, T, D = x.shape
    bb, tt = _token_blocks(B, T, 256)
    nt = T // tt
    tok = lambda w: pl.BlockSpec((bb, tt, w), lambda i: (i // nt, i % nt, 0))
    full = lambda a: pl.BlockSpec(a.shape, lambda i: (0,) * a.ndim)
    vec = lambda a: a.reshape(1, 1, D)
    return pl.pallas_call(
        _merge_kernel,
        out_shape=jax.ShapeDtypeStruct((B, T, D), F32),
        grid=((B // bb) * nt,),
        in_specs=[tok(ya.shape[-1]), tok(yb.shape[-1]), tok(yc.shape[-1]), tok(yd.shape[-1]),
                  pl.BlockSpec((bb, tt, N_BRANCH * D), lambda i: (i // nt, i % nt, Z_G // (N_BRANCH * D_MODEL))),
                  tok(D),
                  pl.BlockSpec((bb, 1, D), lambda i: (i // nt, 0, 0)),
                  full(wa), full(wb), full(wc), full(wd), full(wo),
                  pl.BlockSpec((1, 1, D), lambda i: (0, 0, 0)), pl.BlockSpec((1, 1, D), lambda i: (0, 0, 0))],
        out_specs=tok(D),
        compiler_params=_cp(("parallel",)),
        name="merge_out_ln",
    )(ya, yb, yc, yd, z3, x, g1, wa, wb, wc, wd, wo, vec(lg), vec(lb))


def _ffn_kernel(x_ref, sc_ref, sh_ref, g2_ref, w1_ref, w3_ref, w2_ref, lg_ref, lb_ref, o_ref, u_s, acc_s):
    j = pl.program_id(1)

    @pl.when(j == 0)
    def _():
        u = x_ref[...] * (1.0 + sc_ref[...]) + sh_ref[...]
        u_s[...] = u.reshape(u_s.shape).astype(BF16)
        acc_s[...] = jnp.zeros(acc_s.shape, F32)

    u = u_s[...]
    hid = _silu(_dot(u, w1_ref[...])) * _dot(u, w3_ref[...])
    acc_s[...] += _dot(hid.astype(BF16), w2_ref[...])

    @pl.when(j == pl.num_programs(1) - 1)
    def _():
        v = DEEPNORM_ALPHA * x_ref[...] + g2_ref[...] * acc_s[...].reshape(o_ref.shape)
        o_ref[...] = _ln_rows(v) * lg_ref[...] + lb_ref[...]


def _ffn(x, sc, sh, g2, w1, w3, w2, lg, lb):
    B, T, D = x.shape
    Hd = w1.shape[1]
    bb, tt = _token_blocks(B, T, 512)
    nt = T // tt
    tm = bb * tt
    th = _tile(Hd, 1408, LANE)
    tok = pl.BlockSpec((bb, tt, D), lambda i, j: (i // nt, i % nt, 0))
    per_b = pl.BlockSpec((bb, 1, D), lambda i, j: (i // nt, 0, 0))
    vspec = pl.BlockSpec((1, 1, D), lambda i, j: (0, 0, 0))
    vec = lambda a: a.reshape(1, 1, D)
    return pl.pallas_call(
        _ffn_kernel,
        out_shape=jax.ShapeDtypeStruct((B, T, D), F32),
        grid=((B // bb) * nt, Hd // th),
        in_specs=[tok, per_b, per_b, per_b,
                  pl.BlockSpec((D, th), lambda i, j: (0, j)),
                  pl.BlockSpec((D, th), lambda i, j: (0, j)),
                  pl.BlockSpec((th, D), lambda i, j: (j, 0)),
                  vspec, vspec],
        out_specs=tok,
        scratch_shapes=[pltpu.VMEM((tm, D), BF16), pltpu.VMEM((tm, D), F32)],
        compiler_params=_cp(("parallel", "arbitrary")),
        name="ffn_ln",
    )(x, sc, sh, g2, w1, w3, w2, vec(lg), vec(lb))


def _rope_tables(pos):
    posf = pos.astype(F32)[:, None]
    T = pos.shape[0]

    def cs(d):
        inv = ROPE_BASE ** (-jnp.arange(0, d, 2, dtype=F32) / d)
        ang = posf * inv[None, :]
        c, s = jnp.cos(ang), jnp.sin(ang)
        return jnp.concatenate([c, c], axis=1), jnp.concatenate([-s, s], axis=1)

    c32, s32 = cs(B_ROPE)
    z = lambda w: jnp.zeros((T, w), F32)
    cos_q = jnp.concatenate([jnp.ones((T, B_NOPE), F32), c32, z(HP - B_NOPE - B_ROPE)], axis=1)
    sin_q = jnp.concatenate([z(B_NOPE), s32, z(HP - B_NOPE - B_ROPE)], axis=1)
    cos_k = jnp.concatenate([c32, z(LANE - B_ROPE)], axis=1)
    sin_k = jnp.concatenate([s32, z(LANE - B_ROPE)], axis=1)
    tab_b = jnp.concatenate([cos_q, sin_q, cos_k, sin_k], axis=1)
    c128, s128 = cs(D_DK)
    return tab_b, jnp.concatenate([c128, s128], axis=1)


def _swap_halves(a, lo, width):
    half = width // 2
    return jnp.concatenate([a[..., lo + half:lo + width], a[..., lo:lo + half]], axis=-1)


def _layer_weights(l, w_in, b_in, mla_w_uq, mla_w_uk, mla_w_uv, w_branch):
    def main_cols(a):
        return jnp.concatenate([a[..., O_G:N_IN], a[..., O_AQ:O_AI], a[..., O_D:O_G], a[..., O_C:O_D]], axis=-1)

    def misc_cols(a):
        zeros = lambda w: jnp.zeros(a.shape[:-1] + (w,), a.dtype)
        return jnp.concatenate([
            a[..., O_BQ:O_BKR + B_ROPE], a[..., O_AI:O_BQ], zeros(MISC_KRSW - MISC_AF - A_HEADS),
            _swap_halves(a, O_BKR, B_ROPE), zeros(LANE - MISC_KRSW - B_ROPE)], axis=-1)

    w_main, w_misc = main_cols(w_in[l]).astype(BF16), misc_cols(w_in[l]).astype(BF16)
    b_main, b_misc = main_cols(b_in[l]).reshape(1, Z_MAIN), misc_cols(b_in[l]).reshape(1, Z_BW)

    hd = B_NOPE + B_ROPE
    wq = mla_w_uq[l].reshape(B_Q_RANK, B_HEADS, hd)
    zq = jnp.zeros((B_Q_RANK, B_HEADS, HP - hd), F32)
    wq1 = jnp.concatenate([wq, zq], axis=-1).reshape(B_Q_RANK, B_HEADS * HP).astype(BF16)
    wq2 = jnp.concatenate([jnp.zeros((B_Q_RANK, B_HEADS, B_NOPE), F32), _swap_halves(wq, B_NOPE, B_ROPE), zq],
                          axis=-1).reshape(B_Q_RANK, B_HEADS * HP).astype(BF16)
    wuk = mla_w_uk[l]
    wuv = mla_w_uv[l]
    wk = jnp.concatenate([wuk, jnp.zeros((B_KV_RANK, B_HEADS, HP - B_NOPE), F32)], axis=-1)
    wv = jnp.concatenate([wuv, jnp.zeros((B_KV_RANK, B_HEADS, HP - B_VDIM), F32)], axis=-1)
    wk_flat = wk.reshape(B_KV_RANK, B_HEADS * HP).astype(BF16)
    wv_flat = wv.reshape(B_KV_RANK, B_HEADS * HP).astype(BF16)
    wuk_t = jnp.transpose(wk, (1, 2, 0)).astype(BF16)
    wv_h = jnp.transpose(wv, (1, 0, 2)).astype(BF16)
    wb = w_branch[l]
    wb_b = jnp.concatenate([wb[1].reshape(B_HEADS, B_VDIM, D_MODEL),
                            jnp.zeros((B_HEADS, HP - B_VDIM, D_MODEL), F32)], axis=1)
    wb_b = wb_b.reshape(B_HEADS * HP, D_MODEL).astype(BF16)
    return dict(w_main=w_main, b_main=b_main, w_misc=w_misc, b_misc=b_misc, wq1=wq1, wq2=wq2, wk=wk_flat, wv=wv_flat, wuk_t=wuk_t, wv_h=wv_h,
                wb_a=wb[0].astype(BF16), wb_b=wb_b, wb_c=wb[2].astype(BF16), wb_d=wb[3].astype(BF16))


def kernel(x_prompt, x_sample, cache_mla, state_mlstm_C, state_mlstm_n, state_mlstm_m, state_conv, state_ret,
           page_table, c_prompt, c_sample, w_ada, b_ada, w_in, b_in, mla_q_norm, mla_w_uq, mla_kv_norm,
           mla_w_uk, mla_w_uv, conv_w, conv_b, conv_ln_g, conv_ln_b, w_branch, w_out, ln1_g, ln1_b,
           w_ffn1, w_ffn3, w_ffn2, ln2_g, ln2_b):
    dt = x_prompt.dtype
    Bp, Tp, D = x_prompt.shape
    Bd, Td, _ = x_sample.shape
    past_len = page_table.shape[1] * PAGE_SIZE
    cache_t = jnp.swapaxes(cache_mla, 2, 3)
    tabs_p = _rope_tables(jnp.arange(Tp))
    tabs_d = _rope_tables(past_len + jnp.arange(Td))
    c_all = jnp.concatenate([c_prompt, c_sample], axis=0).astype(F32)

    xp, xd = x_prompt.astype(F32), x_sample.astype(F32)
    p_states = [[] for _ in range(6)]
    d_states = [[] for _ in range(6)]
    for l in range(DEPTH):
        lw = _layer_weights(l, w_in, b_in, mla_w_uq, mla_w_uk, mla_w_uv, w_branch)
        qn = mla_q_norm[l].reshape(1, B_Q_RANK)
        kvn = mla_kv_norm[l].reshape(1, B_KV_RANK)
        wo = w_out[l].astype(BF16)
        w1, w3, w2 = w_ffn1[l].astype(BF16), w_ffn3[l].astype(BF16), w_ffn2[l].astype(BF16)
        mod = _ada_mod(c_all, w_ada.astype(F32), l, b_ada[l])

        def group(x, mod_g, tabs, st, sample):
            B, T, _ = x.shape
            sh1, sc1, g1, sh2, sc2, g2 = [mod_g[:, None, i * D:(i + 1) * D] for i in range(6)]
            ydt = F32 if sample else BF16
            z3, zb = _in_proj(x, sc1, sh1, lw['w_main'], lw['b_main'], lw['w_misc'], lw['b_misc'], ydt)
            (c0, c_layer), n0, m0, buf0, (s0, s_layer) = st
            ya, c1, n1, m1 = _mlstm(z3, zb, c0, c_layer, n0, m0, ydt)
            yd, s1 = _retention(z3, tabs[1], s0, s_layer, ydt)
            yc, buf1 = _conv(z3, buf0, conv_w[l], conv_b[l], conv_ln_g[l], conv_ln_b[l], ydt)
            if sample:
                q, rows = _mla_prep_sample(zb, tabs[0].reshape(1, T, 4 * LANE), qn, kvn,
                                           lw['wq1'], lw['wq2'], lw['wuk_t'])
                yb = _sample_attention(l, q, rows, lw['wv_h'], cache_t, page_table, ydt)
            else:
                q, k, v, rows = _mla_prep_prompt(zb, tabs[0], qn, kvn, lw['wq1'], lw['wq2'], lw['wk'], lw['wv'])
                yb = _flash(q, k, v)
            x1 = _merge(ya, yb, yc, yd, z3, x, g1, lw['wb_a'], lw['wb_b'], lw['wb_c'], lw['wb_d'], wo,
                        ln1_g[l], ln1_b[l])
            x2 = _ffn(x1, sc2, sh2, g2, w1, w3, w2, ln2_g[l], ln2_b[l])
            return x2, (rows, c1, n1, m1, buf1, s1)

        st_p = ((jnp.zeros((1, Bp, A_HEADS, A_DK, A_DV), F32), 0), jnp.zeros((Bp, A_HEADS, A_DK), F32),
                jnp.zeros((Bp, A_HEADS), F32), jnp.zeros((Bp, C_WIDTH - 1, C_CH), F32),
                (jnp.zeros((1, Bp, D_HEADS, D_DK, D_DV), F32), 0))
        xp, new_p = group(xp, mod[:Bp], tabs_p, st_p, False)
        st_d = ((state_mlstm_C.astype(F32), l), state_mlstm_n[l].astype(F32), state_mlstm_m[l].astype(F32),
                state_conv[l].astype(F32), (state_ret.astype(F32), l))
        xd, new_d = group(xd, mod[Bp:], tabs_d, st_d, True)
        for i in range(6):
            p_states[i].append(new_p[i])
            d_states[i].append(new_d[i])
    ps = [jnp.stack(s, axis=0).astype(dt) for s in p_states]
    ds = [jnp.stack(s, axis=0).astype(dt) for s in d_states]
    return (xp.astype(dt), xd.astype(dt), ps[0], ds[0], ps[1], ds[1], ps[2], ds[2], ps[3], ds[3],
            ps[4], ds[4], ps[5], ds[5])
```

```python
import functools

import numpy as np
import jax
import jax.numpy as jnp
from jax import lax
from jax.experimental import pallas as pl
from jax.experimental.pallas import tpu as pltpu

F32 = jnp.float32
BF16 = jnp.bfloat16

D_MODEL = 1024
DEPTH = 2
PAGE_SIZE = 128
N_BRANCH = 4
A_HEADS, A_DK, A_DV = 4, 128, 128
B_HEADS, B_Q_RANK, B_KV_RANK, B_NOPE, B_ROPE, B_VDIM = 8, 384, 256, 64, 32, 64
C_CH, C_WIDTH = 512, 31
D_HEADS, D_DK, D_DV = 4, 128, 128
FFN_HIDDEN = -(-8 * D_MODEL // (3 * 256)) * 256
CHUNK = 128
ROPE_BASE = 10000.0
LN_EPS = 1e-5
RMS_EPS = 1e-6
DEEPNORM_ALPHA = (2 * DEPTH) ** 0.25
CACHE_W = B_KV_RANK + B_ROPE
MLA_SCALE = (B_NOPE + B_ROPE) ** -0.5
LOG2E = float(np.log2(np.e))
GATHER_SLOTS = 3
O_AQ, O_AI, O_AF = 0, 2048, 2052
O_BQ, O_BKV, O_BKR = 2056, 2440, 2696
O_C, O_D, O_G = 2728, 3752, 5800
N_IN = 9896

Z_G, Z_A, Z_D, Z_C = 0, 4096, 6144, 8192
Z_MAIN = 9216
Z_BW = 768
MISC_KR, MISC_AI, MISC_AF, MISC_KRSW = 0, 32, 36, 64
LANE = 128
HP = 128

VMEM_LIMIT = 56 * 1024 * 1024


def _cp(sem, vmem=VMEM_LIMIT):
    return pltpu.CompilerParams(dimension_semantics=sem, vmem_limit_bytes=vmem)


def _sigmoid(x):
    return 1.0 / (1.0 + jnp.exp(-x))


def _silu(x):
    return x * _sigmoid(x)


def _log_sigmoid(x):
    return jnp.minimum(x, 0.0) - jnp.log(1.0 + jnp.exp(-jnp.abs(x)))


def _ln_rows(x):
    mu = jnp.mean(x, axis=-1, keepdims=True)
    xc = x - mu
    var = jnp.mean(xc * xc, axis=-1, keepdims=True)
    return xc * lax.rsqrt(var + LN_EPS)


def _rms_rows(x):
    return x * lax.rsqrt(jnp.mean(x * x, axis=-1, keepdims=True) + RMS_EPS)


def _dot(a, b):
    return jnp.dot(a, b, preferred_element_type=F32)


def _dot_nt(a, b):
    return lax.dot_general(a, b, (((1,), (1,)), ((), ())), preferred_element_type=F32)


def _bdot(a, b):
    return lax.dot_general(a, b, (((2,), (1,)), ((0,), (0,))), preferred_element_type=F32)


def _bdot_nt(a, b):
    return lax.dot_general(a, b, (((2,), (2,)), ((0,), (0,))), preferred_element_type=F32)


def _bdot_tn(a, b):
    return lax.dot_general(a, b, (((1,), (1,)), ((0,), (0,))), preferred_element_type=F32)


def _tile(n, pref, align=8):
    if n <= pref:
        return n
    for t in range(pref, 0, -1):
        if n % t == 0 and t % align == 0:
            return t
    return n


def _mod_kernel(c_ref, w_ref, b_ref, o_ref):
    s = _silu(c_ref[...])
    o_ref[...] = _dot(s.astype(BF16), w_ref[...].astype(BF16)) + b_ref[...]


def _ada_mod(c, w_all, layer, b):
    m, d = c.shape
    n = w_all.shape[2]
    tn = _tile(n, 768, LANE)
    w = w_all
    return pl.pallas_call(
        _mod_kernel,
        out_shape=jax.ShapeDtypeStruct((m, n), F32),
        grid=(n // tn,),
        in_specs=[pl.BlockSpec((m, d), lambda j: (0, 0)),
                  pl.BlockSpec((None, d, tn), lambda j: (layer, 0, j)),
                  pl.BlockSpec((1, tn), lambda j: (0, j))],
        out_specs=pl.BlockSpec((m, tn), lambda j: (0, j)),
        compiler_params=_cp(("arbitrary",)),
        name="ada_mod",
    )(c, w, b.reshape(1, n))


def _inproj_kernel(x_ref, sc_ref, sh_ref, wm_ref, bm_ref, wz_ref, bz_ref, om_ref, oz_ref, u_ref, *, n_main):
    j = pl.program_id(1)

    @pl.when(j == 0)
    def _():
        u = x_ref[...] * (1.0 + sc_ref[...]) + sh_ref[...]
        u_ref[...] = u.reshape(u_ref.shape).astype(BF16)

    @pl.when(j < n_main)
    def _():
        om_ref[...] = (_dot(u_ref[...], wm_ref[...]) + bm_ref[...]).astype(om_ref.dtype)

    @pl.when(j == n_main)
    def _():
        oz_ref[...] = _dot(u_ref[...], wz_ref[...]) + bz_ref[...]


def _token_blocks(B, T, rows):
    if T >= rows:
        return 1, _tile(T, rows)
    return _tile(B, max(1, rows // T), 1), T


def _in_proj(x, sc, sh, wm, bm, wz, bz, main_dtype):
    B, T, D = x.shape
    bb, tt = _token_blocks(B, T, 1024)
    tm = bb * tt
    nt = T // tt
    tn = _tile(Z_MAIN, 1536, LANE)
    n_main = Z_MAIN // tn
    last = n_main - 1
    main, misc = pl.pallas_call(
        functools.partial(_inproj_kernel, n_main=n_main),
        out_shape=(jax.ShapeDtypeStruct((B * T, Z_MAIN), main_dtype),
                   jax.ShapeDtypeStruct((B * T, Z_BW), F32)),
        grid=((B // bb) * nt, n_main + 1),
        in_specs=[pl.BlockSpec((bb, tt, D), lambda i, j: (i // nt, i % nt, 0)),
                  pl.BlockSpec((bb, 1, D), lambda i, j: (i // nt, 0, 0)),
                  pl.BlockSpec((bb, 1, D), lambda i, j: (i // nt, 0, 0)),
                  pl.BlockSpec((D, tn), lambda i, j: (0, jnp.minimum(j, last))),
                  pl.BlockSpec((1, tn), lambda i, j: (0, jnp.minimum(j, last))),
                  pl.BlockSpec((D, Z_BW), lambda i, j: (0, 0)),
                  pl.BlockSpec((1, Z_BW), lambda i, j: (0, 0))],
        out_specs=(pl.BlockSpec((tm, tn), lambda i, j: (i, jnp.minimum(j, last))),
                   pl.BlockSpec((tm, Z_BW), lambda i, j: (i, 0))),
        scratch_shapes=[pltpu.VMEM((tm, D), BF16)],
        compiler_params=_cp(("parallel", "arbitrary")),
        name="in_proj",
    )(x, sc, sh, wm, bm, wz, bz)
    return main.reshape(B, T, Z_MAIN), misc.reshape(B, T, Z_BW)


def _stack_heads(ref, n_heads, width, lo=0):
    return jnp.concatenate([ref[:, :, lo + h * width:lo + (h + 1) * width].astype(F32) for h in range(n_heads)],
                           axis=0)


def _mlstm_cells(q, k, v, og, ig_col, af_col, c_prev, n_prev, m_prev, L, mxu):
    row = lax.broadcasted_iota(jnp.int32, (L, L), 0)
    col = lax.broadcasted_iota(jnp.int32, (L, L), 1)
    tril = col <= row
    eye = col == row
    lf_col = _log_sigmoid(af_col)
    ig_row = jnp.sum(jnp.where(eye, ig_col, 0.0), axis=1, keepdims=True)
    b_row = jnp.sum(jnp.where(row <= col, lf_col, 0.0), axis=1, keepdims=True)
    b_col = jnp.sum(jnp.where(eye, b_row, 0.0), axis=2, keepdims=True)
    inter = b_col + m_prev
    intra = jnp.where(tril, b_col - b_row + ig_row, -jnp.inf)
    m_t = jnp.maximum(inter, jnp.max(intra, axis=2, keepdims=True))
    w_inter = jnp.exp(inter - m_t)
    dmat = jnp.exp(intra - m_t)
    qh = q * (A_DK ** -0.5)
    qb, kb, vb = qh.astype(mxu), k.astype(mxu), v.astype(mxu)
    s = _bdot_nt(qb, kb) * dmat
    num = _bdot(s.astype(mxu), vb) + w_inter * _bdot(qb, c_prev.astype(mxu))
    den = jnp.sum(s, axis=2, keepdims=True) + w_inter * jnp.sum(qh * n_prev, axis=2, keepdims=True)
    hv = num / jnp.maximum(jnp.abs(den), jnp.exp(-m_t))
    y = _sigmoid(og) * _ln_rows(hv)
    b_last = b_col[:, L - 1:L, :]
    m_new = m_t[:, L - 1:L, :]
    w_c = jnp.exp(b_last + m_prev - m_new)
    w_s = jnp.exp(b_last - b_col + ig_col - m_new)
    kw = k * w_s
    c_new = w_c * c_prev + _bdot_tn(kw.astype(mxu), vb)
    n_new = w_c * n_prev + jnp.sum(kw, axis=1, keepdims=True)
    return y, c_new, n_new, m_new


def _mlstm_kernel(q_ref, k_ref, v_ref, o_ref, g_ref, c0_ref, n0_ref, m0_ref,
                  y_ref, c_out, n_out, m_out, c_s, n_s, m_s, *, L, mxu, bb, carry):
    H = A_HEADS
    if carry:
        @pl.when(pl.program_id(1) == 0)
        def _():
            c_s[...] = c0_ref[0]
            n_s[...] = n0_ref[0]
            m_s[...] = m0_ref[0]
        c_prev, n_prev, m_prev = c_s[...], n_s[...], m_s[:, :, 0:1]
    else:
        c_prev = jnp.concatenate([c0_ref[:, h] for h in range(H)], axis=0)
        n_prev = jnp.concatenate([n0_ref[:, h] for h in range(H)], axis=0)
        m_prev = jnp.concatenate([m0_ref[:, h, :, 0:1] for h in range(H)], axis=0)

    y, c_new, n_new, m_new = _mlstm_cells(
        _stack_heads(q_ref, H, A_DK), _stack_heads(k_ref, H, A_DK), _stack_heads(v_ref, H, A_DV),
        _stack_heads(o_ref, H, A_DV), _stack_heads(g_ref, H, 1, MISC_AI), _stack_heads(g_ref, H, 1, MISC_AF),
        c_prev, n_prev, m_prev, L, mxu)
    m_new = jnp.broadcast_to(m_new, (H * bb, 1, LANE))
    for h in range(H):
        rows = slice(h * bb, (h + 1) * bb)
        y_ref[:, :, h * A_DV:(h + 1) * A_DV] = y[rows].astype(y_ref.dtype)
        if not carry:
            c_out[:, h] = c_new[rows]
            n_out[:, h] = n_new[rows]
            m_out[:, h] = m_new[rows]
    if carry:
        c_s[...] = c_new
        n_s[...] = n_new
        m_s[...] = m_new

        @pl.when(pl.program_id(1) == pl.num_programs(1) - 1)
        def _():
            c_out[0] = c_new
            n_out[0] = n_new
            m_out[0] = m_new


SMALL_SEQ_BATCH = 8


def _mlstm(z3, zb, c0_all, layer, n0, m0, ydt):
    B, T, _ = z3.shape
    L = min(CHUNK, T)
    nc = T // L
    carry = nc > 1
    bb = 1 if carry else _tile(B, SMALL_SEQ_BATCH, 1)
    H = A_HEADS
    W = H * A_DK
    ab = Z_A // W
    n0r = n0.reshape(B, H, 1, A_DK)
    m0b = jnp.broadcast_to(m0[..., None, None], (B, H, 1, LANE))
    mxu = BF16 if L >= 16 else F32

    def zspec(k):
        return pl.BlockSpec((bb, L, W), lambda b, c: (b, c, ab + k))

    st_specs = (pl.BlockSpec((bb, H, A_DK, A_DV), lambda b, c: (b, 0, 0, 0)),
                pl.BlockSpec((bb, H, 1, A_DK), lambda b, c: (b, 0, 0, 0)),
                pl.BlockSpec((bb, H, 1, LANE), lambda b, c: (b, 0, 0, 0)))
    y, c1, n1, m1 = pl.pallas_call(
        functools.partial(_mlstm_kernel, L=L, mxu=mxu, bb=bb, carry=carry),
        out_shape=(jax.ShapeDtypeStruct((B, T, W), ydt),
                   jax.ShapeDtypeStruct((B, H, A_DK, A_DV), F32),
                   jax.ShapeDtypeStruct((B, H, 1, A_DK), F32),
                   jax.ShapeDtypeStruct((B, H, 1, LANE), F32)),
        grid=(B // bb, nc),
        in_specs=[zspec(0), zspec(1), zspec(2), zspec(3),
                  pl.BlockSpec((bb, L, LANE), lambda b, c: (b, c, (Z_BW - LANE) // LANE)),
                  pl.BlockSpec((None, bb, H, A_DK, A_DV), lambda b, c: (layer, b, 0, 0, 0))] + list(st_specs[1:]),
        out_specs=(pl.BlockSpec((bb, L, W), lambda b, c: (b, c, 0)),) + st_specs,
        scratch_shapes=[pltpu.VMEM((H, A_DK, A_DV), F32), pltpu.VMEM((H, 1, A_DK), F32),
                        pltpu.VMEM((H, 1, LANE), F32)],
        compiler_params=_cp(("parallel", "arbitrary")),
        name="mlstm",
    )(z3, z3, z3, z3, zb, c0_all, n0r, m0b)
    return y, c1, n1.reshape(B, H, A_DK), m1[:, :, 0, 0]


_LOG_GAMMA = tuple(float(np.log(1.0 - 2.0 ** (-5.0 - h))) for h in range(D_HEADS))


def _ret_kernel(q_ref, k_ref, v_ref, g_ref, tab_ref, s0_ref, y_ref, s_out, s_s, *, L, mxu, bb, carry):
    if carry:
        @pl.when(pl.program_id(1) == 0)
        def _():
            s_s[...] = s0_ref[0]

    H = D_HEADS
    X = H * bb
    cos = tab_ref[:, 0:D_DK]
    sin = tab_ref[:, D_DK:2 * D_DK]
    row = lax.broadcasted_iota(jnp.int32, (L, L), 0)
    col = lax.broadcasted_iota(jnp.int32, (L, L), 1)
    diff = (row - col).astype(F32)
    jcol = lax.broadcasted_iota(jnp.int32, (L, 1), 0).astype(F32)

    def per_head(fn, shape):
        return jnp.concatenate([jnp.broadcast_to(fn(_LOG_GAMMA[h])[None], (bb,) + shape) for h in range(H)], axis=0)

    decay = per_head(lambda lg: jnp.where(diff >= 0.0, jnp.exp(jnp.maximum(diff, 0.0) * lg), 0.0), (L, L))
    w_in = per_head(lambda lg: jnp.exp((jcol + 1.0) * lg), (L, 1))
    w_st = per_head(lambda lg: jnp.exp((L - 1.0 - jcol) * lg), (L, 1))
    w_S = per_head(lambda lg: jnp.full((1, 1), float(np.exp(L * lg)), F32), (1, 1))

    def rotary(x):
        swapped = pltpu.roll(x.reshape(X * L, D_DK), D_DK // 2, 1).reshape(X, L, D_DK)
        return x * cos + swapped * sin

    qh = rotary(_stack_heads(q_ref, H, D_DK))
    kh = rotary(_stack_heads(k_ref, H, D_DK)) * (D_DK ** -0.5)
    vb = _stack_heads(v_ref, H, D_DV).astype(mxu)
    qb = qh.astype(mxu)
    s = _bdot_nt(qb, kh.astype(mxu)) * decay
    s_prev = s_s[...] if carry else jnp.concatenate([s0_ref[:, h] for h in range(H)], axis=0)
    o = _bdot(s.astype(mxu), vb) + w_in * _bdot(qb, s_prev.astype(mxu))
    y = _silu(_stack_heads(g_ref, H, D_DV)) * _ln_rows(o)
    s_new = w_S * s_prev + _bdot_tn((kh * w_st).astype(mxu), vb)
    for h in range(H):
        rows = slice(h * bb, (h + 1) * bb)
        y_ref[:, :, h * D_DV:(h + 1) * D_DV] = y[rows].astype(y_ref.dtype)
        if not carry:
            s_out[:, h] = s_new[rows]
    if carry:
        s_s[...] = s_new

        @pl.when(pl.program_id(1) == pl.num_programs(1) - 1)
        def _():
            s_out[0] = s_new


def _retention(z3, tab, s0_all, layer, ydt):
    B, T, _ = z3.shape
    L = min(CHUNK, T)
    nc = T // L
    carry = nc > 1
    bb = 1 if carry else _tile(B, SMALL_SEQ_BATCH, 1)
    H = D_HEADS
    W = H * D_DK
    db = Z_D // W
    mxu = BF16 if L >= 16 else F32

    def zspec(k):
        return pl.BlockSpec((bb, L, W), lambda b, c: (b, c, db + k))

    st_spec = pl.BlockSpec((bb, H, D_DK, D_DV), lambda b, c: (b, 0, 0, 0))
    return pl.pallas_call(
        functools.partial(_ret_kernel, L=L, mxu=mxu, bb=bb, carry=carry),
        out_shape=(jax.ShapeDtypeStruct((B, T, W), ydt),
                   jax.ShapeDtypeStruct((B, H, D_DK, D_DV), F32)),
        grid=(B // bb, nc),
        in_specs=[zspec(0), zspec(1), zspec(2), zspec(3),
                  pl.BlockSpec((L, 2 * D_DK), lambda b, c: (c, 0)),
                  pl.BlockSpec((None, bb, H, D_DK, D_DV), lambda b, c: (layer, b, 0, 0, 0))],
        out_specs=(pl.BlockSpec((bb, L, W), lambda b, c: (b, c, 0)), st_spec),
        scratch_shapes=[pltpu.VMEM((H, D_DK, D_DV), F32)],
        compiler_params=_cp(("parallel", "arbitrary")),
        name="retention",
    )(z3, z3, z3, z3, tab, s0_all)


_CPAD = 32


_SUB = 8
_CROWS = 64


def _conv_kernel(glu_ref, buf_ref, w_ref, b_ref, g_ref, be_ref, y_ref, buf_out, ext, *shifted, tt):
    ti = pl.program_id(1)
    lead = _CPAD - (C_WIDTH - 1)
    bb = ext.shape[0]

    @pl.when(ti == 0)
    def _():
        ext[:, 0:lead, :] = jnp.zeros((bb, lead, C_CH), F32)
        ext[:, lead:_CPAD, :] = buf_ref[...]
        if shifted:
            ext[:, _CPAD + tt:, :] = jnp.zeros((bb, ext.shape[1] - _CPAD - tt, C_CH), F32)

    glu = glu_ref[...].astype(F32)
    ext[:, _CPAD:_CPAD + tt, :] = glu[:, :, :C_CH] * _sigmoid(glu[:, :, C_CH:])

    def finish(acc):
        return _silu(_ln_rows(acc + b_ref[...]) * g_ref[...] + be_ref[...]).astype(y_ref.dtype)

    if shifted:
        sh = shifted[0]
        for r in range(1, _SUB):
            sh[r - 1] = ext[:, r:r + tt + _CPAD, :]
        for c0 in range(0, tt, _CROWS):
            acc = jnp.zeros((bb, _CROWS, C_CH), F32)
            for j in range(C_WIDTH):
                r = (lead + j) % _SUB
                a = lead + j - r + c0
                tap = ext[:, a:a + _CROWS, :] if r == 0 else sh[r - 1, :, a:a + _CROWS, :]
                acc = acc + tap * w_ref[j:j + 1, :]
            y_ref[:, c0:c0 + _CROWS, :] = finish(acc)
    else:
        acc = jnp.zeros((bb, tt, C_CH), F32)
        for j in range(C_WIDTH):
            acc = acc + ext[:, lead + j:lead + j + tt, :] * w_ref[j:j + 1, :]
        y_ref[...] = finish(acc)

    @pl.when(ti == pl.num_programs(1) - 1)
    def _():
        buf_out[...] = ext[:, tt + lead:tt + _CPAD, :]

    ext[:, 0:_CPAD, :] = ext[:, tt:tt + _CPAD, :]


def _conv(z3, buf0, w, b, g, be, ydt):
    B, T, _ = z3.shape
    bb, tt = _token_blocks(B, T, 512)
    nt = T // tt
    use_shifted = tt % _CROWS == 0
    ext_rows = tt + _CPAD + (_SUB if use_shifted else 0)
    scratch = [pltpu.VMEM((bb, ext_rows, C_CH), F32)]
    if use_shifted:
        scratch.append(pltpu.VMEM((_SUB - 1, bb, tt + _CPAD, C_CH), F32))
    vec = lambda a: a.reshape(1, C_CH)
    return pl.pallas_call(
        functools.partial(_conv_kernel, tt=tt),
        out_shape=(jax.ShapeDtypeStruct((B, T, C_CH), ydt),
                   jax.ShapeDtypeStruct((B, C_WIDTH - 1, C_CH), F32)),
        grid=(B // bb, nt),
        in_specs=[pl.BlockSpec((bb, tt, 2 * C_CH), lambda b, t: (b, t, Z_C // (2 * C_CH))),
                  pl.BlockSpec((bb, C_WIDTH - 1, C_CH), lambda b, t: (b, 0, 0)),
                  pl.BlockSpec((C_WIDTH, C_CH), lambda b, t: (0, 0)),
                  pl.BlockSpec((1, C_CH), lambda b, t: (0, 0)),
                  pl.BlockSpec((1, C_CH), lambda b, t: (0, 0)),
                  pl.BlockSpec((1, C_CH), lambda b, t: (0, 0))],
        out_specs=(pl.BlockSpec((bb, tt, C_CH), lambda b, t: (b, t, 0)),
                   pl.BlockSpec((bb, C_WIDTH - 1, C_CH), lambda b, t: (b, 0, 0))),
        scratch_shapes=scratch,
        compiler_params=_cp(("parallel", "arbitrary")),
        name="conv_module",
    )(z3, buf0, w, vec(b), vec(g), vec(be))


def _mla_common(zb_ref, tab, qn_ref, kvn_ref, wq1_ref, wq2_ref, rows_ref, q_scale):
    zb = zb_ref[...].reshape(-1, Z_BW)
    bq = zb[:, 0:B_Q_RANK]
    bkv = zb[:, B_Q_RANK:B_Q_RANK + B_KV_RANK]
    misc = zb[:, B_Q_RANK + B_KV_RANK:]
    qn = (_rms_rows(bq) * qn_ref[...]).astype(BF16)
    ckv = _rms_rows(bkv) * kvn_ref[...]
    cos_q = jnp.concatenate([tab[:, 0:LANE]] * B_HEADS, axis=1)
    sin_q = jnp.concatenate([tab[:, LANE:2 * LANE]] * B_HEADS, axis=1)
    q = (_dot(qn, wq1_ref[...]) * cos_q + _dot(qn, wq2_ref[...]) * sin_q) * q_scale
    kpe = misc * tab[:, 2 * LANE:3 * LANE] + pltpu.roll(misc, LANE // 2, 1) * tab[:, 3 * LANE:]
    lead = rows_ref.shape[:-1]
    rows_ref[:, :, 0:B_KV_RANK] = ckv.reshape(lead + (B_KV_RANK,))
    rows_ref[:, :, B_KV_RANK:CACHE_W] = kpe[:, 0:B_ROPE].reshape(lead + (B_ROPE,))
    return q, ckv, kpe


def _mla_prep_prompt_kernel(zb_ref, tab_ref, qn_ref, kvn_ref, wq1_ref, wq2_ref, wk_ref, wv_ref,
                            q_ref, k_ref, v_ref, rows_ref):
    q, ckv, kpe = _mla_common(zb_ref, tab_ref[...], qn_ref, kvn_ref, wq1_ref, wq2_ref, rows_ref,
                              MLA_SCALE * LOG2E)
    q_ref[0] = q.astype(BF16)
    cb = ckv.astype(BF16)
    kpe_hi = pltpu.roll(kpe, LANE // 2, 1)
    k = _dot(cb, wk_ref[...]) + jnp.concatenate([kpe_hi] * B_HEADS, axis=1)
    k_ref[0] = k.astype(BF16)
    lane = lax.broadcasted_iota(jnp.int32, (1, B_HEADS * HP), 1)
    ones_col = jnp.where(lane % HP == B_VDIM, 1.0, 0.0)
    v_ref[0] = (_dot(cb, wv_ref[...]) + ones_col).astype(BF16)


def _mla_prep_prompt(z3, tab, qn, kvn, wq1, wq2, wk, wv):
    B, T, _ = z3.shape
    tt = _tile(T, 512)
    nt = T // tt
    HW = B_HEADS * HP
    full = lambda a: pl.BlockSpec(a.shape, lambda b, t: (0,) * a.ndim)
    big = pl.BlockSpec((1, tt, HW), lambda b, t: (b, t, 0))
    return pl.pallas_call(
        _mla_prep_prompt_kernel,
        out_shape=(jax.ShapeDtypeStruct((B, T, HW), BF16),) * 3
        + (jax.ShapeDtypeStruct((B, T, CACHE_W), F32),),
        grid=(B, nt),
        in_specs=[pl.BlockSpec((1, tt, Z_BW), lambda b, t: (b, t, 0)),
                  pl.BlockSpec((tt, 4 * LANE), lambda b, t: (t, 0)),
                  full(qn), full(kvn), full(wq1), full(wq2), full(wk), full(wv)],
        out_specs=(big, big, big, pl.BlockSpec((1, tt, CACHE_W), lambda b, t: (b, t, 0))),
        compiler_params=_cp(("parallel", "parallel")),
        name="mla_prep_prompt",
    )(z3, tab, qn, kvn, wq1, wq2, wk, wv)


def _flash_kernel(q_ref, k_ref, v_ref, o_ref, *, tq, nq):
    row = lax.broadcasted_iota(jnp.int32, (tq, tq), 0)
    col = lax.broadcasted_iota(jnp.int32, (tq, tq), 1)
    causal = col <= row
    for qi in range(nq):
        q = q_ref[0, qi * tq:(qi + 1) * tq, :]
        m = acc = None
        for kj in range(qi + 1):
            k = k_ref[0, kj * tq:(kj + 1) * tq, :]
            v = v_ref[0, kj * tq:(kj + 1) * tq, :]
            s = _dot_nt(q, k)
            if kj == qi:
                s = jnp.where(causal, s, -jnp.inf)
            s_max = jnp.max(s, axis=1, keepdims=True)
            if kj == 0:
                m = s_max
                acc = _dot(jnp.exp2(s - m).astype(BF16), v)
            else:
                m_new = jnp.maximum(m, s_max)
                acc = jnp.exp2(m - m_new) * acc + _dot(jnp.exp2(s - m_new).astype(BF16), v)
                m = m_new
        o_ref[0, qi * tq:(qi + 1) * tq, :] = (acc / acc[:, B_VDIM:B_VDIM + 1]).astype(o_ref.dtype)


def _flash(q, k, v):
    B, T, HW = q.shape
    tq = _tile(T, 512)
    spec = pl.BlockSpec((1, T, HP), lambda b, h: (b, 0, h))
    return pl.pallas_call(
        functools.partial(_flash_kernel, tq=tq, nq=T // tq),
        out_shape=jax.ShapeDtypeStruct((B, T, HW), BF16),
        grid=(B, B_HEADS),
        in_specs=[spec, spec, spec],
        out_specs=spec,
        compiler_params=_cp(("parallel", "parallel")),
        name="mla_prompt_attention",
    )(q, k, v)


def _mla_prep_sample_kernel(zb_ref, tab_ref, qn_ref, kvn_ref, wq1_ref, wq2_ref, wuk_ref,
                            q_ref, rows_ref):
    bb, _, tt, _ = q_ref.shape
    tab = jnp.broadcast_to(tab_ref[...], (bb, tt, 4 * LANE)).reshape(bb * tt, 4 * LANE)
    q, _, _ = _mla_common(zb_ref, tab, qn_ref, kvn_ref, wq1_ref, wq2_ref, rows_ref, MLA_SCALE)
    lane = lax.broadcasted_iota(jnp.int32, (1, HP), 1)
    for h in range(B_HEADS):
        qh = q[:, h * HP:(h + 1) * HP]
        q_abs = _dot(qh.astype(BF16), wuk_ref[h])
        q_pe = jnp.where(lane < B_ROPE, pltpu.roll(qh, LANE // 2, 1), 0.0)
        q_ref[:, h, :, 0:B_KV_RANK] = q_abs.reshape(bb, tt, B_KV_RANK)
        q_ref[:, h, :, B_KV_RANK:] = q_pe[:, 0:B_ROPE].reshape(bb, tt, B_ROPE)


def _mla_prep_sample(z3, tab, qn, kvn, wq1, wq2, wuk):
    B, T, _ = z3.shape
    bb = _tile(B, max(1, 512 // T), 1)
    full = lambda a: pl.BlockSpec(a.shape, lambda i: (0,) * a.ndim)
    return pl.pallas_call(
        _mla_prep_sample_kernel,
        out_shape=(jax.ShapeDtypeStruct((B, B_HEADS, T, CACHE_W), F32),
                   jax.ShapeDtypeStruct((B, T, CACHE_W), F32)),
        grid=(B // bb,),
        in_specs=[pl.BlockSpec((bb, T, Z_BW), lambda i: (i, 0, 0)),
                  pl.BlockSpec((1, T, 4 * LANE), lambda i: (0, 0, 0)),
                  full(qn), full(kvn), full(wq1), full(wq2), full(wuk)],
        out_specs=(pl.BlockSpec((bb, B_HEADS, T, CACHE_W), lambda i: (i, 0, 0, 0)),
                   pl.BlockSpec((bb, T, CACHE_W), lambda i: (i, 0, 0))),
        compiler_params=_cp(("parallel",)),
        name="mla_prep_sample",
    )(z3, tab, qn, kvn, wq1, wq2, wuk)


def _sattn_kernel(pt_ref, q_ref, new_ref, wv_ref, cache_ref, y_ref, kbuf, sem, *, layer, n_pages, n_rows, T):
    b = pl.program_id(0)
    R = B_HEADS * T

    def page_copy(row, g, slot):
        return pltpu.make_async_copy(cache_ref.at[layer, pt_ref[row, g]], kbuf.at[slot, g], sem.at[slot])

    def fetch(row, slot):
        for g in range(n_pages):
            page_copy(row, g, slot).start()

    ahead = kbuf.shape[0] - 1

    @pl.when(b == 0)
    def _():
        for row in range(min(ahead, n_rows)):
            fetch(row, row)

    @pl.when(b + ahead < n_rows)
    def _():
        fetch(b + ahead, (b + ahead) % (ahead + 1))

    slot = b % (ahead + 1)
    for g in range(n_pages):
        page_copy(b, g, slot).wait()

    q = q_ref[0].reshape(R, CACHE_W).astype(BF16)
    kt = jnp.concatenate([kbuf[slot, g].astype(BF16) for g in range(n_pages)], axis=1)
    pad = jnp.zeros((PAGE_SIZE - T, CACHE_W), F32)
    new = jnp.concatenate([new_ref[0], pad], axis=0).astype(BF16)
    r = lax.broadcasted_iota(jnp.int32, (R, PAGE_SIZE), 0)
    c = lax.broadcasted_iota(jnp.int32, (R, PAGE_SIZE), 1)
    s_past = _dot(q, kt)
    s_new = jnp.where(c <= r % T, _dot_nt(q, new), -jnp.inf)
    m = jnp.maximum(jnp.max(s_past, axis=1, keepdims=True), jnp.max(s_new, axis=1, keepdims=True))
    p_past = jnp.exp(s_past - m)
    p_new = jnp.exp(s_new - m)
    l = jnp.sum(p_past, axis=1, keepdims=True) + jnp.sum(p_new, axis=1, keepdims=True)
    acc = _dot_nt(p_past.astype(BF16), kt[0:B_KV_RANK, :]) + _dot(p_new.astype(BF16), new[:, 0:B_KV_RANK])
    o = acc / l
    for h in range(B_HEADS):
        oh = o[h * T:(h + 1) * T, :].astype(BF16)
        y_ref[0, :, h * HP:(h + 1) * HP] = _dot(oh, wv_ref[h]).astype(y_ref.dtype)


def _sample_attention(layer, q, new_rows, wv, cache_t, page_table, ydt):
    B, H, T, _ = q.shape
    n_pages = page_table.shape[1]
    grid_spec = pltpu.PrefetchScalarGridSpec(
        num_scalar_prefetch=1,
        grid=(B,),
        in_specs=[pl.BlockSpec((1, H, T, CACHE_W), lambda b, pt: (b, 0, 0, 0)),
                  pl.BlockSpec((1, T, CACHE_W), lambda b, pt: (b, 0, 0)),
                  pl.BlockSpec(wv.shape, lambda b, pt: (0, 0, 0)),
                  pl.BlockSpec(memory_space=pl.ANY)],
        out_specs=pl.BlockSpec((1, T, H * HP), lambda b, pt: (b, 0, 0)),
        scratch_shapes=[pltpu.VMEM((GATHER_SLOTS, n_pages, CACHE_W, PAGE_SIZE), F32),
                        pltpu.SemaphoreType.DMA((GATHER_SLOTS,))],
    )
    return pl.pallas_call(
        functools.partial(_sattn_kernel, layer=layer, n_pages=n_pages, n_rows=B, T=T),
        out_shape=jax.ShapeDtypeStruct((B, T, H * HP), ydt),
        grid_spec=grid_spec,
        compiler_params=_cp(("arbitrary",)),
        name="mla_sample_attention",
    )(page_table, q, new_rows, wv, cache_t)


def _merge_kernel(ya_ref, yb_ref, yc_ref, yd_ref, gt_ref, x_ref, g1_ref, wa_ref, wb_ref, wc_ref, wd_ref,
                  wo_ref, lg_ref, lb_ref, o_ref):
    rows = o_ref.shape[0] * o_ref.shape[1]

    def flat(ref):
        return ref[...].reshape(rows, ref.shape[-1])

    gates = flat(gt_ref).astype(F32)
    acc = None
    for n, (y_ref, w_ref) in enumerate(((ya_ref, wa_ref), (yb_ref, wb_ref), (yc_ref, wc_ref), (yd_ref, wd_ref))):
        term = _sigmoid(gates[:, n * D_MODEL:(n + 1) * D_MODEL]) * _dot(flat(y_ref).astype(BF16), w_ref[...])
        acc = term if acc is None else acc + term
    mix = _dot(acc.astype(BF16), wo_ref[...]).reshape(o_ref.shape)
    v = DEEPNORM_ALPHA * x_ref[...] + g1_ref[...] * mix
    o_ref[...] = _ln_rows(v) * lg_ref[...] + lb_ref[...]


def _merge(ya, yb, yc, yd, z3, x, g1, wa, wb, wc, wd, wo, lg, lb):
    B, T, D = x.shape
    bb, tt = _token_blocks(B, T, 512)
    nt = T // tt
    tok = lambda w: pl.BlockSpec((bb, tt, w), lambda i: (i // nt, i % nt, 0))
    full = lambda a: pl.BlockSpec(a.shape, lambda i: (0,) * a.ndim)
    vec = lambda a: a.reshape(1, 1, D)
    return pl.pallas_call(
        _merge_kernel,
        out_shape=jax.ShapeDtypeStruct((B, T, D), F32),
        grid=((B // bb) * nt,),
        in_specs=[tok(ya.shape[-1]), tok(yb.shape[-1]), tok(yc.shape[-1]), tok(yd.shape[-1]),
                  pl.BlockSpec((bb, tt, N_BRANCH * D), lambda i: (i // nt, i % nt, Z_G // (N_BRANCH * D_MODEL))),
                  tok(D),
                  pl.BlockSpec((bb, 1, D), lambda i: (i // nt, 0, 0)),
                  full(wa), full(wb), full(wc), full(wd), full(wo),
                  pl.BlockSpec((1, 1, D), lambda i: (0, 0, 0)), pl.BlockSpec((1, 1, D), lambda i: (0, 0, 0))],
        out_specs=tok(D),
        compiler_params=_cp(("parallel",)),
        name="merge_out_ln",
    )(ya, yb, yc, yd, z3, x, g1, wa, wb, wc, wd, wo, vec(lg), vec(lb))


def _ffn_kernel(x_ref, sc_ref, sh_ref, g2_ref, w1_ref, w3_ref, w2_ref, lg_ref, lb_ref, o_ref, u_s, acc_s):
    j = pl.program_id(1)

    @pl.when(j == 0)
    def _():
        u = x_ref[...] * (1.0 + sc_ref[...]) + sh_ref[...]
        u_s[...] = u.reshape(u_s.shape).astype(BF16)
        acc_s[...] = jnp.zeros(acc_s.shape, F32)

    u = u_s[...]
    hid = _silu(_dot(u, w1_ref[...])) * _dot(u, w3_ref[...])
    acc_s[...] += _dot(hid.astype(BF16), w2_ref[...])

    @pl.when(j == pl.num_programs(1) - 1)
    def _():
        v = DEEPNORM_ALPHA * x_ref[...] + g2_ref[...] * acc_s[...].reshape(o_ref.shape)
        o_ref[...] = _ln_rows(v) * lg_ref[...] + lb_ref[...]


def _ffn(x, sc, sh, g2, w1, w3, w2, lg, lb):
    B, T, D = x.shape
    Hd = w1.shape[1]
    bb, tt = _token_blocks(B, T, 512)
    nt = T // tt
    tm = bb * tt
    th = _tile(Hd, 1408, LANE)
    tok = pl.BlockSpec((bb, tt, D), lambda i, j: (i // nt, i % nt, 0))
    per_b = pl.BlockSpec((bb, 1, D), lambda i, j: (i // nt, 0, 0))
    vspec = pl.BlockSpec((1, 1, D), lambda i, j: (0, 0, 0))
    vec = lambda a: a.reshape(1, 1, D)
    return pl.pallas_call(
        _ffn_kernel,
        out_shape=jax.ShapeDtypeStruct((B, T, D), F32),
        grid=((B // bb) * nt, Hd // th),
        in_specs=[tok, per_b, per_b, per_b,
                  pl.BlockSpec((D, th), lambda i, j: (0, j)),
                  pl.BlockSpec((D, th), lambda i, j: (0, j)),
                  pl.BlockSpec((th, D), lambda i, j: (j, 0)),
                  vspec, vspec],
        out_specs=tok,
        scratch_shapes=[pltpu.VMEM((tm, D), BF16), pltpu.VMEM((tm, D), F32)],
        compiler_params=_cp(("parallel", "arbitrary")),
        name="ffn_ln",
    )(x, sc, sh, g2, w1, w3, w2, vec(lg), vec(lb))


def _rope_tables(pos):
    posf = pos.astype(F32)[:, None]
    T = pos.shape[0]

    def cs(d):
        inv = ROPE_BASE ** (-jnp.arange(0, d, 2, dtype=F32) / d)
        ang = posf * inv[None, :]
        c, s = jnp.cos(ang), jnp.sin(ang)
        return jnp.concatenate([c, c], axis=1), jnp.concatenate([-s, s], axis=1)

    c32, s32 = cs(B_ROPE)
    z = lambda w: jnp.zeros((T, w), F32)
    cos_q = jnp.concatenate([jnp.ones((T, B_NOPE), F32), c32, z(HP - B_NOPE - B_ROPE)], axis=1)
    sin_q = jnp.concatenate([z(B_NOPE), s32, z(HP - B_NOPE - B_ROPE)], axis=1)
    cos_k = jnp.concatenate([c32, z(LANE - B_ROPE)], axis=1)
    sin_k = jnp.concatenate([s32, z(LANE - B_ROPE)], axis=1)
    tab_b = jnp.concatenate([cos_q, sin_q, cos_k, sin_k], axis=1)
    c128, s128 = cs(D_DK)
    return tab_b, jnp.concatenate([c128, s128], axis=1)


def _swap_halves(a, lo, width):
    half = width // 2
    return jnp.concatenate([a[..., lo + half:lo + width], a[..., lo:lo + half]], axis=-1)


def _layer_weights(l, w_in, b_in, mla_w_uq, mla_w_uk, mla_w_uv, w_branch):
    def main_cols(a):
        return jnp.concatenate([a[..., O_G:N_IN], a[..., O_AQ:O_AI], a[..., O_D:O_G], a[..., O_C:O_D]], axis=-1)

    def misc_cols(a):
        zeros = lambda w: jnp.zeros(a.shape[:-1] + (w,), a.dtype)
        return jnp.concatenate([
            a[..., O_BQ:O_BKR + B_ROPE], a[..., O_AI:O_BQ], zeros(MISC_KRSW - MISC_AF - A_HEADS),
            _swap_halves(a, O_BKR, B_ROPE), zeros(LANE - MISC_KRSW - B_ROPE)], axis=-1)

    w_main, w_misc = main_cols(w_in[l]).astype(BF16), misc_cols(w_in[l]).astype(BF16)
    b_main, b_misc = main_cols(b_in[l]).reshape(1, Z_MAIN), misc_cols(b_in[l]).reshape(1, Z_BW)

    hd = B_NOPE + B_ROPE
    wq = mla_w_uq[l].reshape(B_Q_RANK, B_HEADS, hd)
    zq = jnp.zeros((B_Q_RANK, B_HEADS, HP - hd), F32)
    wq1 = jnp.concatenate([wq, zq], axis=-1).reshape(B_Q_RANK, B_HEADS * HP).astype(BF16)
    wq2 = jnp.concatenate([jnp.zeros((B_Q_RANK, B_HEADS, B_NOPE), F32), _swap_halves(wq, B_NOPE, B_ROPE), zq],
                          axis=-1).reshape(B_Q_RANK, B_HEADS * HP).astype(BF16)
    wuk = mla_w_uk[l]
    wuv = mla_w_uv[l]
    wk = jnp.concatenate([wuk, jnp.zeros((B_KV_RANK, B_HEADS, HP - B_NOPE), F32)], axis=-1)
    wv = jnp.concatenate([wuv, jnp.zeros((B_KV_RANK, B_HEADS, HP - B_VDIM), F32)], axis=-1)
    wk_flat = wk.reshape(B_KV_RANK, B_HEADS * HP).astype(BF16)
    wv_flat = wv.reshape(B_KV_RANK, B_HEADS * HP).astype(BF16)
    wuk_t = jnp.transpose(wk, (1, 2, 0)).astype(BF16)
    wv_h = jnp.transpose(wv, (1, 0, 2)).astype(BF16)
    wb = w_branch[l]
    wb_b = jnp.concatenate([wb[1].reshape(B_HEADS, B_VDIM, D_MODEL),
                            jnp.zeros((B_HEADS, HP - B_VDIM, D_MODEL), F32)], axis=1)
    wb_b = wb_b.reshape(B_HEADS * HP, D_MODEL).astype(BF16)
    return dict(w_main=w_main, b_main=b_main, w_misc=w_misc, b_misc=b_misc, wq1=wq1, wq2=wq2, wk=wk_flat, wv=wv_flat, wuk_t=wuk_t, wv_h=wv_h,
                wb_a=wb[0].astype(BF16), wb_b=wb_b, wb_c=wb[2].astype(BF16), wb_d=wb[3].astype(BF16))


def kernel(x_prompt, x_sample, cache_mla, state_mlstm_C, state_mlstm_n, state_mlstm_m, state_conv, state_ret,
           page_table, c_prompt, c_sample, w_ada, b_ada, w_in, b_in, mla_q_norm, mla_w_uq, mla_kv_norm,
           mla_w_uk, mla_w_uv, conv_w, conv_b, conv_ln_g, conv_ln_b, w_branch, w_out, ln1_g, ln1_b,
           w_ffn1, w_ffn3, w_ffn2, ln2_g, ln2_b):
    dt = x_prompt.dtype
    Bp, Tp, D = x_prompt.shape
    Bd, Td, _ = x_sample.shape
    past_len = page_table.shape[1] * PAGE_SIZE
    cache_t = jnp.swapaxes(cache_mla, 2, 3)
    tabs_p = _rope_tables(jnp.arange(Tp))
    tabs_d = _rope_tables(past_len + jnp.arange(Td))
    c_all = jnp.concatenate([c_prompt, c_sample], axis=0).astype(F32)

    xp, xd = x_prompt.astype(F32), x_sample.astype(F32)
    p_states = [[] for _ in range(6)]
    d_states = [[] for _ in range(6)]
    for l in range(DEPTH):
        lw = _layer_weights(l, w_in, b_in, mla_w_uq, mla_w_uk, mla_w_uv, w_branch)
        qn = mla_q_norm[l].reshape(1, B_Q_RANK)
        kvn = mla_kv_norm[l].reshape(1, B_KV_RANK)
        wo = w_out[l].astype(BF16)
        w1, w3, w2 = w_ffn1[l].astype(BF16), w_ffn3[l].astype(BF16), w_ffn2[l].astype(BF16)
        mod = _ada_mod(c_all, w_ada.astype(F32), l, b_ada[l])

        def group(x, mod_g, tabs, st, sample):
            B, T, _ = x.shape
            sh1, sc1, g1, sh2, sc2, g2 = [mod_g[:, None, i * D:(i + 1) * D] for i in range(6)]
            ydt = F32 if sample else BF16
            z3, zb = _in_proj(x, sc1, sh1, lw['w_main'], lw['b_main'], lw['w_misc'], lw['b_misc'], ydt)
            (c0, c_layer), n0, m0, buf0, (s0, s_layer) = st
            ya, c1, n1, m1 = _mlstm(z3, zb, c0, c_layer, n0, m0, ydt)
            yd, s1 = _retention(z3, tabs[1], s0, s_layer, ydt)
            yc, buf1 = _conv(z3, buf0, conv_w[l], conv_b[l], conv_ln_g[l], conv_ln_b[l], ydt)
            if sample:
                q, rows = _mla_prep_sample(zb, tabs[0].reshape(1, T, 4 * LANE), qn, kvn,
                                           lw['wq1'], lw['wq2'], lw['wuk_t'])
                yb = _sample_attention(l, q, rows, lw['wv_h'], cache_t, page_table, ydt)
            else:
                q, k, v, rows = _mla_prep_prompt(zb, tabs[0], qn, kvn, lw['wq1'], lw['wq2'], lw['wk'], lw['wv'])
                yb = _flash(q, k, v)
            x1 = _merge(ya, yb, yc, yd, z3, x, g1, lw['wb_a'], lw['wb_b'], lw['wb_c'], lw['wb_d'], wo,
                        ln1_g[l], ln1_b[l])
            x2 = _ffn(x1, sc2, sh2, g2, w1, w3, w2, ln2_g[l], ln2_b[l])
            return x2, (rows, c1, n1, m1, buf1, s1)

        st_p = ((jnp.zeros((1, Bp, A_HEADS, A_DK, A_DV), F32), 0), jnp.zeros((Bp, A_HEADS, A_DK), F32),
                jnp.zeros((Bp, A_HEADS), F32), jnp.zeros((Bp, C_WIDTH - 1, C_CH), F32),
                (jnp.zeros((1, Bp, D_HEADS, D_DK, D_DV), F32), 0))
        xp, new_p = group(xp, mod[:Bp], tabs_p, st_p, False)
        st_d = ((state_mlstm_C.astype(F32), l), state_mlstm_n[l].astype(F32), state_mlstm_m[l].astype(F32),
                state_conv[l].astype(F32), (state_ret.astype(F32), l))
        xd, new_d = group(xd, mod[Bp:], tabs_d, st_d, True)
        for i in range(6):
            p_states[i].append(new_p[i])
            d_states[i].append(new_d[i])
    ps = [jnp.stack(s, axis=0).astype(dt) for s in p_states]
    ds = [jnp.stack(s, axis=0).astype(dt) for s in d_states]
    return (xp.astype(dt), xd.astype(dt), ps[0], ds[0], ps[1], ds[1], ps[2], ds[2], ps[3], ds[3],
            ps[4], ds[4], ps[5], ds[5])
```

```python
import functools

import numpy as np
import jax
import jax.numpy as jnp
from jax import lax
from jax.experimental import pallas as pl
from jax.experimental.pallas import tpu as pltpu

F32 = jnp.float32
BF16 = jnp.bfloat16

D_MODEL = 1024
DEPTH = 2
PAGE_SIZE = 128
N_BRANCH = 4
A_HEADS, A_DK, A_DV = 4, 128, 128
B_HEADS, B_Q_RANK, B_KV_RANK, B_NOPE, B_ROPE, B_VDIM = 8, 384, 256, 64, 32, 64
C_CH, C_WIDTH = 512, 31
D_HEADS, D_DK, D_DV = 4, 128, 128
FFN_HIDDEN = -(-8 * D_MODEL // (3 * 256)) * 256
CHUNK = 128
ROPE_BASE = 10000.0
LN_EPS = 1e-5
RMS_EPS = 1e-6
DEEPNORM_ALPHA = (2 * DEPTH) ** 0.25
CACHE_W = B_KV_RANK + B_ROPE
MLA_SCALE = (B_NOPE + B_ROPE) ** -0.5
LOG2E = float(np.log2(np.e))
GATHER_SLOTS = 3
O_AQ, O_AI, O_AF = 0, 2048, 2052
O_BQ, O_BKV, O_BKR = 2056, 2440, 2696
O_C, O_D, O_G = 2728, 3752, 5800
N_IN = 9896

Z_G, Z_A, Z_D, Z_C = 0, 4096, 6144, 8192
Z_MAIN = 9216
Z_BW = 768
MISC_KR, MISC_AI, MISC_AF, MISC_KRSW = 0, 32, 36, 64
LANE = 128
HP = 128

VMEM_LIMIT = 56 * 1024 * 1024


def _cp(sem, vmem=VMEM_LIMIT):
    return pltpu.CompilerParams(dimension_semantics=sem, vmem_limit_bytes=vmem)


def _sigmoid(x):
    return 1.0 / (1.0 + jnp.exp(-x))


def _silu(x):
    return x * _sigmoid(x)


def _log_sigmoid(x):
    return jnp.minimum(x, 0.0) - jnp.log(1.0 + jnp.exp(-jnp.abs(x)))


def _ln_rows(x):
    mu = jnp.mean(x, axis=-1, keepdims=True)
    xc = x - mu
    var = jnp.mean(xc * xc, axis=-1, keepdims=True)
    return xc * lax.rsqrt(var + LN_EPS)


def _rms_rows(x):
    return x * lax.rsqrt(jnp.mean(x * x, axis=-1, keepdims=True) + RMS_EPS)


def _dot(a, b):
    return jnp.dot(a, b, preferred_element_type=F32)


def _dot_nt(a, b):
    return lax.dot_general(a, b, (((1,), (1,)), ((), ())), preferred_element_type=F32)


def _bdot(a, b):
    return lax.dot_general(a, b, (((2,), (1,)), ((0,), (0,))), preferred_element_type=F32)


def _bdot_nt(a, b):
    return lax.dot_general(a, b, (((2,), (2,)), ((0,), (0,))), preferred_element_type=F32)


def _bdot_tn(a, b):
    return lax.dot_general(a, b, (((1,), (1,)), ((0,), (0,))), preferred_element_type=F32)


def _tile(n, pref, align=8):
    if n <= pref:
        return n
    for t in range(pref, 0, -1):
        if n % t == 0 and t % align == 0:
            return t
    return n


def _mod_kernel(c_ref, w_ref, b_ref, o_ref):
    s = _silu(c_ref[...])
    o_ref[...] = _dot(s.astype(BF16), w_ref[...].astype(BF16)) + b_ref[...]


def _ada_mod(c, w_all, layer, b):
    m, d = c.shape
    n = w_all.shape[2]
    tn = _tile(n, 768, LANE)
    w = w_all
    return pl.pallas_call(
        _mod_kernel,
        out_shape=jax.ShapeDtypeStruct((m, n), F32),
        grid=(n // tn,),
        in_specs=[pl.BlockSpec((m, d), lambda j: (0, 0)),
                  pl.BlockSpec((None, d, tn), lambda j: (layer, 0, j)),
                  pl.BlockSpec((1, tn), lambda j: (0, j))],
        out_specs=pl.BlockSpec((m, tn), lambda j: (0, j)),
        compiler_params=_cp(("arbitrary",)),
        name="ada_mod",
    )(c, w, b.reshape(1, n))


def _inproj_kernel(x_ref, sc_ref, sh_ref, wm_ref, bm_ref, wz_ref, bz_ref, om_ref, oz_ref, u_ref, *, n_main):
    j = pl.program_id(1)

    @pl.when(j == 0)
    def _():
        u = x_ref[...] * (1.0 + sc_ref[...]) + sh_ref[...]
        u_ref[...] = u.reshape(u_ref.shape).astype(BF16)

    @pl.when(j < n_main)
    def _():
        om_ref[...] = (_dot(u_ref[...], wm_ref[...]) + bm_ref[...]).astype(om_ref.dtype)

    @pl.when(j == n_main)
    def _():
        oz_ref[...] = _dot(u_ref[...], wz_ref[...]) + bz_ref[...]


def _token_blocks(B, T, rows):
    if T >= rows:
        return 1, _tile(T, rows)
    return _tile(B, max(1, rows // T), 1), T


def _in_proj(x, sc, sh, wm, bm, wz, bz, main_dtype):
    B, T, D = x.shape
    bb, tt = _token_blocks(B, T, 1024)
    tm = bb * tt
    nt = T // tt
    tn = _tile(Z_MAIN, 1536, LANE)
    n_main = Z_MAIN // tn
    last = n_main - 1
    main, misc = pl.pallas_call(
        functools.partial(_inproj_kernel, n_main=n_main),
        out_shape=(jax.ShapeDtypeStruct((B * T, Z_MAIN), main_dtype),
                   jax.ShapeDtypeStruct((B * T, Z_BW), F32)),
        grid=((B // bb) * nt, n_main + 1),
        in_specs=[pl.BlockSpec((bb, tt, D), lambda i, j: (i // nt, i % nt, 0)),
                  pl.BlockSpec((bb, 1, D), lambda i, j: (i // nt, 0, 0)),
                  pl.BlockSpec((bb, 1, D), lambda i, j: (i // nt, 0, 0)),
                  pl.BlockSpec((D, tn), lambda i, j: (0, jnp.minimum(j, last))),
                  pl.BlockSpec((1, tn), lambda i, j: (0, jnp.minimum(j, last))),
                  pl.BlockSpec((D, Z_BW), lambda i, j: (0, 0)),
                  pl.BlockSpec((1, Z_BW), lambda i, j: (0, 0))],
        out_specs=(pl.BlockSpec((tm, tn), lambda i, j: (i, jnp.minimum(j, last))),
                   pl.BlockSpec((tm, Z_BW), lambda i, j: (i, 0))),
        scratch_shapes=[pltpu.VMEM((tm, D), BF16)],
        compiler_params=_cp(("parallel", "arbitrary")),
        name="in_proj",
    )(x, sc, sh, wm, bm, wz, bz)
    return main.reshape(B, T, Z_MAIN), misc.reshape(B, T, Z_BW)


def _stack_heads(ref, n_heads, width, lo=0):
    return jnp.concatenate([ref[:, :, lo + h * width:lo + (h + 1) * width].astype(F32) for h in range(n_heads)],
                           axis=0)


def _mlstm_cells(q, k, v, og, ig_col, af_col, c_prev, n_prev, m_prev, L, mxu):
    row = lax.broadcasted_iota(jnp.int32, (L, L), 0)
    col = lax.broadcasted_iota(jnp.int32, (L, L), 1)
    tril = col <= row
    eye = col == row
    lf_col = _log_sigmoid(af_col)
    ig_row = jnp.sum(jnp.where(eye, ig_col, 0.0), axis=1, keepdims=True)
    b_row = jnp.sum(jnp.where(row <= col, lf_col, 0.0), axis=1, keepdims=True)
    b_col = jnp.sum(jnp.where(eye, b_row, 0.0), axis=2, keepdims=True)
    inter = b_col + m_prev
    intra = jnp.where(tril, b_col - b_row + ig_row, -jnp.inf)
    m_t = jnp.maximum(inter, jnp.max(intra, axis=2, keepdims=True))
    w_inter = jnp.exp(inter - m_t)
    dmat = jnp.exp(intra - m_t)
    qh = q * (A_DK ** -0.5)
    qb, kb, vb = qh.astype(mxu), k.astype(mxu), v.astype(mxu)
    s = _bdot_nt(qb, kb) * dmat
    num = _bdot(s.astype(mxu), vb) + w_inter * _bdot(qb, c_prev.astype(mxu))
    den = jnp.sum(s, axis=2, keepdims=True) + w_inter * jnp.sum(qh * n_prev, axis=2, keepdims=True)
    hv = num / jnp.maximum(jnp.abs(den), jnp.exp(-m_t))
    y = _sigmoid(og) * _ln_rows(hv)
    b_last = b_col[:, L - 1:L, :]
    m_new = m_t[:, L - 1:L, :]
    w_c = jnp.exp(b_last + m_prev - m_new)
    w_s = jnp.exp(b_last - b_col + ig_col - m_new)
    kw = k * w_s
    c_new = w_c * c_prev + _bdot_tn(kw.astype(mxu), vb)
    n_new = w_c * n_prev + jnp.sum(kw, axis=1, keepdims=True)
    return y, c_new, n_new, m_new


def _mlstm_kernel(q_ref, k_ref, v_ref, o_ref, g_ref, c0_ref, n0_ref, m0_ref,
                  y_ref, c_out, n_out, m_out, c_s, n_s, m_s, *, L, mxu, bb, carry):
    H = A_HEADS
    if carry:
        @pl.when(pl.program_id(1) == 0)
        def _():
            c_s[...] = c0_ref[0]
            n_s[...] = n0_ref[0]
            m_s[...] = m0_ref[0]
        c_prev, n_prev, m_prev = c_s[...], n_s[...], m_s[:, :, 0:1]
    else:
        c_prev = jnp.concatenate([c0_ref[:, h] for h in range(H)], axis=0)
        n_prev = jnp.concatenate([n0_ref[:, h] for h in range(H)], axis=0)
        m_prev = jnp.concatenate([m0_ref[:, h, :, 0:1] for h in range(H)], axis=0)

    y, c_new, n_new, m_new = _mlstm_cells(
        _stack_heads(q_ref, H, A_DK), _stack_heads(k_ref, H, A_DK), _stack_heads(v_ref, H, A_DV),
        _stack_heads(o_ref, H, A_DV), _stack_heads(g_ref, H, 1, MISC_AI), _stack_heads(g_ref, H, 1, MISC_AF),
        c_prev, n_prev, m_prev, L, mxu)
    m_new = jnp.broadcast_to(m_new, (H * bb, 1, LANE))
    for h in range(H):
        rows = slice(h * bb, (h + 1) * bb)
        y_ref[:, :, h * A_DV:(h + 1) * A_DV] = y[rows].astype(y_ref.dtype)
        if not carry:
            c_out[:, h] = c_new[rows]
            n_out[:, h] = n_new[rows]
            m_out[:, h] = m_new[rows]
    if carry:
        c_s[...] = c_new
        n_s[...] = n_new
        m_s[...] = m_new

        @pl.when(pl.program_id(1) == pl.num_programs(1) - 1)
        def _():
            c_out[0] = c_new
            n_out[0] = n_new
            m_out[0] = m_new


SMALL_SEQ_BATCH = 8


def _mlstm(z3, zb, c0_all, layer, n0, m0, ydt):
    B, T, _ = z3.shape
    L = min(CHUNK, T)
    nc = T // L
    carry = nc > 1
    bb = 1 if carry else _tile(B, SMALL_SEQ_BATCH, 1)
    H = A_HEADS
    W = H * A_DK
    ab = Z_A // W
    n0r = n0.reshape(B, H, 1, A_DK)
    m0b = jnp.broadcast_to(m0[..., None, None], (B, H, 1, LANE))
    mxu = BF16 if L >= 16 else F32

    def zspec(k):
        return pl.BlockSpec((bb, L, W), lambda b, c: (b, c, ab + k))

    st_specs = (pl.BlockSpec((bb, H, A_DK, A_DV), lambda b, c: (b, 0, 0, 0)),
                pl.BlockSpec((bb, H, 1, A_DK), lambda b, c: (b, 0, 0, 0)),
                pl.BlockSpec((bb, H, 1, LANE), lambda b, c: (b, 0, 0, 0)))
    y, c1, n1, m1 = pl.pallas_call(
        functools.partial(_mlstm_kernel, L=L, mxu=mxu, bb=bb, carry=carry),
        out_shape=(jax.ShapeDtypeStruct((B, T, W), ydt),
                   jax.ShapeDtypeStruct((B, H, A_DK, A_DV), F32),
                   jax.ShapeDtypeStruct((B, H, 1, A_DK), F32),
                   jax.ShapeDtypeStruct((B, H, 1, LANE), F32)),
        grid=(B // bb, nc),
        in_specs=[zspec(0), zspec(1), zspec(2), zspec(3),
                  pl.BlockSpec((bb, L, LANE), lambda b, c: (b, c, (Z_BW - LANE) // LANE)),
                  pl.BlockSpec((None, bb, H, A_DK, A_DV), lambda b, c: (layer, b, 0, 0, 0))] + list(st_specs[1:]),
        out_specs=(pl.BlockSpec((bb, L, W), lambda b, c: (b, c, 0)),) + st_specs,
        scratch_shapes=[pltpu.VMEM((H, A_DK, A_DV), F32), pltpu.VMEM((H, 1, A_DK), F32),
                        pltpu.VMEM((H, 1, LANE), F32)],
        compiler_params=_cp(("parallel", "arbitrary")),
        name="mlstm",
    )(z3, z3, z3, z3, zb, c0_all, n0r, m0b)
    return y, c1, n1.reshape(B, H, A_DK), m1[:, :, 0, 0]


_LOG_GAMMA = tuple(float(np.log(1.0 - 2.0 ** (-5.0 - h))) for h in range(D_HEADS))


def _ret_kernel(q_ref, k_ref, v_ref, g_ref, tab_ref, s0_ref, y_ref, s_out, s_s, *, L, mxu, bb, carry):
    if carry:
        @pl.when(pl.program_id(1) == 0)
        def _():
            s_s[...] = s0_ref[0]

    H = D_HEADS
    X = H * bb
    cos = tab_ref[:, 0:D_DK]
    sin = tab_ref[:, D_DK:2 * D_DK]
    row = lax.broadcasted_iota(jnp.int32, (L, L), 0)
    col = lax.broadcasted_iota(jnp.int32, (L, L), 1)
    diff = (row - col).astype(F32)
    jcol = lax.broadcasted_iota(jnp.int32, (L, 1), 0).astype(F32)

    def per_head(fn, shape):
        return jnp.concatenate([jnp.broadcast_to(fn(_LOG_GAMMA[h])[None], (bb,) + shape) for h in range(H)], axis=0)

    decay = per_head(lambda lg: jnp.where(diff >= 0.0, jnp.exp(jnp.maximum(diff, 0.0) * lg), 0.0), (L, L))
    w_in = per_head(lambda lg: jnp.exp((jcol + 1.0) * lg), (L, 1))
    w_st = per_head(lambda lg: jnp.exp((L - 1.0 - jcol) * lg), (L, 1))
    w_S = per_head(lambda lg: jnp.full((1, 1), float(np.exp(L * lg)), F32), (1, 1))

    def rotary(x):
        swapped = pltpu.roll(x.reshape(X * L, D_DK), D_DK // 2, 1).reshape(X, L, D_DK)
        return x * cos + swapped * sin

    qh = rotary(_stack_heads(q_ref, H, D_DK))
    kh = rotary(_stack_heads(k_ref, H, D_DK)) * (D_DK ** -0.5)
    vb = _stack_heads(v_ref, H, D_DV).astype(mxu)
    qb = qh.astype(mxu)
    s = _bdot_nt(qb, kh.astype(mxu)) * decay
    s_prev = s_s[...] if carry else jnp.concatenate([s0_ref[:, h] for h in range(H)], axis=0)
    o = _bdot(s.astype(mxu), vb) + w_in * _bdot(qb, s_prev.astype(mxu))
    y = _silu(_stack_heads(g_ref, H, D_DV)) * _ln_rows(o)
    s_new = w_S * s_prev + _bdot_tn((kh * w_st).astype(mxu), vb)
    for h in range(H):
        rows = slice(h * bb, (h + 1) * bb)
        y_ref[:, :, h * D_DV:(h + 1) * D_DV] = y[rows].astype(y_ref.dtype)
        if not carry:
            s_out[:, h] = s_new[rows]
    if carry:
        s_s[...] = s_new

        @pl.when(pl.program_id(1) == pl.num_programs(1) - 1)
        def _():
            s_out[0] = s_new


def _retention(z3, tab, s0_all, layer, ydt):
    B, T, _ = z3.shape
    L = min(CHUNK, T)
    nc = T // L
    carry = nc > 1
    bb = 1 if carry else _tile(B, SMALL_SEQ_BATCH, 1)
    H = D_HEADS
    W = H * D_DK
    db = Z_D // W
    mxu = BF16 if L >= 16 else F32

    def zspec(k):
        return pl.BlockSpec((bb, L, W), lambda b, c: (b, c, db + k))

    st_spec = pl.BlockSpec((bb, H, D_DK, D_DV), lambda b, c: (b, 0, 0, 0))
    return pl.pallas_call(
        functools.partial(_ret_kernel, L=L, mxu=mxu, bb=bb, carry=carry),
        out_shape=(jax.ShapeDtypeStruct((B, T, W), ydt),
                   jax.ShapeDtypeStruct((B, H, D_DK, D_DV), F32)),
        grid=(B // bb, nc),
        in_specs=[zspec(0), zspec(1), zspec(2), zspec(3),
                  pl.BlockSpec((L, 2 * D_DK), lambda b, c: (c, 0)),
                  pl.BlockSpec((None, bb, H, D_DK, D_DV), lambda b, c: (layer, b, 0, 0, 0))],
        out_specs=(pl.BlockSpec((bb, L, W), lambda b, c: (b, c, 0)), st_spec),
        scratch_shapes=[pltpu.VMEM((H, D_DK, D_DV), F32)],
        compiler_params=_cp(("parallel", "arbitrary")),
        name="retention",
    )(z3, z3, z3, z3, tab, s0_all)


_CPAD = 32


_SUB = 8
_CROWS = 64


def _conv_kernel(glu_ref, buf_ref, w_ref, b_ref, g_ref, be_ref, y_ref, buf_out, ext, *shifted, tt):
    ti = pl.program_id(1)
    lead = _CPAD - (C_WIDTH - 1)
    bb = ext.shape[0]

    @pl.when(ti == 0)
    def _():
        ext[:, 0:lead, :] = jnp.zeros((bb, lead, C_CH), F32)
        ext[:, lead:_CPAD, :] = buf_ref[...]
        if shifted:
            ext[:, _CPAD + tt:, :] = jnp.zeros((bb, ext.shape[1] - _CPAD - tt, C_CH), F32)

    glu = glu_ref[...].astype(F32)
    ext[:, _CPAD:_CPAD + tt, :] = glu[:, :, :C_CH] * _sigmoid(glu[:, :, C_CH:])

    def finish(acc):
        return _silu(_ln_rows(acc + b_ref[...]) * g_ref[...] + be_ref[...]).astype(y_ref.dtype)

    if shifted:
        sh = shifted[0]
        for r in range(1, _SUB):
            sh[r - 1] = ext[:, r:r + tt + _CPAD, :]
        for c0 in range(0, tt, _CROWS):
            acc = jnp.zeros((bb, _CROWS, C_CH), F32)
            for j in range(C_WIDTH):
                r = (lead + j) % _SUB
                a = lead + j - r + c0
                tap = ext[:, a:a + _CROWS, :] if r == 0 else sh[r - 1, :, a:a + _CROWS, :]
                acc = acc + tap * w_ref[j:j + 1, :]
            y_ref[:, c0:c0 + _CROWS, :] = finish(acc)
    else:
        acc = jnp.zeros((bb, tt, C_CH), F32)
        for j in range(C_WIDTH):
            acc = acc + ext[:, lead + j:lead + j + tt, :] * w_ref[j:j + 1, :]
        y_ref[...] = finish(acc)

    @pl.when(ti == pl.num_programs(1) - 1)
    def _():
        buf_out[...] = ext[:, tt + lead:tt + _CPAD, :]

    ext[:, 0:_CPAD, :] = ext[:, tt:tt + _CPAD, :]


def _conv(z3, buf0, w, b, g, be, ydt):
    B, T, _ = z3.shape
    bb, tt = _token_blocks(B, T, 512)
    nt = T // tt
    use_shifted = tt % _CROWS == 0
    ext_rows = tt + _CPAD + (_SUB if use_shifted else 0)
    scratch = [pltpu.VMEM((bb, ext_rows, C_CH), F32)]
    if use_shifted:
        scratch.append(pltpu.VMEM((_SUB - 1, bb, tt + _CPAD, C_CH), F32))
    vec = lambda a: a.reshape(1, C_CH)
    return pl.pallas_call(
        functools.partial(_conv_kernel, tt=tt),
        out_shape=(jax.ShapeDtypeStruct((B, T, C_CH), ydt),
                   jax.ShapeDtypeStruct((B, C_WIDTH - 1, C_CH), F32)),
        grid=(B // bb, nt),
        in_specs=[pl.BlockSpec((bb, tt, 2 * C_CH), lambda b, t: (b, t, Z_C // (2 * C_CH))),
                  pl.BlockSpec((bb, C_WIDTH - 1, C_CH), lambda b, t: (b, 0, 0)),
                  pl.BlockSpec((C_WIDTH, C_CH), lambda b, t: (0, 0)),
                  pl.BlockSpec((1, C_CH), lambda b, t: (0, 0)),
                  pl.BlockSpec((1, C_CH), lambda b, t: (0, 0)),
                  pl.BlockSpec((1, C_CH), lambda b, t: (0, 0))],
        out_specs=(pl.BlockSpec((bb, tt, C_CH), lambda b, t: (b, t, 0)),
                   pl.BlockSpec((bb, C_WIDTH - 1, C_CH), lambda b, t: (b, 0, 0))),
        scratch_shapes=scratch,
        compiler_params=_cp(("parallel", "arbitrary")),
        name="conv_module",
    )(z3, buf0, w, vec(b), vec(g), vec(be))


def _mla_common(zb_ref, tab, qn_ref, kvn_ref, wq1_ref, wq2_ref, rows_ref, q_scale):
    zb = zb_ref[...].reshape(-1, Z_BW)
    bq = zb[:, 0:B_Q_RANK]
    bkv = zb[:, B_Q_RANK:B_Q_RANK + B_KV_RANK]
    misc = zb[:, B_Q_RANK + B_KV_RANK:]
    qn = (_rms_rows(bq) * qn_ref[...]).astype(BF16)
    ckv = _rms_rows(bkv) * kvn_ref[...]
    cos_q = jnp.concatenate([tab[:, 0:LANE]] * B_HEADS, axis=1)
    sin_q = jnp.concatenate([tab[:, LANE:2 * LANE]] * B_HEADS, axis=1)
    q = (_dot(qn, wq1_ref[...]) * cos_q + _dot(qn, wq2_ref[...]) * sin_q) * q_scale
    kpe = misc * tab[:, 2 * LANE:3 * LANE] + pltpu.roll(misc, LANE // 2, 1) * tab[:, 3 * LANE:]
    lead = rows_ref.shape[:-1]
    rows_ref[:, :, 0:B_KV_RANK] = ckv.reshape(lead + (B_KV_RANK,))
    rows_ref[:, :, B_KV_RANK:CACHE_W] = kpe[:, 0:B_ROPE].reshape(lead + (B_ROPE,))
    return q, ckv, kpe


def _mla_prep_prompt_kernel(zb_ref, tab_ref, qn_ref, kvn_ref, wq1_ref, wq2_ref, wk_ref, wv_ref,
                            q_ref, k_ref, v_ref, rows_ref):
    q, ckv, kpe = _mla_common(zb_ref, tab_ref[...], qn_ref, kvn_ref, wq1_ref, wq2_ref, rows_ref,
                              MLA_SCALE * LOG2E)
    q_ref[0] = q.astype(BF16)
    cb = ckv.astype(BF16)
    kpe_hi = pltpu.roll(kpe, LANE // 2, 1)
    k = _dot(cb, wk_ref[...]) + jnp.concatenate([kpe_hi] * B_HEADS, axis=1)
    k_ref[0] = k.astype(BF16)
    lane = lax.broadcasted_iota(jnp.int32, (1, B_HEADS * HP), 1)
    ones_col = jnp.where(lane % HP == B_VDIM, 1.0, 0.0)
    v_ref[0] = (_dot(cb, wv_ref[...]) + ones_col).astype(BF16)


def _mla_prep_prompt(z3, tab, qn, kvn, wq1, wq2, wk, wv):
    B, T, _ = z3.shape
    tt = _tile(T, 512)
    nt = T // tt
    HW = B_HEADS * HP
    full = lambda a: pl.BlockSpec(a.shape, lambda b, t: (0,) * a.ndim)
    big = pl.BlockSpec((1, tt, HW), lambda b, t: (b, t, 0))
    return pl.pallas_call(
        _mla_prep_prompt_kernel,
        out_shape=(jax.ShapeDtypeStruct((B, T, HW), BF16),) * 3
        + (jax.ShapeDtypeStruct((B, T, CACHE_W), F32),),
        grid=(B, nt),
        in_specs=[pl.BlockSpec((1, tt, Z_BW), lambda b, t: (b, t, 0)),
                  pl.BlockSpec((tt, 4 * LANE), lambda b, t: (t, 0)),
                  full(qn), full(kvn), full(wq1), full(wq2), full(wk), full(wv)],
        out_specs=(big, big, big, pl.BlockSpec((1, tt, CACHE_W), lambda b, t: (b, t, 0))),
        compiler_params=_cp(("parallel", "parallel")),
        name="mla_prep_prompt",
    )(z3, tab, qn, kvn, wq1, wq2, wk, wv)


def _flash_kernel(q_ref, k_ref, v_ref, o_ref, *, tq, nq):
    row = lax.broadcasted_iota(jnp.int32, (tq, tq), 0)
    col = lax.broadcasted_iota(jnp.int32, (tq, tq), 1)
    causal = col <= row
    for qi in range(nq):
        q = q_ref[0, qi * tq:(qi + 1) * tq, :]
        m = acc = None
        for kj in range(qi + 1):
            k = k_ref[0, kj * tq:(kj + 1) * tq, :]
            v = v_ref[0, kj * tq:(kj + 1) * tq, :]
            s = _dot_nt(q, k)
            if kj == qi:
                s = jnp.where(causal, s, -jnp.inf)
            s_max = jnp.max(s, axis=1, keepdims=True)
            if kj == 0:
                m = s_max
                acc = _dot(jnp.exp2(s - m).astype(BF16), v)
            else:
                m_new = jnp.maximum(m, s_max)
                acc = jnp.exp2(m - m_new) * acc + _dot(jnp.exp2(s - m_new).astype(BF16), v)
                m = m_new
        o_ref[0, qi * tq:(qi + 1) * tq, :] = (acc / acc[:, B_VDIM:B_VDIM + 1]).astype(o_ref.dtype)


def _flash(q, k, v):
    B, T, HW = q.shape
    tq = _tile(T, 512)
    spec = pl.BlockSpec((1, T, HP), lambda b, h: (b, 0, h))
    return pl.pallas_call(
        functools.partial(_flash_kernel, tq=tq, nq=T // tq),
        out_shape=jax.ShapeDtypeStruct((B, T, HW), BF16),
        grid=(B, B_HEADS),
        in_specs=[spec, spec, spec],
        out_specs=spec,
        compiler_params=_cp(("parallel", "parallel")),
        name="mla_prompt_attention",
    )(q, k, v)


def _mla_prep_sample_kernel(zb_ref, tab_ref, qn_ref, kvn_ref, wq1_ref, wq2_ref, wuk_ref,
                            q_ref, rows_ref):
    bb, _, tt, _ = q_ref.shape
    tab = jnp.broadcast_to(tab_ref[...], (bb, tt, 4 * LANE)).reshape(bb * tt, 4 * LANE)
    q, _, _ = _mla_common(zb_ref, tab, qn_ref, kvn_ref, wq1_ref, wq2_ref, rows_ref, MLA_SCALE)
    lane = lax.broadcasted_iota(jnp.int32, (1, HP), 1)
    for h in range(B_HEADS):
        qh = q[:, h * HP:(h + 1) * HP]
        q_abs = _dot(qh.astype(BF16), wuk_ref[h])
        q_pe = jnp.where(lane < B_ROPE, pltpu.roll(qh, LANE // 2, 1), 0.0)
        q_ref[:, h, :, 0:B_KV_RANK] = q_abs.reshape(bb, tt, B_KV_RANK)
        q_ref[:, h, :, B_KV_RANK:] = q_pe[:, 0:B_ROPE].reshape(bb, tt, B_ROPE)


def _mla_prep_sample(z3, tab, qn, kvn, wq1, wq2, wuk):
    B, T, _ = z3.shape
    bb = _tile(B, max(1, 512 // T), 1)
    full = lambda a: pl.BlockSpec(a.shape, lambda i: (0,) * a.ndim)
    return pl.pallas_call(
        _mla_prep_sample_kernel,
        out_shape=(jax.ShapeDtypeStruct((B, B_HEADS, T, CACHE_W), F32),
                   jax.ShapeDtypeStruct((B, T, CACHE_W), F32)),
        grid=(B // bb,),
        in_specs=[pl.BlockSpec((bb, T, Z_BW), lambda i: (i, 0, 0)),
                  pl.BlockSpec((1, T, 4 * LANE), lambda i: (0, 0, 0)),
                  full(qn), full(kvn), full(wq1), full(wq2), full(wuk)],
        out_specs=(pl.BlockSpec((bb, B_HEADS, T, CACHE_W), lambda i: (i, 0, 0, 0)),
                   pl.BlockSpec((bb, T, CACHE_W), lambda i: (i, 0, 0))),
        compiler_params=_cp(("parallel",)),
        name="mla_prep_sample",
    )(z3, tab, qn, kvn, wq1, wq2, wuk)


def _sattn_kernel(pt_ref, q_ref, new_ref, wv_ref, cache_ref, y_ref, kbuf, sem, *, layer, n_pages, n_rows, T):
    b = pl.program_id(0)
    R = B_HEADS * T

    def page_copy(row, g, slot):
        return pltpu.make_async_copy(cache_ref.at[layer, pt_ref[row, g]], kbuf.at[slot, g], sem.at[slot])

    def fetch(row, slot):
        for g in range(n_pages):
            page_copy(row, g, slot).start(priority=g % 2)

    ahead = kbuf.shape[0] - 1

    @pl.when(b == 0)
    def _():
        for row in range(min(ahead, n_rows)):
            fetch(row, row)

    @pl.when(b + ahead < n_rows)
    def _():
        fetch(b + ahead, (b + ahead) % (ahead + 1))

    slot = b % (ahead + 1)
    for g in range(n_pages):
        page_copy(b, g, slot).wait()

    q = q_ref[0].reshape(R, CACHE_W).astype(BF16)
    kt = jnp.concatenate([kbuf[slot, g].astype(BF16) for g in range(n_pages)], axis=1)
    pad = jnp.zeros((PAGE_SIZE - T, CACHE_W), F32)
    new = jnp.concatenate([new_ref[0], pad], axis=0).astype(BF16)
    r = lax.broadcasted_iota(jnp.int32, (R, PAGE_SIZE), 0)
    c = lax.broadcasted_iota(jnp.int32, (R, PAGE_SIZE), 1)
    s_past = _dot(q, kt)
    s_new = jnp.where(c <= r % T, _dot_nt(q, new), -jnp.inf)
    m = jnp.maximum(jnp.max(s_past, axis=1, keepdims=True), jnp.max(s_new, axis=1, keepdims=True))
    p_past = jnp.exp(s_past - m)
    p_new = jnp.exp(s_new - m)
    l = jnp.sum(p_past, axis=1, keepdims=True) + jnp.sum(p_new, axis=1, keepdims=True)
    acc = _dot_nt(p_past.astype(BF16), kt[0:B_KV_RANK, :]) + _dot(p_new.astype(BF16), new[:, 0:B_KV_RANK])
    o = acc / l
    for h in range(B_HEADS):
        oh = o[h * T:(h + 1) * T, :].astype(BF16)
        y_ref[0, :, h * HP:(h + 1) * HP] = _dot(oh, wv_ref[h]).astype(y_ref.dtype)


def _sample_attention(layer, q, new_rows, wv, cache_t, page_table, ydt):
    B, H, T, _ = q.shape
    n_pages = page_table.shape[1]
    grid_spec = pltpu.PrefetchScalarGridSpec(
        num_scalar_prefetch=1,
        grid=(B,),
        in_specs=[pl.BlockSpec((1, H, T, CACHE_W), lambda b, pt: (b, 0, 0, 0)),
                  pl.BlockSpec((1, T, CACHE_W), lambda b, pt: (b, 0, 0)),
                  pl.BlockSpec(wv.shape, lambda b, pt: (0, 0, 0)),
                  pl.BlockSpec(memory_space=pl.ANY)],
        out_specs=pl.BlockSpec((1, T, H * HP), lambda b, pt: (b, 0, 0)),
        scratch_shapes=[pltpu.VMEM((GATHER_SLOTS, n_pages, CACHE_W, PAGE_SIZE), F32),
                        pltpu.SemaphoreType.DMA((GATHER_SLOTS,))],
    )
    return pl.pallas_call(
        functools.partial(_sattn_kernel, layer=layer, n_pages=n_pages, n_rows=B, T=T),
        out_shape=jax.ShapeDtypeStruct((B, T, H * HP), ydt),
        grid_spec=grid_spec,
        compiler_params=_cp(("arbitrary",)),
        name="mla_sample_attention",
    )(page_table, q, new_rows, wv, cache_t)


def _merge_kernel(ya_ref, yb_ref, yc_ref, yd_ref, gt_ref, x_ref, g1_ref, wa_ref, wb_ref, wc_ref, wd_ref,
                  wo_ref, lg_ref, lb_ref, o_ref):
    rows = o_ref.shape[0] * o_ref.shape[1]

    def flat(ref):
        return ref[...].reshape(rows, ref.shape[-1])

    gates = flat(gt_ref).astype(F32)
    acc = None
    for n, (y_ref, w_ref) in enumerate(((ya_ref, wa_ref), (yb_ref, wb_ref), (yc_ref, wc_ref), (yd_ref, wd_ref))):
        term = _sigmoid(gates[:, n * D_MODEL:(n + 1) * D_MODEL]) * _dot(flat(y_ref).astype(BF16), w_ref[...])
        acc = term if acc is None else acc + term
    mix = _dot(acc.astype(BF16), wo_ref[...]).reshape(o_ref.shape)
    v = DEEPNORM_ALPHA * x_ref[...] + g1_ref[...] * mix
    o_ref[...] = _ln_rows(v) * lg_ref[...] + lb_ref[...]


def _merge(ya, yb, yc, yd, z3, x, g1, wa, wb, wc, wd, wo, lg, lb):
    B, T, D = x.shape
    bb, tt = _token_blocks(B, T, 512)
    nt = T // tt
    tok = lambda w: pl.BlockSpec((bb, tt, w), lambda i: (i // nt, i % nt, 0))
    full = lambda a: pl.BlockSpec(a.shape, lambda i: (0,) * a.ndim)
    vec = lambda a: a.reshape(1, 1, D)
    return pl.pallas_call(
        _merge_kernel,
        out_shape=jax.ShapeDtypeStruct((B, T, D), F32),
        grid=((B // bb) * nt,),
        in_specs=[tok(ya.shape[-1]), tok(yb.shape[-1]), tok(yc.shape[-1]), tok(yd.shape[-1]),
                  pl.BlockSpec((bb, tt, N_BRANCH * D), lambda i: (i // nt, i % nt, Z_G // (N_BRANCH * D_MODEL))),
                  tok(D),
                  pl.BlockSpec((bb, 1, D), lambda i: (i // nt, 0, 0)),
                  full(wa), full(wb), full(wc), full(wd), full(wo),
                  pl.BlockSpec((1, 1, D), lambda i: (0, 0, 0)), pl.BlockSpec((1, 1, D), lambda i: (0, 0, 0))],
        out_specs=tok(D),
        compiler_params=_cp(("parallel",)),
        name="merge_out_ln",
    )(ya, yb, yc, yd, z3, x, g1, wa, wb, wc, wd, wo, vec(lg), vec(lb))


def _ffn_kernel(x_ref, sc_ref, sh_ref, g2_ref, w1_ref, w3_ref, w2_ref, lg_ref, lb_ref, o_ref, u_s, acc_s):
    j = pl.program_id(1)

    @pl.when(j == 0)
    def _():
        u = x_ref[...] * (1.0 + sc_ref[...]) + sh_ref[...]
        u_s[...] = u.reshape(u_s.shape).astype(BF16)
        acc_s[...] = jnp.zeros(acc_s.shape, F32)

    u = u_s[...]
    hid = _silu(_dot(u, w1_ref[...])) * _dot(u, w3_ref[...])
    acc_s[...] += _dot(hid.astype(BF16), w2_ref[...])

    @pl.when(j == pl.num_programs(1) - 1)
    def _():
        v = DEEPNORM_ALPHA * x_ref[...] + g2_ref[...] * acc_s[...].reshape(o_ref.shape)
        o_ref[...] = _ln_rows(v) * lg_ref[...] + lb_ref[...]


def _ffn(x, sc, sh, g2, w1, w3, w2, lg, lb):
    B, T, D = x.shape
    Hd = w1.shape[1]
    bb, tt = _token_blocks(B, T, 512)
    nt = T // tt
    tm = bb * tt
    th = _tile(Hd, 1408, LANE)
    tok = pl.BlockSpec((bb, tt, D), lambda i, j: (i // nt, i % nt, 0))
    per_b = pl.BlockSpec((bb, 1, D), lambda i, j: (i // nt, 0, 0))
    vspec = pl.BlockSpec((1, 1, D), lambda i, j: (0, 0, 0))
    vec = lambda a: a.reshape(1, 1, D)
    return pl.pallas_call(
        _ffn_kernel,
        out_shape=jax.ShapeDtypeStruct((B, T, D), F32),
        grid=((B // bb) * nt, Hd // th),
        in_specs=[tok, per_b, per_b, per_b,
                  pl.BlockSpec((D, th), lambda i, j: (0, j)),
                  pl.BlockSpec((D, th), lambda i, j: (0, j)),
                  pl.BlockSpec((th, D), lambda i, j: (j, 0)),
                  vspec, vspec],
        out_specs=tok,
        scratch_shapes=[pltpu.VMEM((tm, D), BF16), pltpu.VMEM((tm, D), F32)],
        compiler_params=_cp(("parallel", "arbitrary")),
        name="ffn_ln",
    )(x, sc, sh, g2, w1, w3, w2, vec(lg), vec(lb))


def _rope_tables(pos):
    posf = pos.astype(F32)[:, None]
    T = pos.shape[0]

    def cs(d):
        inv = ROPE_BASE ** (-jnp.arange(0, d, 2, dtype=F32) / d)
        ang = posf * inv[None, :]
        c, s = jnp.cos(ang), jnp.sin(ang)
        return jnp.concatenate([c, c], axis=1), jnp.concatenate([-s, s], axis=1)

    c32, s32 = cs(B_ROPE)
    z = lambda w: jnp.zeros((T, w), F32)
    cos_q = jnp.concatenate([jnp.ones((T, B_NOPE), F32), c32, z(HP - B_NOPE - B_ROPE)], axis=1)
    sin_q = jnp.concatenate([z(B_NOPE), s32, z(HP - B_NOPE - B_ROPE)], axis=1)
    cos_k = jnp.concatenate([c32, z(LANE - B_ROPE)], axis=1)
    sin_k = jnp.concatenate([s32, z(LANE - B_ROPE)], axis=1)
    tab_b = jnp.concatenate([cos_q, sin_q, cos_k, sin_k], axis=1)
    c128, s128 = cs(D_DK)
    return tab_b, jnp.concatenate([c128, s128], axis=1)


def _swap_halves(a, lo, width):
    half = width // 2
    return jnp.concatenate([a[..., lo + half:lo + width], a[..., lo:lo + half]], axis=-1)


def _layer_weights(l, w_in, b_in, mla_w_uq, mla_w_uk, mla_w_uv, w_branch):
    def main_cols(a):
        return jnp.concatenate([a[..., O_G:N_IN], a[..., O_AQ:O_AI], a[..., O_D:O_G], a[..., O_C:O_D]], axis=-1)

    def misc_cols(a):
        zeros = lambda w: jnp.zeros(a.shape[:-1] + (w,), a.dtype)
        return jnp.concatenate([
            a[..., O_BQ:O_BKR + B_ROPE], a[..., O_AI:O_BQ], zeros(MISC_KRSW - MISC_AF - A_HEADS),
            _swap_halves(a, O_BKR, B_ROPE), zeros(LANE - MISC_KRSW - B_ROPE)], axis=-1)

    w_main, w_misc = main_cols(w_in[l]).astype(BF16), misc_cols(w_in[l]).astype(BF16)
    b_main, b_misc = main_cols(b_in[l]).reshape(1, Z_MAIN), misc_cols(b_in[l]).reshape(1, Z_BW)

    hd = B_NOPE + B_ROPE
    wq = mla_w_uq[l].reshape(B_Q_RANK, B_HEADS, hd)
    zq = jnp.zeros((B_Q_RANK, B_HEADS, HP - hd), F32)
    wq1 = jnp.concatenate([wq, zq], axis=-1).reshape(B_Q_RANK, B_HEADS * HP).astype(BF16)
    wq2 = jnp.concatenate([jnp.zeros((B_Q_RANK, B_HEADS, B_NOPE), F32), _swap_halves(wq, B_NOPE, B_ROPE), zq],
                          axis=-1).reshape(B_Q_RANK, B_HEADS * HP).astype(BF16)
    wuk = mla_w_uk[l]
    wuv = mla_w_uv[l]
    wk = jnp.concatenate([wuk, jnp.zeros((B_KV_RANK, B_HEADS, HP - B_NOPE), F32)], axis=-1)
    wv = jnp.concatenate([wuv, jnp.zeros((B_KV_RANK, B_HEADS, HP - B_VDIM), F32)], axis=-1)
    wk_flat = wk.reshape(B_KV_RANK, B_HEADS * HP).astype(BF16)
    wv_flat = wv.reshape(B_KV_RANK, B_HEADS * HP).astype(BF16)
    wuk_t = jnp.transpose(wk, (1, 2, 0)).astype(BF16)
    wv_h = jnp.transpose(wv, (1, 0, 2)).astype(BF16)
    wb = w_branch[l]
    wb_b = jnp.concatenate([wb[1].reshape(B_HEADS, B_VDIM, D_MODEL),
                            jnp.zeros((B_HEADS, HP - B_VDIM, D_MODEL), F32)], axis=1)
    wb_b = wb_b.reshape(B_HEADS * HP, D_MODEL).astype(BF16)
    return dict(w_main=w_main, b_main=b_main, w_misc=w_misc, b_misc=b_misc, wq1=wq1, wq2=wq2, wk=wk_flat, wv=wv_flat, wuk_t=wuk_t, wv_h=wv_h,
                wb_a=wb[0].astype(BF16), wb_b=wb_b, wb_c=wb[2].astype(BF16), wb_d=wb[3].astype(BF16))


def kernel(x_prompt, x_sample, cache_mla, state_mlstm_C, state_mlstm_n, state_mlstm_m, state_conv, state_ret,
           page_table, c_prompt, c_sample, w_ada, b_ada, w_in, b_in, mla_q_norm, mla_w_uq, mla_kv_norm,
           mla_w_uk, mla_w_uv, conv_w, conv_b, conv_ln_g, conv_ln_b, w_branch, w_out, ln1_g, ln1_b,
           w_ffn1, w_ffn3, w_ffn2, ln2_g, ln2_b):
    dt = x_prompt.dtype
    Bp, Tp, D = x_prompt.shape
    Bd, Td, _ = x_sample.shape
    past_len = page_table.shape[1] * PAGE_SIZE
    cache_t = jnp.swapaxes(cache_mla, 2, 3)
    tabs_p = _rope_tables(jnp.arange(Tp))
    tabs_d = _rope_tables(past_len + jnp.arange(Td))
    c_all = jnp.concatenate([c_prompt, c_sample], axis=0).astype(F32)

    xp, xd = x_prompt.astype(F32), x_sample.astype(F32)
    p_states = [[] for _ in range(6)]
    d_states = [[] for _ in range(6)]
    for l in range(DEPTH):
        lw = _layer_weights(l, w_in, b_in, mla_w_uq, mla_w_uk, mla_w_uv, w_branch)
        qn = mla_q_norm[l].reshape(1, B_Q_RANK)
        kvn = mla_kv_norm[l].reshape(1, B_KV_RANK)
        wo = w_out[l].astype(BF16)
        w1, w3, w2 = w_ffn1[l].astype(BF16), w_ffn3[l].astype(BF16), w_ffn2[l].astype(BF16)
        mod = _ada_mod(c_all, w_ada.astype(F32), l, b_ada[l])

        def group(x, mod_g, tabs, st, sample):
            B, T, _ = x.shape
            sh1, sc1, g1, sh2, sc2, g2 = [mod_g[:, None, i * D:(i + 1) * D] for i in range(6)]
            ydt = F32 if sample else BF16
            z3, zb = _in_proj(x, sc1, sh1, lw['w_main'], lw['b_main'], lw['w_misc'], lw['b_misc'], ydt)
            (c0, c_layer), n0, m0, buf0, (s0, s_layer) = st
            ya, c1, n1, m1 = _mlstm(z3, zb, c0, c_layer, n0, m0, ydt)
            yd, s1 = _retention(z3, tabs[1], s0, s_layer, ydt)
            yc, buf1 = _conv(z3, buf0, conv_w[l], conv_b[l], conv_ln_g[l], conv_ln_b[l], ydt)
            if sample:
                q, rows = _mla_prep_sample(zb, tabs[0].reshape(1, T, 4 * LANE), qn, kvn,
                                           lw['wq1'], lw['wq2'], lw['wuk_t'])
                yb = _sample_attention(l, q, rows, lw['wv_h'], cache_t, page_table, ydt)
            else:
                q, k, v, rows = _mla_prep_prompt(zb, tabs[0], qn, kvn, lw['wq1'], lw['wq2'], lw['wk'], lw['wv'])
                yb = _flash(q, k, v)
            x1 = _merge(ya, yb, yc, yd, z3, x, g1, lw['wb_a'], lw['wb_b'], lw['wb_c'], lw['wb_d'], wo,
                        ln1_g[l], ln1_b[l])
            x2 = _ffn(x1, sc2, sh2, g2, w1, w3, w2, ln2_g[l], ln2_b[l])
            return x2, (rows, c1, n1, m1, buf1, s1)

        st_p = ((jnp.zeros((1, Bp, A_HEADS, A_DK, A_DV), F32), 0), jnp.zeros((Bp, A_HEADS, A_DK), F32),
                jnp.zeros((Bp, A_HEADS), F32), jnp.zeros((Bp, C_WIDTH - 1, C_CH), F32),
                (jnp.zeros((1, Bp, D_HEADS, D_DK, D_DV), F32), 0))
        xp, new_p = group(xp, mod[:Bp], tabs_p, st_p, False)
        st_d = ((state_mlstm_C.astype(F32), l), state_mlstm_n[l].astype(F32), state_mlstm_m[l].astype(F32),
                state_conv[l].astype(F32), (state_ret.astype(F32), l))
        xd, new_d = group(xd, mod[Bp:], tabs_d, st_d, True)
        for i in range(6):
            p_states[i].append(new_p[i])
            d_states[i].append(new_d[i])
    ps = [jnp.stack(s, axis=0).astype(dt) for s in p_states]
    ds = [jnp.stack(s, axis=0).astype(dt) for s in d_states]
    return (xp.astype(dt), xd.astype(dt), ps[0], ds[0], ps[1], ds[1], ps[2], ds[2], ps[3], ds[3],
            ps[4], ds[4], ps[5], ds[5])
```
